```python
import math
import jax, jax.numpy as jnp
from jax import lax
import numpy as np


D_MODEL = 2048
BATCH = 1
SEQ = 16384
DEPTH = 4
DEC_BATCH = 16
DEC_SEQ = 16
PAST_LEN = 4096

CHUNK = 64
N_META = 16
D_MIX = D_MODEL
ATTN_WIDTH = D_MIX // 2
SSM_WIDTH = D_MIX - ATTN_WIDTH
HEAD_DIM = 128
N_HEADS = ATTN_WIDTH // HEAD_DIM
SSM_CH = 16
SSM_GROUPS = SSM_WIDTH // SSM_CH
SSM_STATE = 64
D_FF = 5632
Q_BLOCK = 128
IN_COLS = 3 * ATTN_WIDTH + N_HEADS + SSM_WIDTH
MACARON_W = 0.5
NORM_EPS = 1e-6
ATTN_SCALE = HEAD_DIM ** -0.5
NEG_INF = -1e30

kernel_name = 'hybrid_fox_s5_streaming_encoder_step'


def rms_norm(x, g):
    xf = x.astype(jnp.float32)
    y = xf * lax.rsqrt(jnp.mean(xf * xf, axis=-1, keepdims=True) + NORM_EPS)
    return (y * g.astype(jnp.float32)).astype(x.dtype)


def ffn_sublayer(x, g_pre, g_post, w_gate, w_up, w_down):
    h = rms_norm(x, g_pre)
    h = (jax.nn.silu(h @ w_gate) * (h @ w_up)) @ w_down
    return x + MACARON_W * rms_norm(h, g_post)


def fox_attend(q, k, v, fq, fk, qpos, kpos):
    s = jnp.einsum('bqhd,bkhd->bhqk', q, k) * ATTN_SCALE + (fq[..., :, None] - fk[..., None, :])
    s = jnp.where((kpos[None, :] <= qpos[:, None])[None, None], s, NEG_INF)
    p = jax.nn.softmax(s, axis=-1)
    return jnp.einsum('bhqk,bkhd->bqhd', p, v)


def fox_attention_prompt(q, k, v, logf):
    bsz, t = q.shape[:2]
    fcum = jnp.cumsum(logf, axis=1)
    n_blk = -(-t // Q_BLOCK)
    pad = n_blk * Q_BLOCK - t
    qp = jnp.pad(q.astype(jnp.float32), ((0, 0), (0, pad), (0, 0), (0, 0)))
    fqp = jnp.pad(fcum, ((0, 0), (0, pad), (0, 0)))
    q_blocks = qp.reshape(bsz, n_blk, Q_BLOCK, N_HEADS, HEAD_DIM).transpose(1, 0, 2, 3, 4)
    fq_blocks = fqp.reshape(bsz, n_blk, Q_BLOCK, N_HEADS).transpose(1, 0, 3, 2)
    pos_blocks = jnp.arange(n_blk * Q_BLOCK, dtype=jnp.int32).reshape(n_blk, Q_BLOCK)
    kf = k.astype(jnp.float32)
    vf = v.astype(jnp.float32)
    fk = fcum.transpose(0, 2, 1)
    kpos = jnp.arange(t, dtype=jnp.int32)
    out = lax.map(lambda blk: fox_attend(blk[0], kf, vf, blk[1], fk, blk[2], kpos),
                  (q_blocks, fq_blocks, pos_blocks))
    out = out.transpose(1, 0, 2, 3, 4).reshape(bsz, n_blk * Q_BLOCK, N_HEADS, HEAD_DIM)
    return out[:, :t]


def fox_attention_sample(q, k, v, logf, ck, cv, clogf):
    past, s = ck.shape[1], q.shape[1]
    k_all = jnp.concatenate([ck.astype(jnp.float32), k.astype(jnp.float32)], axis=1)
    v_all = jnp.concatenate([cv.astype(jnp.float32), v.astype(jnp.float32)], axis=1)
    f_all = jnp.cumsum(jnp.concatenate([clogf.astype(jnp.float32), logf], axis=1), axis=1).transpose(0, 2, 1)
    qpos = past + jnp.arange(s, dtype=jnp.int32)
    kpos = jnp.arange(past + s, dtype=jnp.int32)
    return fox_attend(q.astype(jnp.float32), k_all, v_all, f_all[:, :, past:], f_all, qpos, kpos)


def linear_recurrence_combine(e1, e2):
    a1, b1 = e1
    a2, b2 = e2
    return a1 * a2, a2 * b1 + b2


def s5_mixer(u, h0_re, h0_im, a_re, a_im, log_dt, b_re, b_im, c_re, c_im, d_skip, w_glu):
    bsz, t = u.shape[:2]
    uf = u.astype(jnp.float32).reshape(bsz, t, SSM_GROUPS, SSM_CH)
    lam = lax.complex(a_re.astype(jnp.float32), a_im.astype(jnp.float32))
    dt = jnp.exp(log_dt.astype(jnp.float32))[:, None]
    a_bar = jnp.exp(lam * dt)
    b_bar = ((a_bar - 1.0) / lam)[..., None] * lax.complex(b_re.astype(jnp.float32), b_im.astype(jnp.float32))
    bu = jnp.einsum('gpc,btgc->btgp', b_bar, uf.astype(jnp.complex64))
    h0 = lax.complex(h0_re.astype(jnp.float32), h0_im.astype(jnp.float32))
    bu = bu.at[:, 0].add(a_bar * h0)
    a_seq = jnp.broadcast_to(a_bar, bu.shape)
    _, h = lax.associative_scan(linear_recurrence_combine, (a_seq, bu), axis=1)
    c = lax.complex(c_re.astype(jnp.float32), c_im.astype(jnp.float32))
    y = jnp.real(jnp.einsum('gcp,btgp->btgc', c, h)) + d_skip.astype(jnp.float32) * uf
    y = jax.nn.gelu(y)
    y = y * jax.nn.sigmoid(jnp.einsum('btgc,gce->btge', y, w_glu.astype(jnp.float32)))
    h_last = h[:, -1]
    return y.reshape(bsz, t, SSM_WIDTH).astype(u.dtype), jnp.real(h_last), jnp.imag(h_last)


def mixer_sublayer(x, g_pre, g_post, w_in, b_forget, ssm_params, g_attn_out, g_ssm_out, w_out, past):
    bsz, t, _ = x.shape
    h = rms_norm(x, g_pre)
    z = h @ w_in
    q = z[..., :ATTN_WIDTH].reshape(bsz, t, N_HEADS, HEAD_DIM)
    k = z[..., ATTN_WIDTH:2 * ATTN_WIDTH].reshape(bsz, t, N_HEADS, HEAD_DIM)
    v = z[..., 2 * ATTN_WIDTH:3 * ATTN_WIDTH].reshape(bsz, t, N_HEADS, HEAD_DIM)
    f_logit = z[..., 3 * ATTN_WIDTH:3 * ATTN_WIDTH + N_HEADS]
    u = z[..., 3 * ATTN_WIDTH + N_HEADS:]
    logf = jax.nn.log_sigmoid(f_logit.astype(jnp.float32) + b_forget.astype(jnp.float32))
    if past is None:
        attn = fox_attention_prompt(q, k, v, logf)
        h0_re = jnp.zeros((bsz, SSM_GROUPS, SSM_STATE), jnp.float32)
        h0_im = jnp.zeros((bsz, SSM_GROUPS, SSM_STATE), jnp.float32)
    else:
        ck, cv, clogf, h0_re, h0_im = past
        attn = fox_attention_sample(q, k, v, logf, ck, cv, clogf)
    ssm, h_re, h_im = s5_mixer(u, h0_re, h0_im, *ssm_params)
    attn = rms_norm(attn.reshape(bsz, t, ATTN_WIDTH).astype(x.dtype), g_attn_out)
    ssm = rms_norm(ssm, g_ssm_out)
    mixed = jnp.concatenate([attn, ssm], axis=-1) @ w_out
    x = x + rms_norm(mixed, g_post)
    return x, (k, v, logf, h_re, h_im)


def run_trunk(x, past, weights):
    (f1_pre, f1_post, f1_g, f1_u, f1_d, m_pre, m_post, w_in, b_forget,
     a_re, a_im, log_dt, b_re, b_im, c_re, c_im, d_skip, w_glu,
     g_ao, g_so, w_out, f2_pre, f2_post, f2_g, f2_u, f2_d) = weights
    new = ([], [], [], [], [])
    for l in range(DEPTH):
        x = ffn_sublayer(x, f1_pre[l], f1_post[l], f1_g[l], f1_u[l], f1_d[l])
        layer_past = None if past is None else tuple(p[l] for p in past)
        ssm_params = (a_re[l], a_im[l], log_dt[l], b_re[l], b_im[l], c_re[l], c_im[l], d_skip[l], w_glu[l])
        x, st = mixer_sublayer(x, m_pre[l], m_post[l], w_in[l], b_forget[l], ssm_params,
                               g_ao[l], g_so[l], w_out[l], layer_past)
        for lst, s in zip(new, st):
            lst.append(s)
        x = ffn_sublayer(x, f2_pre[l], f2_post[l], f2_g[l], f2_u[l], f2_d[l])
    return x, tuple(jnp.stack(lst) for lst in new)


def setup_inputs(seed: int = 0) -> dict:
    key = jax.random.key(seed)
    ks = jax.random.split(key, 40)

    def nrm(i, shape, scale):
        return scale * jax.random.normal(ks[i], shape, jnp.float32)

    def gain(i, shape):
        return 1.0 + 0.02 * jax.random.normal(ks[i], shape, jnp.float32)

    def unif(i, shape, lo, hi):
        return jax.random.uniform(ks[i], shape, jnp.float32, lo, hi)

    G, P, C = SSM_GROUPS, SSM_STATE, SSM_CH
    logf_shape = (DEPTH, DEC_BATCH, PAST_LEN, N_HEADS)
    return {
        'x_prompt': nrm(0, (BATCH, SEQ, D_MODEL), 1.0),
        'x_sample': nrm(1, (DEC_BATCH, DEC_SEQ, D_MODEL), 1.0),
        'cache_k': nrm(2, (DEPTH, DEC_BATCH, PAST_LEN, N_HEADS, HEAD_DIM), 1.0),
        'cache_v': nrm(3, (DEPTH, DEC_BATCH, PAST_LEN, N_HEADS, HEAD_DIM), 1.0),
        'cache_logf': jax.nn.log_sigmoid(unif(4, logf_shape, 1.0, 7.0) + nrm(5, logf_shape, 1.0)),
        'state_ssm_re': nrm(6, (DEPTH, DEC_BATCH, G, P), 0.1),
        'state_ssm_im': nrm(7, (DEPTH, DEC_BATCH, G, P), 0.1),
        'meta_tokens': nrm(8, (N_META, D_MODEL), 1.0),
        'ffn1_norm_pre': gain(9, (DEPTH, D_MODEL)),
        'ffn1_norm_post': gain(10, (DEPTH, D_MODEL)),
        'ffn1_w_gate': nrm(11, (DEPTH, D_MODEL, D_FF), D_MODEL ** -0.5),
        'ffn1_w_up': nrm(12, (DEPTH, D_MODEL, D_FF), D_MODEL ** -0.5),
        'ffn1_w_down': nrm(13, (DEPTH, D_FF, D_MODEL), D_FF ** -0.5),
        'mix_norm_pre': gain(14, (DEPTH, D_MODEL)),
        'mix_norm_post': gain(15, (DEPTH, D_MODEL)),
        'w_in': nrm(16, (DEPTH, D_MODEL, IN_COLS), D_MODEL ** -0.5),
        'b_forget': unif(17, (DEPTH, N_HEADS), 1.0, 7.0),
        'ssm_a_re': -0.5 * jnp.exp(nrm(18, (DEPTH, G, P), 0.05)),
        'ssm_a_im': jnp.pi * jnp.arange(P, dtype=jnp.float32) + nrm(19, (DEPTH, G, P), 0.05),
        'ssm_log_dt': unif(20, (DEPTH, G), math.log(1e-3), math.log(1e-1)),
        'ssm_b_re': nrm(21, (DEPTH, G, P, C), (2 * C) ** -0.5),
        'ssm_b_im': nrm(22, (DEPTH, G, P, C), (2 * C) ** -0.5),
        'ssm_c_re': nrm(23, (DEPTH, G, C, P), 0.5),
        'ssm_c_im': nrm(24, (DEPTH, G, C, P), 0.5),
        'ssm_d': nrm(25, (DEPTH, G, C), 1.0),
        'ssm_w_glu': nrm(26, (DEPTH, G, C, C), C ** -0.5),
        'attn_out_norm': gain(27, (DEPTH, ATTN_WIDTH)),
        'ssm_out_norm': gain(28, (DEPTH, SSM_WIDTH)),
        'w_out': nrm(29, (DEPTH, D_MIX, D_MODEL), D_MIX ** -0.5),
        'ffn2_norm_pre': gain(30, (DEPTH, D_MODEL)),
        'ffn2_norm_post': gain(31, (DEPTH, D_MODEL)),
        'ffn2_w_gate': nrm(32, (DEPTH, D_MODEL, D_FF), D_MODEL ** -0.5),
        'ffn2_w_up': nrm(33, (DEPTH, D_MODEL, D_FF), D_MODEL ** -0.5),
        'ffn2_w_down': nrm(34, (DEPTH, D_FF, D_MODEL), D_FF ** -0.5),
    }


def reference(x_prompt, x_sample, cache_k, cache_v, cache_logf, state_ssm_re, state_ssm_im, meta_tokens,
              ffn1_norm_pre, ffn1_norm_post, ffn1_w_gate, ffn1_w_up, ffn1_w_down,
              mix_norm_pre, mix_norm_post, w_in, b_forget,
              ssm_a_re, ssm_a_im, ssm_log_dt, ssm_b_re, ssm_b_im, ssm_c_re, ssm_c_im, ssm_d, ssm_w_glu,
              attn_out_norm, ssm_out_norm, w_out,
              ffn2_norm_pre, ffn2_norm_post, ffn2_w_gate, ffn2_w_up, ffn2_w_down):
    weights = (ffn1_norm_pre, ffn1_norm_post, ffn1_w_gate, ffn1_w_up, ffn1_w_down,
               mix_norm_pre, mix_norm_post, w_in, b_forget,
               ssm_a_re, ssm_a_im, ssm_log_dt, ssm_b_re, ssm_b_im, ssm_c_re, ssm_c_im, ssm_d, ssm_w_glu,
               attn_out_norm, ssm_out_norm, w_out,
               ffn2_norm_pre, ffn2_norm_post, ffn2_w_gate, ffn2_w_up, ffn2_w_down)
    meta = jnp.broadcast_to(meta_tokens[None].astype(x_prompt.dtype), (x_prompt.shape[0], N_META, D_MODEL))
    xp = jnp.concatenate([meta, x_prompt], axis=1)
    yp, (k_p, v_p, logf_p, sre_p, sim_p) = run_trunk(xp, None, weights)
    ys, (k_s, v_s, logf_s, sre_s, sim_s) = run_trunk(
        x_sample, (cache_k, cache_v, cache_logf, state_ssm_re, state_ssm_im), weights)
    return (yp[:, N_META:], ys, k_p, v_p, logf_p, sre_p, sim_p, k_s, v_s, logf_s, sre_s, sim_s)
```

```python
import functools
import math

import jax
import jax.numpy as jnp
from jax import lax
from jax.experimental import pallas as pl
from jax.experimental.pallas import tpu as pltpu

F32 = jnp.float32
BF16 = jnp.bfloat16

HEAD_DIM = 128
SSM_CH = 16
SSM_STATE = 64
S5_CHUNK = 16
LANES = 128
GROUPS_PER_TILE = LANES // SSM_CH
MACARON_W = 0.5
NORM_EPS = 1e-6
ATTN_SCALE = HEAD_DIM ** -0.5
NEG_INF = -1e30
VMEM_LIMIT = 56 * 1024 * 1024


def _cparams(*sem):
    return pltpu.CompilerParams(dimension_semantics=sem, vmem_limit_bytes=VMEM_LIMIT)


def _rms(x, g):
    return x * lax.rsqrt(jnp.mean(x * x, axis=-1, keepdims=True) + NORM_EPS) * g


def _dot(a, b):
    return jnp.dot(a, b, preferred_element_type=F32)


def _split3(x):
    hi = x.astype(BF16)
    r1 = x - hi.astype(F32)
    mid = r1.astype(BF16)
    lo = (r1 - mid.astype(F32)).astype(BF16)
    return hi, mid, lo


def _dot_exact_lhs01(sel, x):
    hi, mid, lo = _split3(x)
    return _dot(sel, hi) + _dot(sel, mid) + _dot(sel, lo)


def _ffn_kernel(x_ref, gpre_ref, gpost_ref, wg_ref, wu_ref, wd_ref, o_ref, xn_ref, acc_ref):
    j = pl.program_id(1)

    @pl.when(j == 0)
    def _():
        xn_ref[...] = _rms(x_ref[...], gpre_ref[...]).astype(BF16)
        acc_ref[...] = jnp.zeros_like(acc_ref)

    xn = xn_ref[...]
    g = _dot(xn, wg_ref[...])
    u = _dot(xn, wu_ref[...])
    h = (g * jax.nn.sigmoid(g) * u).astype(BF16)
    acc_ref[...] += _dot(h, wd_ref[...])

    @pl.when(j == pl.num_programs(1) - 1)
    def _():
        o_ref[...] = x_ref[...] + MACARON_W * _rms(acc_ref[...], gpost_ref[...])


def _ffn(x, g_pre, g_post, wg, wu, wd, *, tm, tf):
    r, d = x.shape
    f = wg.shape[1]
    return pl.pallas_call(
        _ffn_kernel,
        grid=(r // tm, f // tf),
        in_specs=[
            pl.BlockSpec((tm, d), lambda i, j: (i, 0)),
            pl.BlockSpec((1, d), lambda i, j: (0, 0)),
            pl.BlockSpec((1, d), lambda i, j: (0, 0)),
            pl.BlockSpec((d, tf), lambda i, j: (0, j)),
            pl.BlockSpec((d, tf), lambda i, j: (0, j)),
            pl.BlockSpec((tf, d), lambda i, j: (j, 0)),
        ],
        out_specs=pl.BlockSpec((tm, d), lambda i, j: (i, 0)),
        out_shape=jax.ShapeDtypeStruct((r, d), F32),
        scratch_shapes=[pltpu.VMEM((tm, d), BF16), pltpu.VMEM((tm, d), F32)],
        compiler_params=_cparams("parallel", "arbitrary"),
        name="ffn",
    )(x, g_pre, g_post, wg, wu, wd)


def _in_kernel(x_ref, g_ref, w_ref, wf_ref, bf_ref, zf_ref, zb_ref, logf_ref, fcum_ref, xn_ref, carry_ref):
    i = pl.program_id(0)
    j = pl.program_id(1)
    tm = x_ref.shape[0]

    @pl.when(j == 0)
    def _():
        xn = _rms(x_ref[...], g_ref[...]).astype(BF16)
        xn_ref[...] = xn
        logf = jax.nn.log_sigmoid(_dot(xn, wf_ref[...]) + bf_ref[...])
        logf_ref[...] = logf

        @pl.when(i == 0)
        def _():
            carry_ref[...] = jnp.zeros_like(carry_ref)

        row = lax.broadcasted_iota(jnp.int32, (tm, tm), 0)
        col = lax.broadcasted_iota(jnp.int32, (tm, tm), 1)
        tril = (col <= row).astype(BF16)
        fcum = _dot_exact_lhs01(tril, logf) + carry_ref[...]
        fcum_ref[...] = fcum
        carry_ref[...] = fcum[tm - 1:tm, :]

    z = _dot(xn_ref[...], w_ref[0])
    zf_ref[0] = z
    zb_ref[0] = (z * jnp.where(j == 0, ATTN_SCALE, 1.0)).astype(BF16)


def _in_proj(x, g, w4, wf, bf, *, tm):
    r, d = x.shape
    nsec, _, wsec = w4.shape
    return pl.pallas_call(
        _in_kernel,
        grid=(r // tm, nsec),
        in_specs=[
            pl.BlockSpec((tm, d), lambda i, j: (i, 0)),
            pl.BlockSpec((1, d), lambda i, j: (0, 0)),
            pl.BlockSpec((1, d, wsec), lambda i, j: (j, 0, 0)),
            pl.BlockSpec((d, LANES), lambda i, j: (0, 0)),
            pl.BlockSpec((1, LANES), lambda i, j: (0, 0)),
        ],
        out_specs=[
            pl.BlockSpec((1, tm, wsec), lambda i, j: (j, i, 0)),
            pl.BlockSpec((1, tm, wsec), lambda i, j: (j, i, 0)),
            pl.BlockSpec((tm, LANES), lambda i, j: (i, 0)),
            pl.BlockSpec((tm, LANES), lambda i, j: (i, 0)),
        ],
        out_shape=[
            jax.ShapeDtypeStruct((nsec, r, wsec), F32),
            jax.ShapeDtypeStruct((nsec, r, wsec), BF16),
            jax.ShapeDtypeStruct((r, LANES), F32),
            jax.ShapeDtypeStruct((r, LANES), F32),
        ],
        scratch_shapes=[pltpu.VMEM((tm, d), BF16), pltpu.VMEM((1, LANES), F32)],
        compiler_params=_cparams("arbitrary", "arbitrary"),
        name="in_proj",
    )(x, g, w4, wf, bf)


def _attn_p_kernel(q_ref, k_ref, v_ref, fq_ref, fk_ref, o_ref, *, rep):
    h = pl.program_id(0)
    qi = pl.program_id(1)
    tq = q_ref.shape[0]
    q = q_ref[...]

    lane = lax.broadcasted_iota(jnp.int32, (tq, LANES), 1)
    fq = jnp.max(jnp.where(lane // rep == h, fq_ref[...], -jnp.inf), axis=-1, keepdims=True)

    def block(kb, carry, masked):
        m, l, acc = carry
        start = pl.multiple_of(kb * tq, tq)
        k = k_ref[pl.ds(start, tq), :]
        v = v_ref[pl.ds(start, tq), :]
        s = lax.dot_general(q, k, (((1,), (1,)), ((), ())), preferred_element_type=F32)
        t = s - fk_ref[0, pl.ds(kb, 1), :]
        if masked:
            row = lax.broadcasted_iota(jnp.int32, (tq, tq), 0)
            col = lax.broadcasted_iota(jnp.int32, (tq, tq), 1)
            t = jnp.where(col <= row, t, NEG_INF)
        m_new = jnp.maximum(m, jnp.max(t, axis=-1, keepdims=True) + fq)
        p = jnp.exp(t + (fq - m_new))
        alpha = jnp.exp(m - m_new)
        l = alpha * l + jnp.sum(p, axis=-1, keepdims=True)
        acc = alpha * acc + _dot(p.astype(BF16), v)
        return m_new, l, acc

    init = (jnp.full((tq, 1), NEG_INF, F32), jnp.zeros((tq, 1), F32), jnp.zeros((tq, HEAD_DIM), F32))
    carry = lax.fori_loop(0, qi, lambda kb, c: block(kb, c, False), init)
    m, l, acc = block(qi, carry, True)
    o_ref[...] = acc / l


def _attn_prompt(qkv, fcum, fcum_t, *, tq, rep):
    _, r, aw = qkv.shape
    nh = aw // HEAD_DIM
    nq = r // tq
    return pl.pallas_call(
        functools.partial(_attn_p_kernel, rep=rep),
        grid=(nh, nq),
        in_specs=[
            pl.BlockSpec((None, tq, HEAD_DIM), lambda h, i: (0, i, h)),
            pl.BlockSpec((None, r, HEAD_DIM), lambda h, i: (1, 0, h)),
            pl.BlockSpec((None, r, HEAD_DIM), lambda h, i: (2, 0, h)),
            pl.BlockSpec((tq, LANES), lambda h, i: (i, 0)),
            pl.BlockSpec((1, nq, tq), lambda h, i: (h, 0, 0)),
        ],
        out_specs=pl.BlockSpec((tq, HEAD_DIM), lambda h, i: (i, h)),
        out_shape=jax.ShapeDtypeStruct((r, aw), F32),
        compiler_params=_cparams("parallel", "arbitrary"),
        name="attn_p",
    )(qkv, qkv, qkv, fcum, fcum_t)


def _attn_s_kernel(qbd_ref, ck_ref, cv_ref, cf_ref, kn_ref, vn_ref, fn_ref, attn_in_ref, o_ref,
                   m_ref, l_ref, acc_ref, suf_ref, cq_ref, *, rep, seg):
    del attn_in_ref
    step = pl.program_id(1)
    s_new = kn_ref.shape[0]
    tk = ck_ref.shape[0]
    aw = acc_ref.shape[1]
    nh = aw // HEAD_DIM
    qbd = qbd_ref[...]

    def update(t, v_bf):
        m_old = m_ref[...]
        m_new = jnp.maximum(m_old, jnp.max(t, axis=0, keepdims=True))
        p = jnp.exp(t - m_new)
        alpha = jnp.exp(m_old - m_new)
        l_ref[...] = alpha * l_ref[...] + jnp.sum(p, axis=0, keepdims=True)
        m_ref[...] = m_new
        alpha_col = jnp.transpose(jnp.broadcast_to(alpha, (LANES, LANES)))
        pv = _dot(jnp.transpose(p).astype(BF16), v_bf)
        for c in range(nh):
            sl = slice(c * HEAD_DIM, (c + 1) * HEAD_DIM)
            acc_ref[:, sl] = acc_ref[:, sl] * alpha_col + pv[:, sl]

    @pl.when(step == 0)
    def _():
        m_ref[...] = jnp.full_like(m_ref, NEG_INF)
        l_ref[...] = jnp.zeros_like(l_ref)
        acc_ref[...] = jnp.zeros_like(acc_ref)
        row = lax.broadcasted_iota(jnp.int32, (s_new, s_new), 0)
        col = lax.broadcasted_iota(jnp.int32, (s_new, s_new), 1)
        cn = _dot_exact_lhs01((col <= row).astype(BF16), fn_ref[...])
        krow = lax.broadcasted_iota(jnp.int32, (s_new, LANES), 0)
        qlane = lax.broadcasted_iota(jnp.int32, (s_new, LANES), 1) % rep
        cq = jnp.sum(jnp.where(krow == qlane, cn, 0.0), axis=0, keepdims=True)
        cq_ref[...] = cq
        suf_ref[...] = jnp.zeros_like(suf_ref)
        st = _dot(kn_ref[...].astype(BF16), qbd)
        t = jnp.where(krow <= qlane, st + (cq - cn), NEG_INF)
        update(t, vn_ref[...].astype(BF16))

    srow = lax.broadcasted_iota(jnp.int32, (seg, seg), 0)
    scol = lax.broadcasted_iota(jnp.int32, (seg, seg), 1)
    triu = (scol > srow).astype(BF16)
    carry = suf_ref[...]
    sufs = [None] * (tk // seg)
    for sidx in reversed(range(tk // seg)):
        x = cf_ref[sidx * seg:(sidx + 1) * seg, :]
        sfx = _dot_exact_lhs01(triu, x) + carry
        sufs[sidx] = sfx
        carry = sfx[0:1, :] + x[0:1, :]
    suf_ref[...] = carry
    bias = jnp.concatenate(sufs, axis=0) + cq_ref[...]
    st = _dot(ck_ref[...].astype(BF16), qbd)
    update(st + bias, cv_ref[...].astype(BF16))

    @pl.when(step == pl.num_programs(1) - 1)
    def _():
        l_col = jnp.transpose(jnp.broadcast_to(l_ref[...], (LANES, LANES)))
        for c in range(nh):
            rows = slice(c * rep, c * rep + s_new)
            sl = slice(c * HEAD_DIM, (c + 1) * HEAD_DIM)
            o_ref[:, sl] = acc_ref[rows, sl] / l_col[rows, :]


def _attn_sample(layer, qbd, ck, cv, cf, zf, logf, attn, *, np_rows, tk, rep):
    depth, nb, past, aw = ck.shape
    s_new = S5_CHUNK
    nkb = past // tk
    base = np_rows // s_new
    seg = min(tk, LANES)
    return pl.pallas_call(
        functools.partial(_attn_s_kernel, rep=rep, seg=seg),
        grid=(nb, nkb),
        in_specs=[
            pl.BlockSpec((None, aw, LANES), lambda b, s: (b, 0, 0)),
            pl.BlockSpec((None, None, tk, aw), lambda b, s: (layer, b, nkb - 1 - s, 0)),
            pl.BlockSpec((None, None, tk, aw), lambda b, s: (layer, b, nkb - 1 - s, 0)),
            pl.BlockSpec((None, None, tk, LANES), lambda b, s: (layer, b, nkb - 1 - s, 0)),
            pl.BlockSpec((None, s_new, aw), lambda b, s: (1, base + b, 0)),
            pl.BlockSpec((None, s_new, aw), lambda b, s: (2, base + b, 0)),
            pl.BlockSpec((s_new, LANES), lambda b, s: (base + b, 0)),
            pl.BlockSpec(memory_space=pl.ANY),
        ],
        out_specs=pl.BlockSpec((s_new, aw), lambda b, s: (base + b, 0)),
        out_shape=jax.ShapeDtypeStruct(attn.shape, F32),
        scratch_shapes=[
            pltpu.VMEM((1, LANES), F32), pltpu.VMEM((1, LANES), F32), pltpu.VMEM((LANES, aw), F32),
            pltpu.VMEM((1, LANES), F32), pltpu.VMEM((1, LANES), F32),
        ],
        input_output_aliases={7: 0},
        compiler_params=_cparams("parallel", "arbitrary"),
        name="attn_s",
    )(qbd, ck, cv, cf, zf, zf, logf, attn)


def _s5_kernel(u_ref, kbig_ref, w_ref, v_ref, a_ref, hinit_ref, d_ref, wglu_ref, o_ref, hend_ref,
               s_ref, hprev_ref, hc_ref, *, n_prompt_chunks):
    rt = pl.program_id(1)
    ct = s_ref.shape[0]
    half = s_ref.shape[1] // 2
    L = S5_CHUNK

    def u_at(tau):
        return u_ref[pl.ds(tau, ct, stride=L), :]

    x = jnp.concatenate([u_at(tau).astype(BF16) for tau in range(L)], axis=1)
    y = _dot(x, kbig_ref[...])
    s_ref[...] = _dot(x, w_ref[...])

    ar = a_ref[:, :half]
    ai = a_ref[:, half:]

    @pl.when(rt == 0)
    def _():
        hc_ref[...] = jnp.zeros_like(hc_ref)

    def scan(c, hcar):
        cg = rt * ct + c
        reset = jnp.logical_or(cg == 0, cg >= n_prompt_chunks)
        hp = jnp.where(reset, hinit_ref[pl.ds(c, 1), :], hcar)
        hprev_ref[pl.ds(c, 1), :] = hp
        s = s_ref[pl.ds(c, 1), :]
        hr = hp[:, :half]
        hi = hp[:, half:]
        hn = jnp.concatenate([ar * hr - ai * hi + s[:, :half], ar * hi + ai * hr + s[:, half:]], axis=1)
        hend_ref[pl.ds(c, 1), :] = hn
        return hn

    hc_ref[...] = lax.fori_loop(0, ct, scan, hc_ref[...])

    y = y + _dot(hprev_ref[...].astype(BF16), v_ref[...])
    d = d_ref[...]
    wglu = wglu_ref[...]
    for tau in range(L):
        yt = y[:, tau * LANES:(tau + 1) * LANES] + d * u_at(tau)
        yt = 0.5 * yt * (1.0 + jnp.tanh(math.sqrt(2.0 / math.pi) * (yt + 0.044715 * (yt * yt * yt))))
        gate = jax.nn.sigmoid(_dot(yt.astype(BF16), wglu))
        o_ref[pl.ds(tau, ct, stride=L), :] = yt * gate


def _s5(u, kbig, w, v, a16, hinit, d, wglu, *, rows_tile, n_prompt_chunks):
    r, sw = u.shape
    no = sw // LANES
    L = S5_CHUNK
    ct = rows_tile // L
    nrt = r // rows_tile
    sd = w.shape[2]
    once = dict(pipeline_mode=pl.Buffered(1))
    return pl.pallas_call(
        functools.partial(_s5_kernel, n_prompt_chunks=n_prompt_chunks),
        grid=(no, nrt),
        in_specs=[
            pl.BlockSpec((rows_tile, LANES), lambda o, t: (t, o)),
            pl.BlockSpec((None, L * LANES, L * LANES), lambda o, t: (o, 0, 0), **once),
            pl.BlockSpec((None, L * LANES, sd), lambda o, t: (o, 0, 0), **once),
            pl.BlockSpec((None, sd, L * LANES), lambda o, t: (o, 0, 0), **once),
            pl.BlockSpec((None, 1, sd), lambda o, t: (o, 0, 0)),
            pl.BlockSpec((ct, sd), lambda o, t: (t, o)),
            pl.BlockSpec((1, LANES), lambda o, t: (0, o)),
            pl.BlockSpec((None, LANES, LANES), lambda o, t: (o, 0, 0)),
        ],
        out_specs=[
            pl.BlockSpec((rows_tile, LANES), lambda o, t: (t, o)),
            pl.BlockSpec((ct, sd), lambda o, t: (t, o)),
        ],
        out_shape=[
            jax.ShapeDtypeStruct((r, sw), F32),
            jax.ShapeDtypeStruct((r // L, no * sd), F32),
        ],
        scratch_shapes=[pltpu.VMEM((ct, sd), F32), pltpu.VMEM((ct, sd), F32), pltpu.VMEM((1, sd), F32)],
        compiler_params=_cparams("parallel", "arbitrary"),
        name="s5",
    )(u, kbig, w, v, a16, hinit, d, wglu)


def _s5_operators(a_re, a_im, log_dt, b_re, b_im, c_re, c_im, w_glu):
    g, p = a_re.shape
    c = SSM_CH
    L = S5_CHUNK
    gt = GROUPS_PER_TILE
    no = g // gt
    hp = lax.Precision.HIGHEST
    lam = lax.complex(a_re, a_im)
    dt = jnp.exp(log_dt)[:, None]
    a_bar = jnp.exp(lam * dt)
    b_bar = ((a_bar - 1.0) / lam)[..., None] * lax.complex(b_re, b_im)
    cc = lax.complex(c_re, c_im)
    pows = [jnp.ones_like(a_bar)]
    for _ in range(L):
        pows.append(pows[-1] * a_bar)
    pows = jnp.stack(pows)

    def cdot(spec, x, y):
        rr = jnp.einsum(spec, jnp.real(x), jnp.real(y), precision=hp)
        ii = jnp.einsum(spec, jnp.imag(x), jnp.imag(y), precision=hp)
        return rr - ii

    ab = pows[:L, :, :, None] * b_bar[None]
    kd = cdot('gop,dgpi->gdio', cc, ab)
    eye = jnp.eye(gt, dtype=F32)
    ti = jnp.arange(L)[:, None]
    to = jnp.arange(L)[None, :]
    lag = to - ti
    kt = kd[:, jnp.clip(lag, 0, L - 1)] * (lag >= 0)[None, :, :, None, None].astype(F32)
    kt = kt.reshape(no, gt, L, L, c, c)
    kbig = jnp.einsum('xgabio,gh->xagibho', kt, eye).reshape(no, L * LANES, L * LANES)

    wc = pows[L - 1 - jnp.arange(L)][:, :, :, None] * b_bar[None]
    wc = wc.reshape(L, no, gt, p, c)
    w_re = jnp.einsum('txgpi,gh->xtgihp', jnp.real(wc), eye).reshape(no, L * LANES, gt * p)
    w_im = jnp.einsum('txgpi,gh->xtgihp', jnp.imag(wc), eye).reshape(no, L * LANES, gt * p)
    w = jnp.concatenate([w_re, w_im], axis=2)

    z = cc.transpose(0, 2, 1)[None] * pows[1:, :, :, None]
    z = z.reshape(L, no, gt, p, c)
    v_re = jnp.einsum('txgpo,gh->xgptho', jnp.real(z), eye).reshape(no, gt * p, L * LANES)
    v_im = jnp.einsum('txgpo,gh->xgptho', -jnp.imag(z), eye).reshape(no, gt * p, L * LANES)
    v = jnp.concatenate([v_re, v_im], axis=1)

    a16 = pows[L].reshape(no, 1, gt * p)
    a16 = jnp.concatenate([jnp.real(a16), jnp.imag(a16)], axis=2)
    wglu = jnp.einsum('xgce,gh->xgche', w_glu.reshape(no, gt, c, c), eye).reshape(no, LANES, LANES)
    return kbig.astype(BF16), w.astype(BF16), v.astype(BF16), a16, wglu.astype(BF16)


def _out_kernel(x_ref, attn_ref, ssm_ref, ga_ref, gs_ref, gpost_ref, wa_ref, ws_ref, o_ref):
    an = _rms(attn_ref[...], ga_ref[...]).astype(BF16)
    sn = _rms(ssm_ref[...], gs_ref[...]).astype(BF16)
    mixed = _dot(an, wa_ref[...]) + _dot(sn, ws_ref[...])
    o_ref[...] = x_ref[...] + _rms(mixed, gpost_ref[...])


def _out_proj(x, attn, ssm, ga, gs, gpost, wa, ws, *, tm):
    r, d = x.shape
    aw = attn.shape[1]
    sw = ssm.shape[1]
    once = dict(pipeline_mode=pl.Buffered(1))
    return pl.pallas_call(
        _out_kernel,
        grid=(r // tm,),
        in_specs=[
            pl.BlockSpec((tm, d), lambda i: (i, 0)),
            pl.BlockSpec((tm, aw), lambda i: (i, 0)),
            pl.BlockSpec((tm, sw), lambda i: (i, 0)),
            pl.BlockSpec((1, aw), lambda i: (0, 0)),
            pl.BlockSpec((1, sw), lambda i: (0, 0)),
            pl.BlockSpec((1, d), lambda i: (0, 0)),
            pl.BlockSpec((aw, d), lambda i: (0, 0), **once),
            pl.BlockSpec((sw, d), lambda i: (0, 0), **once),
        ],
        out_specs=pl.BlockSpec((tm, d), lambda i: (i, 0)),
        out_shape=jax.ShapeDtypeStruct((r, d), F32),
        compiler_params=_cparams("parallel"),
        name="out_proj",
    )(x, attn, ssm, ga, gs, gpost, wa, ws)


def _tiles(r_min):
    tm = 512
    r = -(-r_min // tm) * tm
    s5_tiles = 4 if (r // 4) % (8 * S5_CHUNK) == 0 else 1
    return dict(r=r, tm=tm, tq=tm, tf=512, s5_rows=r // s5_tiles, tk_cache=1024)


def _forward(x_prompt, x_sample, cache_k, cache_v, cache_logf, state_ssm_re, state_ssm_im, meta_tokens,
             ffn1_norm_pre, ffn1_norm_post, ffn1_w_gate, ffn1_w_up, ffn1_w_down,
             mix_norm_pre, mix_norm_post, w_in, b_forget,
             ssm_a_re, ssm_a_im, ssm_log_dt, ssm_b_re, ssm_b_im, ssm_c_re, ssm_c_im, ssm_d, ssm_w_glu,
             attn_out_norm, ssm_out_norm, w_out,
             ffn2_norm_pre, ffn2_norm_post, ffn2_w_gate, ffn2_w_up, ffn2_w_down, tiles=None):
    bsz, seq, d = x_prompt.shape
    nb, s_new, _ = x_sample.shape
    depth, _, past, nh, _ = cache_k.shape
    n_meta = meta_tokens.shape[0]
    aw = nh * HEAD_DIM
    g, p = ssm_a_re.shape[1:]
    sw = g * SSM_CH
    L = S5_CHUNK
    assert bsz == 1 and s_new == L and n_meta % L == 0 and seq % L == 0
    assert LANES % nh == 0 and s_new <= LANES // nh and g % GROUPS_PER_TILE == 0
    rep = LANES // nh
    no = g // GROUPS_PER_TILE
    sd = 2 * GROUPS_PER_TILE * p
    np_rows = n_meta + seq
    ns_rows = nb * s_new
    t = tiles or _tiles(np_rows + ns_rows)
    r = t["r"]
    npc = np_rows // L
    n_chunks = r // L

    x = jnp.concatenate([meta_tokens.astype(F32), x_prompt[0], x_sample.reshape(ns_rows, d),
                         jnp.zeros((r - np_rows - ns_rows, d), F32)], axis=0)

    ck = cache_k.reshape(depth, nb, past, aw)
    cv = cache_v.reshape(depth, nb, past, aw)
    cf = jnp.repeat(cache_logf, rep, axis=-1)

    row2 = lambda a: a.reshape(1, -1)
    outs = dict(k=[], v=[], logf=[], hend=[])
    for l in range(depth):
        x = _ffn(x, row2(ffn1_norm_pre[l]), row2(ffn1_norm_post[l]), ffn1_w_gate[l].astype(BF16),
                 ffn1_w_up[l].astype(BF16), ffn1_w_down[l].astype(BF16), tm=t["tm"], tf=t["tf"])

        wl = w_in[l]
        w4 = jnp.stack([wl[:, :aw], wl[:, aw:2 * aw], wl[:, 2 * aw:3 * aw], wl[:, 3 * aw + nh:]]).astype(BF16)
        wf = jnp.repeat(wl[:, 3 * aw:3 * aw + nh], rep, axis=1).astype(BF16)
        bf = jnp.repeat(b_forget[l], rep).reshape(1, LANES)
        zf, zb, logf, fcum = _in_proj(x, row2(mix_norm_pre[l]), w4, wf, bf, tm=t["tm"])

        fcum_t = fcum[:, ::rep].T.reshape(nh, r // t["tq"], t["tq"])
        attn = _attn_prompt(zb, fcum, fcum_t, tq=t["tq"], rep=rep)

        qs = zb[0, np_rows:np_rows + ns_rows].reshape(nb, s_new, nh, HEAD_DIM)
        qs = jnp.pad(qs.transpose(0, 2, 3, 1), ((0, 0), (0, 0), (0, 0), (0, rep - s_new)))
        qbd = jnp.einsum('bhdr,hg->bhdgr', qs, jnp.eye(nh, dtype=BF16)).reshape(nb, aw, LANES)
        attn = _attn_sample(l, qbd, ck, cv, cf, zf, logf, attn, np_rows=np_rows, tk=t["tk_cache"], rep=rep)

        kbig, wop, vop, a16, wglu = _s5_operators(ssm_a_re[l], ssm_a_im[l], ssm_log_dt[l], ssm_b_re[l],
                                                  ssm_b_im[l], ssm_c_re[l], ssm_c_im[l], ssm_w_glu[l])
        h0 = jnp.concatenate([state_ssm_re[l].reshape(nb, no, 1, sd // 2),
                              state_ssm_im[l].reshape(nb, no, 1, sd // 2)], axis=2).reshape(nb, no * sd)
        hinit = jnp.zeros((n_chunks, no * sd), F32).at[npc:npc + nb].set(h0)
        ssm, hend = _s5(zf[3], kbig, wop, vop, a16, hinit, row2(ssm_d[l]), wglu,
                        rows_tile=t["s5_rows"], n_prompt_chunks=npc)

        wo = w_out[l].astype(BF16)
        x = _out_proj(x, attn, ssm, row2(attn_out_norm[l]), row2(ssm_out_norm[l]), row2(mix_norm_post[l]),
                      wo[:aw], wo[aw:], tm=t["tm"])

        x = _ffn(x, row2(ffn2_norm_pre[l]), row2(ffn2_norm_post[l]), ffn2_w_gate[l].astype(BF16),
                 ffn2_w_up[l].astype(BF16), ffn2_w_down[l].astype(BF16), tm=t["tm"], tf=t["tf"])

        outs["k"].append(zf[1])
        outs["v"].append(zf[2])
        outs["logf"].append(logf[:, ::rep])
        outs["hend"].append(hend.reshape(n_chunks, no, 2, GROUPS_PER_TILE, p))

    k_all = jnp.stack(outs["k"])
    v_all = jnp.stack(outs["v"])
    f_all = jnp.stack(outs["logf"])
    h_all = jnp.stack(outs["hend"])
    sl_p = slice(0, np_rows)
    sl_s = slice(np_rows, np_rows + ns_rows)
    heads = lambda a, n: a.reshape(depth, -1, n, nh, HEAD_DIM)
    states = lambda a: a.reshape(depth, -1, g, p)
    return (x[n_meta:np_rows][None], x[sl_s].reshape(nb, s_new, d),
            heads(k_all[:, sl_p], np_rows), heads(v_all[:, sl_p], np_rows),
            f_all[:, sl_p].reshape(depth, 1, np_rows, nh),
            states(h_all[:, npc - 1:npc, :, 0]), states(h_all[:, npc - 1:npc, :, 1]),
            heads(k_all[:, sl_s], s_new), heads(v_all[:, sl_s], s_new),
            f_all[:, sl_s].reshape(depth, nb, s_new, nh),
            states(h_all[:, npc:npc + nb, :, 0]), states(h_all[:, npc:npc + nb, :, 1]))


def kernel(x_prompt, x_sample, cache_k, cache_v, cache_logf, state_ssm_re, state_ssm_im, meta_tokens, ffn1_norm_pre, ffn1_norm_post, ffn1_w_gate, ffn1_w_up, ffn1_w_down, mix_norm_pre, mix_norm_post, w_in, b_forget, ssm_a_re, ssm_a_im, ssm_log_dt, ssm_b_re, ssm_b_im, ssm_c_re, ssm_c_im, ssm_d, ssm_w_glu, attn_out_norm, ssm_out_norm, w_out, ffn2_norm_pre, ffn2_norm_post, ffn2_w_gate, ffn2_w_up, ffn2_w_down):
    return _forward(x_prompt, x_sample, cache_k, cache_v, cache_logf, state_ssm_re, state_ssm_im, meta_tokens,
                    ffn1_norm_pre, ffn1_norm_post, ffn1_w_gate, ffn1_w_up, ffn1_w_down,
                    mix_norm_pre, mix_norm_post, w_in, b_forget,
                    ssm_a_re, ssm_a_im, ssm_log_dt, ssm_b_re, ssm_b_im, ssm_c_re, ssm_c_im, ssm_d, ssm_w_glu,
                    attn_out_norm, ssm_out_norm, w_out,
                    ffn2_norm_pre, ffn2_norm_post, ffn2_w_gate, ffn2_w_up, ffn2_w_down)
```

```python
import functools
import math

import jax
import jax.numpy as jnp
from jax import lax
from jax.experimental import pallas as pl
from jax.experimental.pallas import tpu as pltpu

F32 = jnp.float32
BF16 = jnp.bfloat16

HEAD_DIM = 128
SSM_CH = 16
SSM_STATE = 64
S5_CHUNK = 16
LANES = 128
GROUPS_PER_TILE = LANES // SSM_CH
MACARON_W = 0.5
NORM_EPS = 1e-6
ATTN_SCALE = HEAD_DIM ** -0.5
LOG2E = math.log2(math.e)
NEG_INF = -1e30
VMEM_LIMIT = 56 * 1024 * 1024
N_BIAS_TERMS = 3


def _cparams(*sem):
    return pltpu.CompilerParams(dimension_semantics=sem, vmem_limit_bytes=VMEM_LIMIT)


def _rms(x, g):
    return x * lax.rsqrt(jnp.mean(x * x, axis=-1, keepdims=True) + NORM_EPS) * g


def _dot(a, b):
    return jnp.dot(a, b, preferred_element_type=F32)


def _dot_nt(a, b):
    return lax.dot_general(a, b, (((1,), (1,)), ((), ())), preferred_element_type=F32)


def _split3(x):
    hi = x.astype(BF16)
    r1 = x - hi.astype(F32)
    mid = r1.astype(BF16)
    lo = (r1 - mid.astype(F32)).astype(BF16)
    return hi, mid, lo


def _dot_exact_lhs01(sel, x):
    hi, mid, lo = _split3(x)
    return _dot(sel, hi) + _dot(sel, mid) + _dot(sel, lo)


def _iota2(shape, axis):
    return lax.broadcasted_iota(jnp.int32, shape, axis)


def _ffn_kernel(x_ref, gpre_ref, gpost_ref, wg_ref, wu_ref, wd_ref, o_ref, xn_ref, acc_ref):
    j = pl.program_id(1)

    @pl.when(j == 0)
    def _():
        xn_ref[...] = _rms(x_ref[...], gpre_ref[...]).astype(BF16)
        acc_ref[...] = jnp.zeros_like(acc_ref)

    xn = xn_ref[...]
    g = _dot(xn, wg_ref[...])
    u = _dot(xn, wu_ref[...])
    h = (g * jax.nn.sigmoid(g) * u).astype(BF16)
    acc_ref[...] += _dot(h, wd_ref[...])

    @pl.when(j == pl.num_programs(1) - 1)
    def _():
        o_ref[...] = x_ref[...] + MACARON_W * _rms(acc_ref[...], gpost_ref[...])


def _ffn(x, g_pre, g_post, wg, wu, wd, *, tm, tf):
    r, d = x.shape
    f = wg.shape[1]
    return pl.pallas_call(
        _ffn_kernel,
        grid=(r // tm, f // tf),
        in_specs=[
            pl.BlockSpec((tm, d), lambda i, j: (i, 0)),
            pl.BlockSpec((1, d), lambda i, j: (0, 0)),
            pl.BlockSpec((1, d), lambda i, j: (0, 0)),
            pl.BlockSpec((d, tf), lambda i, j: (0, j)),
            pl.BlockSpec((d, tf), lambda i, j: (0, j)),
            pl.BlockSpec((tf, d), lambda i, j: (j, 0)),
        ],
        out_specs=pl.BlockSpec((tm, d), lambda i, j: (i, 0)),
        out_shape=jax.ShapeDtypeStruct((r, d), F32),
        scratch_shapes=[pltpu.VMEM((tm, d), BF16), pltpu.VMEM((tm, d), F32)],
        compiler_params=_cparams("parallel", "arbitrary"),
        name="ffn",
    )(x, g_pre, g_post, wg, wu, wd)


def _in_kernel(x_ref, g_ref, w_ref, wf_ref, bf_ref, zf_ref, qa_ref, ka_ref, vt_ref, logf_ref,
               xn_ref, carry_ref, qaug_ref, kaug_ref, *, rep):
    i = pl.program_id(0)
    j = pl.program_id(1)
    tm = x_ref.shape[0]
    aw = w_ref.shape[2]
    nh = aw // HEAD_DIM
    nb = N_BIAS_TERMS

    @pl.when(j == 0)
    def _():
        xn = _rms(x_ref[...], g_ref[...]).astype(BF16)
        xn_ref[...] = xn
        logf = jax.nn.log_sigmoid(_dot(xn, wf_ref[...]) + bf_ref[...])
        logf_ref[...] = logf

        @pl.when(i == 0)
        def _():
            carry_ref[...] = jnp.zeros_like(carry_ref)

        tril = (_iota2((tm, tm), 1) <= _iota2((tm, tm), 0)).astype(BF16)
        fcum = _dot_exact_lhs01(tril, logf) + carry_ref[...]
        carry_ref[...] = fcum[tm - 1:tm, :]

        terms = _split3(fcum * LOG2E)
        src = _iota2((LANES, aw), 0)
        dst = _iota2((LANES, aw), 1)
        head_src = src == (dst // HEAD_DIM) * rep
        lane = dst % HEAD_DIM
        kaug = jnp.zeros((tm, aw), F32)
        qaug = jnp.zeros((tm, aw), F32)
        for n, term in enumerate(terms):
            kaug = kaug - _dot(term, jnp.logical_and(head_src, lane == n).astype(BF16))
            qaug = qaug + _dot(term, jnp.logical_and(head_src, lane == nb + n).astype(BF16))
        lane_t = _iota2((tm, aw), 1) % HEAD_DIM
        kaug_ref[...] = (kaug + jnp.logical_and(lane_t >= nb, lane_t < 2 * nb).astype(F32)).astype(BF16)
        qaug_ref[...] = (qaug + (lane_t < nb).astype(F32)).astype(BF16)

    z = _dot(xn_ref[...], w_ref[0])
    zf_ref[0] = z

    def interleave(dst_ref, zb, aug_ref):
        for h in range(nh):
            dst_ref[:, 2 * h * HEAD_DIM:(2 * h + 1) * HEAD_DIM] = zb[:, h * HEAD_DIM:(h + 1) * HEAD_DIM]
            dst_ref[:, (2 * h + 1) * HEAD_DIM:(2 * h + 2) * HEAD_DIM] = aug_ref[:, h * HEAD_DIM:(h + 1) * HEAD_DIM]

    @pl.when(j == 0)
    def _():
        interleave(qa_ref, (z * (ATTN_SCALE * LOG2E)).astype(BF16), qaug_ref)

    @pl.when(j == 1)
    def _():
        interleave(ka_ref, z.astype(BF16), kaug_ref)

    @pl.when(j == 2)
    def _():
        for h in range(nh):
            vt_ref[h] = jnp.transpose(z[:, h * HEAD_DIM:(h + 1) * HEAD_DIM]).astype(BF16)


def _in_proj(x, g, w4, wf, bf, *, tm, rep):
    r, d = x.shape
    nsec, _, aw = w4.shape
    return pl.pallas_call(
        functools.partial(_in_kernel, rep=rep),
        grid=(r // tm, nsec),
        in_specs=[
            pl.BlockSpec((tm, d), lambda i, j: (i, 0)),
            pl.BlockSpec((1, d), lambda i, j: (0, 0)),
            pl.BlockSpec((1, d, aw), lambda i, j: (j, 0, 0)),
            pl.BlockSpec((d, LANES), lambda i, j: (0, 0)),
            pl.BlockSpec((1, LANES), lambda i, j: (0, 0)),
        ],
        out_specs=[
            pl.BlockSpec((1, tm, aw), lambda i, j: (j, i, 0)),
            pl.BlockSpec((tm, 2 * aw), lambda i, j: (i, 0)),
            pl.BlockSpec((tm, 2 * aw), lambda i, j: (i, 0)),
            pl.BlockSpec((aw // HEAD_DIM, None, HEAD_DIM, tm), lambda i, j: (0, i, 0, 0)),
            pl.BlockSpec((tm, LANES), lambda i, j: (i, 0)),
        ],
        out_shape=[
            jax.ShapeDtypeStruct((nsec, r, aw), F32),
            jax.ShapeDtypeStruct((r, 2 * aw), BF16),
            jax.ShapeDtypeStruct((r, 2 * aw), BF16),
            jax.ShapeDtypeStruct((aw // HEAD_DIM, r // tm, HEAD_DIM, tm), BF16),
            jax.ShapeDtypeStruct((r, LANES), F32),
        ],
        scratch_shapes=[pltpu.VMEM((tm, d), BF16), pltpu.VMEM((1, LANES), F32),
                        pltpu.VMEM((tm, aw), BF16), pltpu.VMEM((tm, aw), BF16)],
        compiler_params=_cparams("arbitrary", "arbitrary"),
        name="in_proj",
    )(x, g, w4, wf, bf)


def _attn_p_kernel(qa_ref, ka_ref, vt_ref, o_ref, sa_ref, sb_ref, acc_ref):
    qi = pl.program_id(1)
    tq = qa_ref.shape[0]

    def produce(s_ref, kb):
        s = _dot_nt(ka_ref[pl.ds(pl.multiple_of(kb * tq, tq), tq), :], qa_ref[...])
        s_ref[...] = s
        return jnp.max(s, axis=0, keepdims=True)

    def consume(s_ref, m_blk, kb, carry):
        m, l = carry
        m_new = jnp.maximum(m, m_blk)
        p = jnp.exp2(s_ref[...] - m_new)
        alpha = jnp.exp2(m - m_new)
        acc_ref[...] = alpha * acc_ref[...] + _dot(vt_ref[kb], p.astype(BF16))
        return m_new, alpha * l + jnp.sum(p, axis=0, keepdims=True)

    acc_ref[...] = jnp.zeros_like(acc_ref)
    m_first = produce(sa_ref, 0)

    def pair(i, carry):
        m_a, m, l = carry
        kb = 2 * i
        m_b = produce(sb_ref, kb + 1)
        m, l = consume(sa_ref, m_a, kb, (m, l))
        m_a = produce(sa_ref, kb + 2)
        m, l = consume(sb_ref, m_b, kb + 1, (m, l))
        return m_a, m, l

    init = (m_first, jnp.full((1, tq), NEG_INF, F32), jnp.zeros((1, tq), F32))
    m_a, m, l = lax.fori_loop(0, qi // 2, pair, init)
    causal = _iota2((tq, tq), 0) <= _iota2((tq, tq), 1)

    def finish(s_ref, carry):
        s = jnp.where(causal, s_ref[...], NEG_INF)
        s_ref[...] = s
        _, l_tot = consume(s_ref, jnp.max(s, axis=0, keepdims=True), qi, carry)
        o_ref[...] = jnp.transpose(acc_ref[...] / l_tot)

    @pl.when(qi % 2 == 0)
    def _():
        finish(sa_ref, (m, l))

    @pl.when(qi % 2 == 1)
    def _():
        produce(sb_ref, qi)
        finish(sb_ref, consume(sa_ref, m_a, qi - 1, (m, l)))


def _attn_prompt(qa, ka, vt):
    nh, nq, _, tq = vt.shape
    r = nq * tq
    return pl.pallas_call(
        _attn_p_kernel,
        grid=(nh, nq),
        in_specs=[
            pl.BlockSpec((tq, 2 * HEAD_DIM), lambda h, i: (i, h)),
            pl.BlockSpec((r, 2 * HEAD_DIM), lambda h, i: (0, h)),
            pl.BlockSpec((None, nq, HEAD_DIM, tq), lambda h, i: (h, 0, 0, 0)),
        ],
        out_specs=pl.BlockSpec((tq, HEAD_DIM), lambda h, i: (i, h)),
        out_shape=jax.ShapeDtypeStruct((r, nh * HEAD_DIM), F32),
        scratch_shapes=[pltpu.VMEM((tq, tq), F32), pltpu.VMEM((tq, tq), F32), pltpu.VMEM((HEAD_DIM, tq), F32)],
        compiler_params=_cparams("parallel", "arbitrary"),
        name="attn_p",
    )(qa, ka, vt)


def _attn_s_kernel(qbd_ref, ck_ref, cv_ref, cf_ref, kn_ref, vn_ref, fn_ref, attn_in_ref, o_ref,
                   m_ref, l_ref, acc_ref, suf_ref, cq_ref, *, rep, seg, nh):
    del attn_in_ref
    step = pl.program_id(1)
    s_new = kn_ref.shape[0]
    tk = cf_ref.shape[0]
    qbd = qbd_ref[...]

    def update(t, v_bf):
        m_old = m_ref[...]
        m_new = jnp.maximum(m_old, jnp.max(t, axis=0, keepdims=True))
        p = jnp.exp(t - m_new)
        alpha = jnp.exp(m_old - m_new)
        l_ref[...] = alpha * l_ref[...] + jnp.sum(p, axis=0, keepdims=True)
        m_ref[...] = m_new
        alpha_col = jnp.transpose(jnp.broadcast_to(alpha, (LANES, LANES)))
        pv = _dot(jnp.transpose(p).astype(BF16), v_bf)
        for c in range(nh):
            sl = slice(c * HEAD_DIM, (c + 1) * HEAD_DIM)
            acc_ref[:, sl] = acc_ref[:, sl] * alpha_col + pv[:, sl]

    @pl.when(step == 0)
    def _():
        m_ref[...] = jnp.full_like(m_ref, NEG_INF)
        l_ref[...] = jnp.zeros_like(l_ref)
        acc_ref[...] = jnp.zeros_like(acc_ref)
        tril = (_iota2((s_new, s_new), 1) <= _iota2((s_new, s_new), 0)).astype(BF16)
        cn = _dot_exact_lhs01(tril, fn_ref[...])
        krow = _iota2((s_new, LANES), 0)
        qlane = _iota2((s_new, LANES), 1) % rep
        cq = jnp.sum(jnp.where(krow == qlane, cn, 0.0), axis=0, keepdims=True)
        cq_ref[...] = cq
        suf_ref[...] = jnp.zeros_like(suf_ref)
        st = _dot(kn_ref[...].astype(BF16), qbd)
        t = jnp.where(krow <= qlane, st + (cq - cn), NEG_INF)
        update(t, vn_ref[...].astype(BF16))

    triu = (_iota2((seg, seg), 1) > _iota2((seg, seg), 0)).astype(BF16)
    carry = suf_ref[...]
    sufs = [None] * (tk // seg)
    for sidx in reversed(range(tk // seg)):
        x = cf_ref[sidx * seg:(sidx + 1) * seg, :]
        sfx = _dot_exact_lhs01(triu, x) + carry
        sufs[sidx] = sfx
        carry = sfx[0:1, :] + x[0:1, :]
    suf_ref[...] = carry
    bias = jnp.concatenate(sufs, axis=0) + cq_ref[...]
    def heads_side_by_side(ref):
        return jnp.concatenate([ref[pl.ds(h, tk, stride=nh), :].astype(BF16) for h in range(nh)], axis=1)

    update(_dot(heads_side_by_side(ck_ref), qbd) + bias, heads_side_by_side(cv_ref))

    @pl.when(step == pl.num_programs(1) - 1)
    def _():
        l_col = jnp.transpose(jnp.broadcast_to(l_ref[...], (LANES, LANES)))
        for c in range(nh):
            rws = slice(c * rep, c * rep + s_new)
            sl = slice(c * HEAD_DIM, (c + 1) * HEAD_DIM)
            o_ref[:, sl] = acc_ref[rws, sl] / l_col[rws, :]


def _attn_sample(layer, qbd, cache_k, cache_v, cf, zf, logf, attn, *, np_rows, tk, rep):
    depth, nb, past, nh, _ = cache_k.shape
    aw = nh * HEAD_DIM
    s_new = S5_CHUNK
    nkb = past // tk
    base = np_rows // s_new
    seg = min(tk, LANES)

    ck = cache_k.reshape(depth, nb, past * nh, HEAD_DIM)
    cv = cache_v.reshape(depth, nb, past * nh, HEAD_DIM)
    cache_spec = pl.BlockSpec((None, None, tk * nh, HEAD_DIM), lambda b, s: (layer, b, nkb - 1 - s, 0))
    return pl.pallas_call(
        functools.partial(_attn_s_kernel, rep=rep, seg=seg, nh=nh),
        grid=(nb, nkb),
        in_specs=[
            pl.BlockSpec((None, aw, LANES), lambda b, s: (b, 0, 0)),
            cache_spec,
            cache_spec,
            pl.BlockSpec((None, None, tk, LANES), lambda b, s: (layer, b, nkb - 1 - s, 0)),
            pl.BlockSpec((None, s_new, aw), lambda b, s: (1, base + b, 0)),
            pl.BlockSpec((None, s_new, aw), lambda b, s: (2, base + b, 0)),
            pl.BlockSpec((s_new, LANES), lambda b, s: (base + b, 0)),
            pl.BlockSpec(memory_space=pl.ANY),
        ],
        out_specs=pl.BlockSpec((s_new, aw), lambda b, s: (base + b, 0)),
        out_shape=jax.ShapeDtypeStruct(attn.shape, F32),
        scratch_shapes=[
            pltpu.VMEM((1, LANES), F32), pltpu.VMEM((1, LANES), F32), pltpu.VMEM((LANES, aw), F32),
            pltpu.VMEM((1, LANES), F32), pltpu.VMEM((1, LANES), F32),
        ],
        input_output_aliases={7: 0},
        compiler_params=_cparams("parallel", "arbitrary"),
        name="attn_s",
    )(qbd, ck, cv, cf, zf, zf, logf, attn)


def _s5_kernel(u_ref, kds_ref, wc_ref, vc_ref, a_ref, hinit_ref, d_ref, wglu_ref, o_ref, hend_ref,
               wfull_ref, vfull_ref, s_ref, hprev_ref, hc_ref, *, n_prompt_chunks):
    rt = pl.program_id(1)
    ct, sd = s_ref.shape
    half = sd // 2
    L = S5_CHUNK
    P = SSM_STATE

    @pl.when(rt == 0)
    def _():
        hc_ref[...] = jnp.zeros_like(hc_ref)
        r_, c_ = _iota2((2 * P, sd), 0), _iota2((2 * P, sd), 1)
        rep_w = jnp.logical_and(r_ // P == c_ // half, r_ % P == c_ % P).astype(BF16)
        r_, c_ = _iota2((L * LANES, sd), 0), _iota2((L * LANES, sd), 1)
        same = (r_ // SSM_CH) % GROUPS_PER_TILE == (c_ % half) // P
        wfull_ref[...] = jnp.where(same, _dot(wc_ref[...], rep_w), 0.0).astype(BF16)
        r_, c_ = _iota2((L * SSM_CH, L * LANES), 0), _iota2((L * SSM_CH, L * LANES), 1)
        rep_v = jnp.logical_and(r_ // SSM_CH == c_ // LANES, r_ % SSM_CH == c_ % SSM_CH).astype(BF16)
        r_, c_ = _iota2((sd, L * LANES), 0), _iota2((sd, L * LANES), 1)
        same = (r_ % half) // P == (c_ % LANES) // SSM_CH
        vfull_ref[...] = jnp.where(same, _dot(vc_ref[...], rep_v), 0.0).astype(BF16)

    def u_at(tau):
        return u_ref[pl.ds(tau, ct, stride=L), :]

    xr = jnp.concatenate([u_at(L - 1 - j).astype(BF16) for j in range(L)], axis=1)
    s_ref[...] = _dot(xr, wfull_ref[...])

    ar = a_ref[:, :half]
    ai = a_ref[:, half:]

    def scan(c, hcar):
        cg = rt * ct + c
        reset = jnp.logical_or(cg == 0, cg >= n_prompt_chunks)
        hp = jnp.where(reset, hinit_ref[pl.ds(c, 1), :], hcar)
        hprev_ref[pl.ds(c, 1), :] = hp
        s = s_ref[pl.ds(c, 1), :]
        hr = hp[:, :half]
        hi = hp[:, half:]
        hn = jnp.concatenate([ar * hr - ai * hi + s[:, :half], ar * hi + ai * hr + s[:, half:]], axis=1)
        hend_ref[pl.ds(c, 1), :] = hn
        return hn

    hc_ref[...] = lax.fori_loop(0, ct, scan, hc_ref[...])

    ystate = _dot(hprev_ref[...].astype(BF16), vfull_ref[...])
    d = d_ref[...]
    wglu = wglu_ref[...]
    for tau in range(L):
        lag_rows = (tau + 1) * LANES
        yt = _dot(xr[:, (L - 1 - tau) * LANES:], kds_ref[:lag_rows, :])
        yt = yt + ystate[:, tau * LANES:(tau + 1) * LANES] + d * u_at(tau)
        yt = 0.5 * yt * (1.0 + jnp.tanh(math.sqrt(2.0 / math.pi) * (yt + 0.044715 * (yt * yt * yt))))
        gate = jax.nn.sigmoid(_dot(yt.astype(BF16), wglu))
        o_ref[pl.ds(tau, ct, stride=L), :] = yt * gate


def _s5(u, kds, wc, vc, a16, hinit, d, wglu, *, rows_tile, n_prompt_chunks):
    r, sw = u.shape
    no = sw // LANES
    L = S5_CHUNK
    ct = rows_tile // L
    nrt = r // rows_tile
    sd = a16.shape[2]
    return pl.pallas_call(
        functools.partial(_s5_kernel, n_prompt_chunks=n_prompt_chunks),
        grid=(no, nrt),
        in_specs=[
            pl.BlockSpec((rows_tile, LANES), lambda o, t: (t, o)),
            pl.BlockSpec((None, L * LANES, LANES), lambda o, t: (o, 0, 0)),
            pl.BlockSpec((None, L * LANES, 2 * SSM_STATE), lambda o, t: (o, 0, 0)),
            pl.BlockSpec((None, sd, L * SSM_CH), lambda o, t: (o, 0, 0)),
            pl.BlockSpec((None, 1, sd), lambda o, t: (o, 0, 0)),
            pl.BlockSpec((ct, sd), lambda o, t: (t, o)),
            pl.BlockSpec((1, LANES), lambda o, t: (0, o)),
            pl.BlockSpec((None, LANES, LANES), lambda o, t: (o, 0, 0)),
        ],
        out_specs=[
            pl.BlockSpec((rows_tile, LANES), lambda o, t: (t, o)),
            pl.BlockSpec((ct, sd), lambda o, t: (t, o)),
        ],
        out_shape=[
            jax.ShapeDtypeStruct((r, sw), F32),
            jax.ShapeDtypeStruct((r // L, no * sd), F32),
        ],
        scratch_shapes=[pltpu.VMEM((L * LANES, sd), BF16), pltpu.VMEM((sd, L * LANES), BF16),
                        pltpu.VMEM((ct, sd), F32), pltpu.VMEM((ct, sd), F32), pltpu.VMEM((1, sd), F32)],
        compiler_params=_cparams("parallel", "arbitrary"),
        name="s5",
    )(u, kds, wc, vc, a16, hinit, d, wglu)


def _s5_operators(a_re, a_im, log_dt, b_re, b_im, c_re, c_im, w_glu):
    g, p = a_re.shape
    c = SSM_CH
    L = S5_CHUNK
    gt = GROUPS_PER_TILE
    no = g // gt
    hp = lax.Precision.HIGHEST
    cmul = lambda xr, xi, yr, yi: (xr * yr - xi * yi, xr * yi + xi * yr)
    dt = jnp.exp(log_dt)[:, None]
    mag = jnp.exp(a_re * dt)
    ar, ai = mag * jnp.cos(a_im * dt), mag * jnp.sin(a_im * dt)
    den = a_re * a_re + a_im * a_im
    fr = ((ar - 1.0) * a_re + ai * a_im) / den
    fi = (ai * a_re - (ar - 1.0) * a_im) / den
    bbr, bbi = cmul(fr[..., None], fi[..., None], b_re, b_im)
    pr, pi = [jnp.ones_like(ar)], [jnp.zeros_like(ar)]
    for _ in range(L):
        nr, ni = cmul(pr[-1], pi[-1], ar, ai)
        pr.append(nr)
        pi.append(ni)
    pr, pi = jnp.stack(pr), jnp.stack(pi)

    abr, abi = cmul(pr[:L, :, :, None], pi[:L, :, :, None], bbr[None], bbi[None])
    kd = (jnp.einsum('gop,dgpi->gdio', c_re, abr, precision=hp)
          - jnp.einsum('gop,dgpi->gdio', c_im, abi, precision=hp))
    eye = jnp.eye(gt, dtype=F32)
    kd = kd.reshape(no, gt, L, c, c).transpose(0, 2, 1, 3, 4)
    kds = (kd[:, :, :, :, None, :] * eye[None, None, :, None, :, None]).reshape(no, L * LANES, LANES)

    wc = jnp.stack([abr, abi], axis=0).reshape(2, L, no, gt, p, c)
    wc = wc.transpose(2, 1, 3, 5, 0, 4).reshape(no, L * LANES, 2 * p)

    zr, zi = cmul(c_re.transpose(0, 2, 1)[None], c_im.transpose(0, 2, 1)[None],
                  pr[1:, :, :, None], pi[1:, :, :, None])
    vc = jnp.stack([zr, -zi], axis=0).reshape(2, L, no, gt, p, c)
    vc = vc.transpose(2, 0, 3, 4, 1, 5).reshape(no, 2 * gt * p, L * c)

    a16 = jnp.concatenate([pr[L].reshape(no, 1, gt * p), pi[L].reshape(no, 1, gt * p)], axis=2)
    wg = w_glu.reshape(no, gt, c, c)
    wglu = (wg[:, :, :, None, :] * eye[None, :, None, :, None]).reshape(no, LANES, LANES)
    return kds.astype(BF16), wc.astype(BF16), vc.astype(BF16), a16, wglu.astype(BF16)


def _out_kernel(x_ref, attn_ref, ssm_ref, ga_ref, gs_ref, gpost_ref, wa_ref, ws_ref, o_ref):
    an = _rms(attn_ref[...], ga_ref[...]).astype(BF16)
    sn = _rms(ssm_ref[...], gs_ref[...]).astype(BF16)
    mixed = _dot(an, wa_ref[...]) + _dot(sn, ws_ref[...])
    o_ref[...] = x_ref[...] + _rms(mixed, gpost_ref[...])


def _out_proj(x, attn, ssm, ga, gs, gpost, wa, ws, *, tm):
    r, d = x.shape
    aw = attn.shape[1]
    sw = ssm.shape[1]
    once = dict(pipeline_mode=pl.Buffered(1))
    return pl.pallas_call(
        _out_kernel,
        grid=(r // tm,),
        in_specs=[
            pl.BlockSpec((tm, d), lambda i: (i, 0)),
            pl.BlockSpec((tm, aw), lambda i: (i, 0)),
            pl.BlockSpec((tm, sw), lambda i: (i, 0)),
            pl.BlockSpec((1, aw), lambda i: (0, 0)),
            pl.BlockSpec((1, sw), lambda i: (0, 0)),
            pl.BlockSpec((1, d), lambda i: (0, 0)),
            pl.BlockSpec((aw, d), lambda i: (0, 0), **once),
            pl.BlockSpec((sw, d), lambda i: (0, 0), **once),
        ],
        out_specs=pl.BlockSpec((tm, d), lambda i: (i, 0)),
        out_shape=jax.ShapeDtypeStruct((r, d), F32),
        compiler_params=_cparams("parallel"),
        name="out_proj",
    )(x, attn, ssm, ga, gs, gpost, wa, ws)


def _tiles(r_min):
    tm = 512
    r = -(-r_min // tm) * tm
    s5_tiles = 4 if (r // 4) % (8 * S5_CHUNK) == 0 else 1
    return dict(r=r, tm=tm, tq=tm, tf=512, s5_rows=r // s5_tiles, tk_cache=1024)


def _forward(x_prompt, x_sample, cache_k, cache_v, cache_logf, state_ssm_re, state_ssm_im, meta_tokens,
             ffn1_norm_pre, ffn1_norm_post, ffn1_w_gate, ffn1_w_up, ffn1_w_down,
             mix_norm_pre, mix_norm_post, w_in, b_forget,
             ssm_a_re, ssm_a_im, ssm_log_dt, ssm_b_re, ssm_b_im, ssm_c_re, ssm_c_im, ssm_d, ssm_w_glu,
             attn_out_norm, ssm_out_norm, w_out,
             ffn2_norm_pre, ffn2_norm_post, ffn2_w_gate, ffn2_w_up, ffn2_w_down, tiles=None):
    bsz, seq, d = x_prompt.shape
    nb, s_new, _ = x_sample.shape
    depth, _, past, nh, _ = cache_k.shape
    n_meta = meta_tokens.shape[0]
    aw = nh * HEAD_DIM
    g, p = ssm_a_re.shape[1:]
    L = S5_CHUNK
    assert bsz == 1 and s_new == L and n_meta % L == 0 and seq % L == 0 and p == SSM_STATE
    assert LANES % nh == 0 and s_new <= LANES // nh and g % GROUPS_PER_TILE == 0
    rep = LANES // nh
    no = g // GROUPS_PER_TILE
    sd = 2 * GROUPS_PER_TILE * p
    np_rows = n_meta + seq
    ns_rows = nb * s_new
    t = tiles or _tiles(np_rows + ns_rows)
    r = t["r"]
    npc = np_rows // L
    n_chunks = r // L

    x = jnp.concatenate([meta_tokens.astype(F32), x_prompt[0], x_sample.reshape(ns_rows, d),
                         jnp.zeros((r - np_rows - ns_rows, d), F32)], axis=0)
    cf = jnp.repeat(cache_logf, rep, axis=-1)

    row2 = lambda a: a.reshape(1, -1)
    outs = dict(k=[], v=[], logf=[], hend=[])
    for l in range(depth):
        x = _ffn(x, row2(ffn1_norm_pre[l]), row2(ffn1_norm_post[l]), ffn1_w_gate[l].astype(BF16),
                 ffn1_w_up[l].astype(BF16), ffn1_w_down[l].astype(BF16), tm=t["tm"], tf=t["tf"])

        wl = w_in[l]
        w4 = jnp.stack([wl[:, :aw], wl[:, aw:2 * aw], wl[:, 2 * aw:3 * aw], wl[:, 3 * aw + nh:]]).astype(BF16)
        wf = jnp.repeat(wl[:, 3 * aw:3 * aw + nh], rep, axis=1).astype(BF16)
        bf = jnp.repeat(b_forget[l], rep).reshape(1, LANES)
        zf, qa, ka, vt, logf = _in_proj(x, row2(mix_norm_pre[l]), w4, wf, bf, tm=t["tm"], rep=rep)

        attn = _attn_prompt(qa, ka, vt)

        qs = (zf[0, np_rows:np_rows + ns_rows] * ATTN_SCALE).astype(BF16).reshape(nb, s_new, nh, HEAD_DIM)
        qs = jnp.pad(qs.transpose(0, 2, 3, 1), ((0, 0), (0, 0), (0, 0), (0, rep - s_new)))
        qbd = (qs[:, :, :, None, :] * jnp.eye(nh, dtype=BF16)[None, :, None, :, None]).reshape(nb, aw, LANES)
        attn = _attn_sample(l, qbd, cache_k, cache_v, cf, zf, logf, attn,
                            np_rows=np_rows, tk=t["tk_cache"], rep=rep)

        kds, wc, vc, a16, wglu = _s5_operators(ssm_a_re[l], ssm_a_im[l], ssm_log_dt[l], ssm_b_re[l],
                                               ssm_b_im[l], ssm_c_re[l], ssm_c_im[l], ssm_w_glu[l])
        h0 = jnp.concatenate([state_ssm_re[l].reshape(nb, no, 1, sd // 2),
                              state_ssm_im[l].reshape(nb, no, 1, sd // 2)], axis=2).reshape(nb, no * sd)
        hinit = jnp.zeros((n_chunks, no * sd), F32).at[npc:npc + nb].set(h0)
        ssm, hend = _s5(zf[3], kds, wc, vc, a16, hinit, row2(ssm_d[l]), wglu,
                        rows_tile=t["s5_rows"], n_prompt_chunks=npc)

        wo = w_out[l].astype(BF16)
        x = _out_proj(x, attn, ssm, row2(attn_out_norm[l]), row2(ssm_out_norm[l]), row2(mix_norm_post[l]),
                      wo[:aw], wo[aw:], tm=t["tm"])

        x = _ffn(x, row2(ffn2_norm_pre[l]), row2(ffn2_norm_post[l]), ffn2_w_gate[l].astype(BF16),
                 ffn2_w_up[l].astype(BF16), ffn2_w_down[l].astype(BF16), tm=t["tm"], tf=t["tf"])

        outs["k"].append(zf[1])
        outs["v"].append(zf[2])
        outs["logf"].append(logf[:, ::rep])
        outs["hend"].append(hend.reshape(n_chunks, no, 2, GROUPS_PER_TILE, p))

    k_all = jnp.stack(outs["k"])
    v_all = jnp.stack(outs["v"])
    f_all = jnp.stack(outs["logf"])
    h_all = jnp.stack(outs["hend"])
    sl_p = slice(0, np_rows)
    sl_s = slice(np_rows, np_rows + ns_rows)
    heads = lambda a, n: a.reshape(depth, -1, n, nh, HEAD_DIM)
    states = lambda a: a.reshape(depth, -1, g, p)
    return (x[n_meta:np_rows][None], x[sl_s].reshape(nb, s_new, d),
            heads(k_all[:, sl_p], np_rows), heads(v_all[:, sl_p], np_rows),
            f_all[:, sl_p].reshape(depth, 1, np_rows, nh),
            states(h_all[:, npc - 1:npc, :, 0]), states(h_all[:, npc - 1:npc, :, 1]),
            heads(k_all[:, sl_s], s_new), heads(v_all[:, sl_s], s_new),
            f_all[:, sl_s].reshape(depth, nb, s_new, nh),
            states(h_all[:, npc:npc + nb, :, 0]), states(h_all[:, npc:npc + nb, :, 1]))


def kernel(x_prompt, x_sample, cache_k, cache_v, cache_logf, state_ssm_re, state_ssm_im, meta_tokens, ffn1_norm_pre, ffn1_norm_post, ffn1_w_gate, ffn1_w_up, ffn1_w_down, mix_norm_pre, mix_norm_post, w_in, b_forget, ssm_a_re, ssm_a_im, ssm_log_dt, ssm_b_re, ssm_b_im, ssm_c_re, ssm_c_im, ssm_d, ssm_w_glu, attn_out_norm, ssm_out_norm, w_out, ffn2_norm_pre, ffn2_norm_post, ffn2_w_gate, ffn2_w_up, ffn2_w_down):
    return _forward(x_prompt, x_sample, cache_k, cache_v, cache_logf, state_ssm_re, state_ssm_im, meta_tokens,
                    ffn1_norm_pre, ffn1_norm_post, ffn1_w_gate, ffn1_w_up, ffn1_w_down,
                    mix_norm_pre, mix_norm_post, w_in, b_forget,
                    ssm_a_re, ssm_a_im, ssm_log_dt, ssm_b_re, ssm_b_im, ssm_c_re, ssm_c_im, ssm_d, ssm_w_glu,
                    attn_out_norm, ssm_out_norm, w_out,
                    ffn2_norm_pre, ffn2_norm_post, ffn2_w_gate, ffn2_w_up, ffn2_w_down)
```

```python
import functools
import math

import jax
import jax.numpy as jnp
from jax import lax
from jax.experimental import pallas as pl
from jax.experimental.pallas import tpu as pltpu

F32 = jnp.float32
BF16 = jnp.bfloat16

HEAD_DIM = 128
SSM_CH = 16
SSM_STATE = 64
S5_CHUNK = 16
LANES = 128
GROUPS_PER_TILE = LANES // SSM_CH
MACARON_W = 0.5
NORM_EPS = 1e-6
ATTN_SCALE = HEAD_DIM ** -0.5
LOG2E = math.log2(math.e)
NEG_INF = -1e30
VMEM_LIMIT = 56 * 1024 * 1024
N_BIAS_TERMS = 3
SCAN_UNROLL = 4


def _cparams(*sem):
    return pltpu.CompilerParams(dimension_semantics=sem, vmem_limit_bytes=VMEM_LIMIT)


def _rms(x, g):
    return x * lax.rsqrt(jnp.mean(x * x, axis=-1, keepdims=True) + NORM_EPS) * g


def _dot(a, b):
    return jnp.dot(a, b, preferred_element_type=F32)


def _dot_nt(a, b):
    return lax.dot_general(a, b, (((1,), (1,)), ((), ())), preferred_element_type=F32)


def _split3(x):
    hi = x.astype(BF16)
    r1 = x - hi.astype(F32)
    mid = r1.astype(BF16)
    lo = (r1 - mid.astype(F32)).astype(BF16)
    return hi, mid, lo


def _dot_exact_lhs01(sel, x):
    hi, mid, lo = _split3(x)
    return _dot(sel, hi) + _dot(sel, mid) + _dot(sel, lo)


def _iota2(shape, axis):
    return lax.broadcasted_iota(jnp.int32, shape, axis)


def _ffn_kernel(x_ref, gpre_ref, gpost_ref, wg_ref, wu_ref, wd_ref, o_ref, xn_ref, acc_ref):
    j = pl.program_id(1)

    @pl.when(j == 0)
    def _():
        xn_ref[...] = _rms(x_ref[...], gpre_ref[...]).astype(BF16)
        acc_ref[...] = jnp.zeros_like(acc_ref)

    xn = xn_ref[...]
    g = _dot(xn, wg_ref[...])
    u = _dot(xn, wu_ref[...])
    h = (g * jax.nn.sigmoid(g) * u).astype(BF16)
    acc_ref[...] += _dot(h, wd_ref[...])

    @pl.when(j == pl.num_programs(1) - 1)
    def _():
        o_ref[...] = x_ref[...] + MACARON_W * _rms(acc_ref[...], gpost_ref[...])


def _ffn(x, g_pre, g_post, wg, wu, wd, layer, *, tm, tf):
    r, d = x.shape
    f = wg.shape[2]
    return pl.pallas_call(
        _ffn_kernel,
        grid=(r // tm, f // tf),
        in_specs=[
            pl.BlockSpec((tm, d), lambda i, j: (i, 0)),
            pl.BlockSpec((1, d), lambda i, j: (0, 0)),
            pl.BlockSpec((1, d), lambda i, j: (0, 0)),
            pl.BlockSpec((None, d, tf), lambda i, j: (layer, 0, j)),
            pl.BlockSpec((None, d, tf), lambda i, j: (layer, 0, j)),
            pl.BlockSpec((None, tf, d), lambda i, j: (layer, j, 0)),
        ],
        out_specs=pl.BlockSpec((tm, d), lambda i, j: (i, 0)),
        out_shape=jax.ShapeDtypeStruct((r, d), F32),
        scratch_shapes=[pltpu.VMEM((tm, d), BF16), pltpu.VMEM((tm, d), F32)],
        compiler_params=_cparams("parallel", "arbitrary"),
        name="ffn",
    )(x, g_pre, g_post, wg, wu, wd)


def _in_kernel(x_ref, g_ref, w_ref, wf_ref, bf_ref, qf_ref, kf_ref, vf_ref, uf_ref, qa_ref, ka_ref, vt_ref, logf_ref,
               xn_ref, carry_ref, qaug_ref, kaug_ref, *, rep):
    i = pl.program_id(0)
    j = pl.program_id(1)
    tm = x_ref.shape[0]
    aw = w_ref.shape[2]
    nh = aw // HEAD_DIM
    nb = N_BIAS_TERMS

    @pl.when(j == 0)
    def _():
        xn = _rms(x_ref[...], g_ref[...]).astype(BF16)
        xn_ref[...] = xn
        logf = jax.nn.log_sigmoid(_dot(xn, wf_ref[...]) + bf_ref[...])
        logf_ref[...] = logf

        @pl.when(i == 0)
        def _():
            carry_ref[...] = jnp.zeros_like(carry_ref)

        tril = (_iota2((tm, tm), 1) <= _iota2((tm, tm), 0)).astype(BF16)
        fcum = _dot_exact_lhs01(tril, logf) + carry_ref[...]
        carry_ref[...] = fcum[tm - 1:tm, :]

        terms = jnp.concatenate(_split3(fcum * LOG2E), axis=1)
        src = _iota2((nb * LANES, aw), 0)
        dst = _iota2((nb * LANES, aw), 1)
        head_src = src % LANES == (dst // HEAD_DIM) * rep
        lane = dst % HEAD_DIM
        kaug = -_dot(terms, jnp.logical_and(head_src, lane == src // LANES).astype(BF16))
        qaug = _dot(terms, jnp.logical_and(head_src, lane == nb + src // LANES).astype(BF16))
        lane_t = _iota2((tm, aw), 1) % HEAD_DIM
        kaug_ref[...] = (kaug + jnp.logical_and(lane_t >= nb, lane_t < 2 * nb).astype(F32)).astype(BF16)
        qaug_ref[...] = (qaug + (lane_t < nb).astype(F32)).astype(BF16)

    z = _dot(xn_ref[...], w_ref[0])
    for sec, dst_ref in enumerate((qf_ref, kf_ref, vf_ref, uf_ref)):
        @pl.when(j == sec)
        def _(dst_ref=dst_ref):
            dst_ref[...] = z

    def interleave(dst_ref, zb, aug_ref):
        for h in range(nh):
            dst_ref[:, 2 * h * HEAD_DIM:(2 * h + 1) * HEAD_DIM] = zb[:, h * HEAD_DIM:(h + 1) * HEAD_DIM]
            dst_ref[:, (2 * h + 1) * HEAD_DIM:(2 * h + 2) * HEAD_DIM] = aug_ref[:, h * HEAD_DIM:(h + 1) * HEAD_DIM]

    @pl.when(j == 0)
    def _():
        interleave(qa_ref, (z * (ATTN_SCALE * LOG2E)).astype(BF16), qaug_ref)

    @pl.when(j == 1)
    def _():
        interleave(ka_ref, z.astype(BF16), kaug_ref)

    @pl.when(j == 2)
    def _():
        for h in range(nh):
            vt_ref[h] = jnp.transpose(z[:, h * HEAD_DIM:(h + 1) * HEAD_DIM]).astype(BF16)


def _in_proj(x, g, w4, wf, bf, *, tm, rep):
    r, d = x.shape
    nsec, _, aw = w4.shape
    return pl.pallas_call(
        functools.partial(_in_kernel, rep=rep),
        grid=(r // tm, nsec),
        in_specs=[
            pl.BlockSpec((tm, d), lambda i, j: (i, 0)),
            pl.BlockSpec((1, d), lambda i, j: (0, 0)),
            pl.BlockSpec((1, d, aw), lambda i, j: (j, 0, 0)),
            pl.BlockSpec((d, LANES), lambda i, j: (0, 0)),
            pl.BlockSpec((1, LANES), lambda i, j: (0, 0)),
        ],
        out_specs=[pl.BlockSpec((tm, aw), lambda i, j: (i, 0))] * nsec + [
            pl.BlockSpec((tm, 2 * aw), lambda i, j: (i, 0)),
            pl.BlockSpec((tm, 2 * aw), lambda i, j: (i, 0)),
            pl.BlockSpec((aw // HEAD_DIM, None, HEAD_DIM, tm), lambda i, j: (0, i, 0, 0)),
            pl.BlockSpec((tm, LANES), lambda i, j: (i, 0)),
        ],
        out_shape=[jax.ShapeDtypeStruct((r, aw), F32)] * nsec + [
            jax.ShapeDtypeStruct((r, 2 * aw), BF16),
            jax.ShapeDtypeStruct((r, 2 * aw), BF16),
            jax.ShapeDtypeStruct((aw // HEAD_DIM, r // tm, HEAD_DIM, tm), BF16),
            jax.ShapeDtypeStruct((r, LANES), F32),
        ],
        scratch_shapes=[pltpu.VMEM((tm, d), BF16), pltpu.VMEM((1, LANES), F32),
                        pltpu.VMEM((tm, aw), BF16), pltpu.VMEM((tm, aw), BF16)],
        compiler_params=_cparams("arbitrary", "arbitrary"),
        name="in_proj",
    )(x, g, w4, wf, bf)


def _attn_p_kernel(qa_ref, ka_ref, vt_ref, o_ref, sa_ref, sb_ref, acc_ref):
    qi = pl.program_id(1)
    tq = qa_ref.shape[0]

    def produce(s_ref, kb):
        s = _dot_nt(ka_ref[pl.ds(pl.multiple_of(kb * tq, tq), tq), :], qa_ref[...])
        s_ref[...] = s
        return jnp.max(s, axis=0, keepdims=True)

    def consume(s_ref, m_blk, kb, carry):
        m, l = carry
        m_new = jnp.maximum(m, m_blk)
        p = jnp.exp2(s_ref[...] - m_new)
        alpha = jnp.exp2(m - m_new)
        acc_ref[...] = alpha * acc_ref[...] + _dot(vt_ref[kb], p.astype(BF16))
        return m_new, alpha * l + jnp.sum(p, axis=0, keepdims=True)

    acc_ref[...] = jnp.zeros_like(acc_ref)
    m_first = produce(sa_ref, 0)

    def pair(kb, carry):
        m_a, m, l = carry
        m_b = produce(sb_ref, kb + 1)
        m, l = consume(sa_ref, m_a, kb, (m, l))
        m_a = produce(sa_ref, kb + 2)
        m, l = consume(sb_ref, m_b, kb + 1, (m, l))
        return m_a, m, l

    init = (m_first, jnp.full((1, tq), NEG_INF, F32), jnp.zeros((1, tq), F32))
    carry = lax.fori_loop(0, qi // 4, lambda i, c: pair(4 * i + 2, pair(4 * i, c)), init)
    done = (qi // 4) * 4
    m_a, m, l = lax.fori_loop(0, (qi % 4) // 2, lambda i, c: pair(done + 2 * i, c), carry)
    causal = _iota2((tq, tq), 0) <= _iota2((tq, tq), 1)

    def finish(s_ref, carry):
        s = jnp.where(causal, s_ref[...], NEG_INF)
        s_ref[...] = s
        _, l_tot = consume(s_ref, jnp.max(s, axis=0, keepdims=True), qi, carry)
        o_ref[...] = jnp.transpose(acc_ref[...] / l_tot)

    @pl.when(qi % 2 == 0)
    def _():
        finish(sa_ref, (m, l))

    @pl.when(qi % 2 == 1)
    def _():
        produce(sb_ref, qi)
        finish(sb_ref, consume(sa_ref, m_a, qi - 1, (m, l)))


def _attn_prompt(qa, ka, vt):
    nh, nq, _, tq = vt.shape
    r = nq * tq
    return pl.pallas_call(
        _attn_p_kernel,
        grid=(nh, nq),
        in_specs=[
            pl.BlockSpec((tq, 2 * HEAD_DIM), lambda h, i: (i, h)),
            pl.BlockSpec((r, 2 * HEAD_DIM), lambda h, i: (0, h)),
            pl.BlockSpec((None, nq, HEAD_DIM, tq), lambda h, i: (h, 0, 0, 0)),
        ],
        out_specs=pl.BlockSpec((tq, HEAD_DIM), lambda h, i: (i, h)),
        out_shape=jax.ShapeDtypeStruct((r, nh * HEAD_DIM), F32),
        scratch_shapes=[pltpu.VMEM((tq, tq), F32), pltpu.VMEM((tq, tq), F32), pltpu.VMEM((HEAD_DIM, tq), F32)],
        compiler_params=_cparams("parallel", "arbitrary"),
        name="attn_p",
    )(qa, ka, vt)


def _attn_s_kernel(qbd_ref, ck_ref, cv_ref, cf_ref, kn_ref, vn_ref, fn_ref, attn_in_ref, o_ref,
                   m_ref, l_ref, acc_ref, suf_ref, cq_ref, *, rep, seg, nh):
    del attn_in_ref
    step = pl.program_id(1)
    s_new = kn_ref.shape[0]
    tk = cf_ref.shape[0]
    qbd = qbd_ref[...]

    def update(t, v_bf):
        m_old = m_ref[...]
        m_new = jnp.maximum(m_old, jnp.max(t, axis=0, keepdims=True))
        p = jnp.exp(t - m_new)
        alpha = jnp.exp(m_old - m_new)
        l_ref[...] = alpha * l_ref[...] + jnp.sum(p, axis=0, keepdims=True)
        m_ref[...] = m_new
        alpha_col = jnp.transpose(jnp.broadcast_to(alpha, (LANES, LANES)))
        pv = _dot(jnp.transpose(p).astype(BF16), v_bf)
        for c in range(nh):
            sl = slice(c * HEAD_DIM, (c + 1) * HEAD_DIM)
            acc_ref[:, sl] = acc_ref[:, sl] * alpha_col + pv[:, sl]

    @pl.when(step == 0)
    def _():
        m_ref[...] = jnp.full_like(m_ref, NEG_INF)
        l_ref[...] = jnp.zeros_like(l_ref)
        acc_ref[...] = jnp.zeros_like(acc_ref)
        tril = (_iota2((s_new, s_new), 1) <= _iota2((s_new, s_new), 0)).astype(BF16)
        cn = _dot_exact_lhs01(tril, fn_ref[...])
        krow = _iota2((s_new, LANES), 0)
        qlane = _iota2((s_new, LANES), 1) % rep
        cq = jnp.sum(jnp.where(krow == qlane, cn, 0.0), axis=0, keepdims=True)
        cq_ref[...] = cq
        suf_ref[...] = jnp.zeros_like(suf_ref)
        st = _dot(kn_ref[...].astype(BF16), qbd)
        t = jnp.where(krow <= qlane, st + (cq - cn), NEG_INF)
        update(t, vn_ref[...].astype(BF16))

    triu = (_iota2((seg, seg), 1) > _iota2((seg, seg), 0)).astype(BF16)
    carry = suf_ref[...]
    sufs = [None] * (tk // seg)
    for sidx in reversed(range(tk // seg)):
        x = cf_ref[sidx * seg:(sidx + 1) * seg, :]
        sfx = _dot_exact_lhs01(triu, x) + carry
        sufs[sidx] = sfx
        carry = sfx[0:1, :] + x[0:1, :]
    suf_ref[...] = carry
    bias = jnp.concatenate(sufs, axis=0) + cq_ref[...]
    def heads_side_by_side(ref):
        return jnp.concatenate([ref[pl.ds(h, tk, stride=nh), :].astype(BF16) for h in range(nh)], axis=1)

    update(_dot(heads_side_by_side(ck_ref), qbd) + bias, heads_side_by_side(cv_ref))

    @pl.when(step == pl.num_programs(1) - 1)
    def _():
        l_col = jnp.transpose(jnp.broadcast_to(l_ref[...], (LANES, LANES)))
        for c in range(nh):
            rws = slice(c * rep, c * rep + s_new)
            sl = slice(c * HEAD_DIM, (c + 1) * HEAD_DIM)
            o_ref[:, sl] = acc_ref[rws, sl] / l_col[rws, :]


def _attn_sample(layer, qbd, cache_k, cache_v, cf, kf, vf, logf, attn, *, np_rows, tk, rep):
    depth, nb, past, nh, _ = cache_k.shape
    aw = nh * HEAD_DIM
    s_new = S5_CHUNK
    nkb = past // tk
    base = np_rows // s_new
    seg = min(tk, LANES)

    ck = cache_k.reshape(depth, nb, past * nh, HEAD_DIM)
    cv = cache_v.reshape(depth, nb, past * nh, HEAD_DIM)
    cache_spec = pl.BlockSpec((None, None, tk * nh, HEAD_DIM), lambda b, s: (layer, b, nkb - 1 - s, 0))
    return pl.pallas_call(
        functools.partial(_attn_s_kernel, rep=rep, seg=seg, nh=nh),
        grid=(nb, nkb),
        in_specs=[
            pl.BlockSpec((None, aw, LANES), lambda b, s: (b, 0, 0)),
            cache_spec,
            cache_spec,
            pl.BlockSpec((None, None, tk, LANES), lambda b, s: (layer, b, nkb - 1 - s, 0)),
            pl.BlockSpec((s_new, aw), lambda b, s: (base + b, 0)),
            pl.BlockSpec((s_new, aw), lambda b, s: (base + b, 0)),
            pl.BlockSpec((s_new, LANES), lambda b, s: (base + b, 0)),
            pl.BlockSpec(memory_space=pl.ANY),
        ],
        out_specs=pl.BlockSpec((s_new, aw), lambda b, s: (base + b, 0)),
        out_shape=jax.ShapeDtypeStruct(attn.shape, F32),
        scratch_shapes=[
            pltpu.VMEM((1, LANES), F32), pltpu.VMEM((1, LANES), F32), pltpu.VMEM((LANES, aw), F32),
            pltpu.VMEM((1, LANES), F32), pltpu.VMEM((1, LANES), F32),
        ],
        input_output_aliases={7: 0},
        compiler_params=_cparams("parallel", "arbitrary"),
        name="attn_s",
    )(qbd, ck, cv, cf, kf, vf, logf, attn)


def _s5_kernel(u_ref, kds2_ref, wc_ref, vc_ref, a_ref, hinit_ref, d_ref, wglu2_ref, o_ref, hend_ref,
               wfull_ref, vfull_ref, s_ref, hprev_ref, hc_ref, *, n_prompt_chunks):
    rt = pl.program_id(1)
    ct, sd = s_ref.shape
    half = sd // 2
    L = S5_CHUNK
    P = SSM_STATE

    @pl.when(rt == 0)
    def _():
        hc_ref[...] = jnp.zeros_like(hc_ref)
        r_, c_ = _iota2((2 * P, sd), 0), _iota2((2 * P, sd), 1)
        rep_w = jnp.logical_and(r_ // P == c_ // half, r_ % P == c_ % P).astype(BF16)
        r_, c_ = _iota2((L * LANES, sd), 0), _iota2((L * LANES, sd), 1)
        same = (r_ // SSM_CH) % GROUPS_PER_TILE == (c_ % half) // P
        wfull_ref[...] = jnp.where(same, _dot(wc_ref[...], rep_w), 0.0).astype(BF16)
        r_, c_ = _iota2((L * SSM_CH, L * LANES), 0), _iota2((L * SSM_CH, L * LANES), 1)
        rep_v = jnp.logical_and(r_ // SSM_CH == c_ // LANES, r_ % SSM_CH == c_ % SSM_CH).astype(BF16)
        r_, c_ = _iota2((sd, L * LANES), 0), _iota2((sd, L * LANES), 1)
        same = (r_ % half) // P == (c_ % LANES) // SSM_CH
        vfull_ref[...] = jnp.where(same, _dot(vc_ref[...], rep_v), 0.0).astype(BF16)

    def u_at(tau):
        return u_ref[pl.ds(tau, ct, stride=L), :]

    xr = jnp.concatenate([u_at(L - 1 - j).astype(BF16) for j in range(L)], axis=1)
    s_ref[...] = _dot(xr, wfull_ref[...])

    ar = a_ref[:, :half]
    ai = a_ref[:, half:]

    def scan(c, hcar):
        cg = rt * ct + c
        reset = jnp.logical_or(cg == 0, cg >= n_prompt_chunks)
        hp = jnp.where(reset, hinit_ref[pl.ds(c, 1), :], hcar)
        hprev_ref[pl.ds(c, 1), :] = hp
        s = s_ref[pl.ds(c, 1), :]
        hr = hp[:, :half]
        hi = hp[:, half:]
        hn = jnp.concatenate([ar * hr - ai * hi + s[:, :half], ar * hi + ai * hr + s[:, half:]], axis=1)
        hend_ref[pl.ds(c, 1), :] = hn
        return hn

    hc_ref[...] = lax.fori_loop(0, ct, scan, hc_ref[...], unroll=SCAN_UNROLL)

    ystate = _dot(hprev_ref[...].astype(BF16), vfull_ref[...])
    d2 = jnp.concatenate([d_ref[...], d_ref[...]], axis=1)
    wglu2 = wglu2_ref[...]
    for tau in range(0, L, 2):
        lag_rows = (tau + 2) * LANES
        yy = _dot(xr[:, (L - 2 - tau) * LANES:], kds2_ref[:lag_rows, :])
        yy = yy + jnp.concatenate([ystate[:, (tau + 1) * LANES:(tau + 2) * LANES],
                                   ystate[:, tau * LANES:(tau + 1) * LANES]], axis=1)
        yy = yy + d2 * jnp.concatenate([u_at(tau + 1), u_at(tau)], axis=1)
        yy = 0.5 * yy * (1.0 + jnp.tanh(math.sqrt(2.0 / math.pi) * (yy + 0.044715 * (yy * yy * yy))))
        out = yy * jax.nn.sigmoid(_dot(yy.astype(BF16), wglu2))
        o_ref[pl.ds(tau + 1, ct, stride=L), :] = out[:, :LANES]
        o_ref[pl.ds(tau, ct, stride=L), :] = out[:, LANES:]


def _s5(u, kds2, wc, vc, a16, hinit, d, wglu2, *, rows_tile, n_prompt_chunks):
    r, sw = u.shape
    no = sw // LANES
    L = S5_CHUNK
    ct = rows_tile // L
    nrt = r // rows_tile
    sd = a16.shape[2]
    return pl.pallas_call(
        functools.partial(_s5_kernel, n_prompt_chunks=n_prompt_chunks),
        grid=(no, nrt),
        in_specs=[
            pl.BlockSpec((rows_tile, LANES), lambda o, t: (t, o)),
            pl.BlockSpec((None, L * LANES, 2 * LANES), lambda o, t: (o, 0, 0)),
            pl.BlockSpec((None, L * LANES, 2 * SSM_STATE), lambda o, t: (o, 0, 0)),
            pl.BlockSpec((None, sd, L * SSM_CH), lambda o, t: (o, 0, 0)),
            pl.BlockSpec((None, 1, sd), lambda o, t: (o, 0, 0)),
            pl.BlockSpec((ct, sd), lambda o, t: (t, o)),
            pl.BlockSpec((1, LANES), lambda o, t: (0, o)),
            pl.BlockSpec((None, 2 * LANES, 2 * LANES), lambda o, t: (o, 0, 0)),
        ],
        out_specs=[
            pl.BlockSpec((rows_tile, LANES), lambda o, t: (t, o)),
            pl.BlockSpec((ct, sd), lambda o, t: (t, o)),
        ],
        out_shape=[
            jax.ShapeDtypeStruct((r, sw), F32),
            jax.ShapeDtypeStruct((r // L, no * sd), F32),
        ],
        scratch_shapes=[pltpu.VMEM((L * LANES, sd), BF16), pltpu.VMEM((sd, L * LANES), BF16),
                        pltpu.VMEM((ct, sd), F32), pltpu.VMEM((ct, sd), F32), pltpu.VMEM((1, sd), F32)],
        compiler_params=_cparams("parallel", "arbitrary"),
        name="s5",
    )(u, kds2, wc, vc, a16, hinit, d, wglu2)


def _s5_operators(a_re, a_im, log_dt, b_re, b_im, c_re, c_im, w_glu):
    g, p = a_re.shape
    c = SSM_CH
    L = S5_CHUNK
    gt = GROUPS_PER_TILE
    no = g // gt
    hp = lax.Precision.HIGHEST
    cmul = lambda xr, xi, yr, yi: (xr * yr - xi * yi, xr * yi + xi * yr)
    dt = jnp.exp(log_dt)[:, None]
    mag = jnp.exp(a_re * dt)
    ar, ai = mag * jnp.cos(a_im * dt), mag * jnp.sin(a_im * dt)
    den = a_re * a_re + a_im * a_im
    fr = ((ar - 1.0) * a_re + ai * a_im) / den
    fi = (ai * a_re - (ar - 1.0) * a_im) / den
    bbr, bbi = cmul(fr[..., None], fi[..., None], b_re, b_im)
    pr, pi = [jnp.ones_like(ar)], [jnp.zeros_like(ar)]
    for _ in range(L):
        nr, ni = cmul(pr[-1], pi[-1], ar, ai)
        pr.append(nr)
        pi.append(ni)
    pr, pi = jnp.stack(pr), jnp.stack(pi)

    abr, abi = cmul(pr[:L, :, :, None], pi[:L, :, :, None], bbr[None], bbi[None])
    kd = (jnp.einsum('gop,dgpi->gdio', c_re, abr, precision=hp)
          - jnp.einsum('gop,dgpi->gdio', c_im, abi, precision=hp))
    eye = jnp.eye(gt, dtype=F32)
    kd = kd.reshape(no, gt, L, c, c).transpose(0, 2, 1, 3, 4)
    kds = (kd[:, :, :, :, None, :] * eye[None, None, :, None, :, None]).reshape(no, L * LANES, LANES)
    kds2 = jnp.concatenate([kds, jnp.pad(kds, ((0, 0), (LANES, 0), (0, 0)))[:, :L * LANES]], axis=2)

    wc = jnp.stack([abr, abi], axis=0).reshape(2, L, no, gt, p, c)
    wc = wc.transpose(2, 1, 3, 5, 0, 4).reshape(no, L * LANES, 2 * p)

    zr, zi = cmul(c_re.transpose(0, 2, 1)[None], c_im.transpose(0, 2, 1)[None],
                  pr[1:, :, :, None], pi[1:, :, :, None])
    vc = jnp.stack([zr, -zi], axis=0).reshape(2, L, no, gt, p, c)
    vc = vc.transpose(2, 0, 3, 4, 1, 5).reshape(no, 2 * gt * p, L * c)

    a16 = jnp.concatenate([pr[L].reshape(no, 1, gt * p), pi[L].reshape(no, 1, gt * p)], axis=2)
    wg = w_glu.reshape(no, gt, c, c)
    wglu = (wg[:, :, :, None, :] * eye[None, :, None, :, None]).reshape(no, LANES, LANES)
    zero = jnp.zeros_like(wglu)
    wglu2 = jnp.concatenate([jnp.concatenate([wglu, zero], axis=2), jnp.concatenate([zero, wglu], axis=2)], axis=1)
    return kds2.astype(BF16), wc.astype(BF16), vc.astype(BF16), a16, wglu2.astype(BF16)


def _out_kernel(x_ref, attn_ref, ssm_ref, ga_ref, gs_ref, gpost_ref, wa_ref, ws_ref, o_ref):
    an = _rms(attn_ref[...], ga_ref[...]).astype(BF16)
    sn = _rms(ssm_ref[...], gs_ref[...]).astype(BF16)
    mixed = _dot(an, wa_ref[...]) + _dot(sn, ws_ref[...])
    o_ref[...] = x_ref[...] + _rms(mixed, gpost_ref[...])


def _out_proj(x, attn, ssm, ga, gs, gpost, w_out, layer, *, tm):
    r, d = x.shape
    aw = attn.shape[1]
    sw = ssm.shape[1]
    assert aw == sw
    once = dict(pipeline_mode=pl.Buffered(1))
    return pl.pallas_call(
        _out_kernel,
        grid=(r // tm,),
        in_specs=[
            pl.BlockSpec((tm, d), lambda i: (i, 0)),
            pl.BlockSpec((tm, aw), lambda i: (i, 0)),
            pl.BlockSpec((tm, sw), lambda i: (i, 0)),
            pl.BlockSpec((1, aw), lambda i: (0, 0)),
            pl.BlockSpec((1, sw), lambda i: (0, 0)),
            pl.BlockSpec((1, d), lambda i: (0, 0)),
            pl.BlockSpec((None, aw, d), lambda i: (layer, 0, 0), **once),
            pl.BlockSpec((None, sw, d), lambda i: (layer, 1, 0), **once),
        ],
        out_specs=pl.BlockSpec((tm, d), lambda i: (i, 0)),
        out_shape=jax.ShapeDtypeStruct((r, d), F32),
        compiler_params=_cparams("parallel"),
        name="out_proj",
    )(x, attn, ssm, ga, gs, gpost, w_out, w_out)


def _tiles(r_min):
    tm = 512
    r = -(-r_min // tm) * tm
    s5_tiles = 4 if (r // 4) % (8 * S5_CHUNK) == 0 else 1
    return dict(r=r, tm=tm, tq=tm, tf=512, s5_rows=r // s5_tiles, tk_cache=1024)


def _forward(x_prompt, x_sample, cache_k, cache_v, cache_logf, state_ssm_re, state_ssm_im, meta_tokens,
             ffn1_norm_pre, ffn1_norm_post, ffn1_w_gate, ffn1_w_up, ffn1_w_down,
             mix_norm_pre, mix_norm_post, w_in, b_forget,
             ssm_a_re, ssm_a_im, ssm_log_dt, ssm_b_re, ssm_b_im, ssm_c_re, ssm_c_im, ssm_d, ssm_w_glu,
             attn_out_norm, ssm_out_norm, w_out,
             ffn2_norm_pre, ffn2_norm_post, ffn2_w_gate, ffn2_w_up, ffn2_w_down, tiles=None):
    bsz, seq, d = x_prompt.shape
    nb, s_new, _ = x_sample.shape
    depth, _, past, nh, _ = cache_k.shape
    n_meta = meta_tokens.shape[0]
    aw = nh * HEAD_DIM
    g, p = ssm_a_re.shape[1:]
    L = S5_CHUNK
    assert bsz == 1 and s_new == L and n_meta % L == 0 and seq % L == 0 and p == SSM_STATE
    assert LANES % nh == 0 and s_new <= LANES // nh and g % GROUPS_PER_TILE == 0
    rep = LANES // nh
    no = g // GROUPS_PER_TILE
    sd = 2 * GROUPS_PER_TILE * p
    np_rows = n_meta + seq
    ns_rows = nb * s_new
    t = tiles or _tiles(np_rows + ns_rows)
    r = t["r"]
    npc = np_rows // L
    n_chunks = r // L

    x = jnp.concatenate([meta_tokens.astype(F32), x_prompt[0], x_sample.reshape(ns_rows, d),
                         jnp.zeros((r - np_rows - ns_rows, d), F32)], axis=0)
    cf = jnp.repeat(cache_logf, rep, axis=-1)

    row2 = lambda a: a.reshape(1, -1)
    ffn1_w = [w.astype(BF16) for w in (ffn1_w_gate, ffn1_w_up, ffn1_w_down)]
    ffn2_w = [w.astype(BF16) for w in (ffn2_w_gate, ffn2_w_up, ffn2_w_down)]
    w_out_b = w_out.astype(BF16)
    outs = dict(k=[], v=[], logf=[], hend=[])
    for l in range(depth):
        x = _ffn(x, row2(ffn1_norm_pre[l]), row2(ffn1_norm_post[l]), *ffn1_w, l, tm=t["tm"], tf=t["tf"])

        wl = w_in[l]
        w4 = jnp.stack([wl[:, :aw], wl[:, aw:2 * aw], wl[:, 2 * aw:3 * aw], wl[:, 3 * aw + nh:]]).astype(BF16)
        wf = jnp.repeat(wl[:, 3 * aw:3 * aw + nh], rep, axis=1).astype(BF16)
        bf = jnp.repeat(b_forget[l], rep).reshape(1, LANES)
        qf, kf, vf, uf, qa, ka, vt, logf = _in_proj(x, row2(mix_norm_pre[l]), w4, wf, bf, tm=t["tm"], rep=rep)

        attn = _attn_prompt(qa, ka, vt)

        qs = (qf[np_rows:np_rows + ns_rows] * ATTN_SCALE).astype(BF16).reshape(nb, s_new, nh, HEAD_DIM)
        qs = jnp.pad(qs.transpose(0, 2, 3, 1), ((0, 0), (0, 0), (0, 0), (0, rep - s_new)))
        qbd = (qs[:, :, :, None, :] * jnp.eye(nh, dtype=BF16)[None, :, None, :, None]).reshape(nb, aw, LANES)
        attn = _attn_sample(l, qbd, cache_k, cache_v, cf, kf, vf, logf, attn,
                            np_rows=np_rows, tk=t["tk_cache"], rep=rep)

        kds2, wc, vc, a16, wglu2 = _s5_operators(ssm_a_re[l], ssm_a_im[l], ssm_log_dt[l], ssm_b_re[l],
                                                 ssm_b_im[l], ssm_c_re[l], ssm_c_im[l], ssm_w_glu[l])
        h0 = jnp.concatenate([state_ssm_re[l].reshape(nb, no, 1, sd // 2),
                              state_ssm_im[l].reshape(nb, no, 1, sd // 2)], axis=2).reshape(nb, no * sd)
        hinit = jnp.zeros((n_chunks, no * sd), F32).at[npc:npc + nb].set(h0)
        ssm, hend = _s5(uf, kds2, wc, vc, a16, hinit, row2(ssm_d[l]), wglu2,
                        rows_tile=t["s5_rows"], n_prompt_chunks=npc)

        x = _out_proj(x, attn, ssm, row2(attn_out_norm[l]), row2(ssm_out_norm[l]), row2(mix_norm_post[l]),
                      w_out_b, l, tm=t["tm"])

        x = _ffn(x, row2(ffn2_norm_pre[l]), row2(ffn2_norm_post[l]), *ffn2_w, l, tm=t["tm"], tf=t["tf"])

        outs["k"].append(kf)
        outs["v"].append(vf)
        outs["logf"].append(logf[:, ::rep])
        outs["hend"].append(hend[npc - 1:npc + nb].reshape(1 + nb, no, 2, GROUPS_PER_TILE, p))

    sl_p = slice(0, np_rows)
    sl_s = slice(np_rows, np_rows + ns_rows)
    rows = lambda arrs, sl: jnp.stack([a[sl] for a in arrs])
    heads = lambda a, n: a.reshape(depth, -1, n, nh, HEAD_DIM)
    states = lambda a: a.reshape(depth, -1, g, p)
    h_all = jnp.stack(outs["hend"])
    return (x[n_meta:np_rows][None], x[sl_s].reshape(nb, s_new, d),
            heads(rows(outs["k"], sl_p), np_rows), heads(rows(outs["v"], sl_p), np_rows),
            rows(outs["logf"], sl_p).reshape(depth, 1, np_rows, nh),
            states(h_all[:, :1, :, 0]), states(h_all[:, :1, :, 1]),
            heads(rows(outs["k"], sl_s), s_new), heads(rows(outs["v"], sl_s), s_new),
            rows(outs["logf"], sl_s).reshape(depth, nb, s_new, nh),
            states(h_all[:, 1:, :, 0]), states(h_all[:, 1:, :, 1]))


def kernel(x_prompt, x_sample, cache_k, cache_v, cache_logf, state_ssm_re, state_ssm_im, meta_tokens, ffn1_norm_pre, ffn1_norm_post, ffn1_w_gate, ffn1_w_up, ffn1_w_down, mix_norm_pre, mix_norm_post, w_in, b_forget, ssm_a_re, ssm_a_im, ssm_log_dt, ssm_b_re, ssm_b_im, ssm_c_re, ssm_c_im, ssm_d, ssm_w_glu, attn_out_norm, ssm_out_norm, w_out, ffn2_norm_pre, ffn2_norm_post, ffn2_w_gate, ffn2_w_up, ffn2_w_down):
    return _forward(x_prompt, x_sample, cache_k, cache_v, cache_logf, state_ssm_re, state_ssm_im, meta_tokens,
                    ffn1_norm_pre, ffn1_norm_post, ffn1_w_gate, ffn1_w_up, ffn1_w_down,
                    mix_norm_pre, mix_norm_post, w_in, b_forget,
                    ssm_a_re, ssm_a_im, ssm_log_dt, ssm_b_re, ssm_b_im, ssm_c_re, ssm_c_im, ssm_d, ssm_w_glu,
                    attn_out_norm, ssm_out_norm, w_out,
                    ffn2_norm_pre, ffn2_norm_post, ffn2_w_gate, ffn2_w_up, ffn2_w_down)
```

```python
import functools
import math

import jax
import jax.numpy as jnp
from jax import lax
from jax.experimental import pallas as pl
from jax.experimental.pallas import tpu as pltpu

F32 = jnp.float32
BF16 = jnp.bfloat16

HEAD_DIM = 128
SSM_CH = 16
SSM_STATE = 64
S5_CHUNK = 16
LANES = 128
GROUPS_PER_TILE = LANES // SSM_CH
MACARON_W = 0.5
NORM_EPS = 1e-6
ATTN_SCALE = HEAD_DIM ** -0.5
LOG2E = math.log2(math.e)
NEG_INF = -1e30
VMEM_LIMIT = 56 * 1024 * 1024
N_BIAS_TERMS = 3
SCAN_UNROLL = 4


def _cparams(*sem):
    return pltpu.CompilerParams(dimension_semantics=sem, vmem_limit_bytes=VMEM_LIMIT)


def _rms(x, g):
    return x * lax.rsqrt(jnp.mean(x * x, axis=-1, keepdims=True) + NORM_EPS) * g


def _dot(a, b):
    return jnp.dot(a, b, preferred_element_type=F32)


def _dot_nt(a, b):
    return lax.dot_general(a, b, (((1,), (1,)), ((), ())), preferred_element_type=F32)


def _split3(x):
    hi = x.astype(BF16)
    r1 = x - hi.astype(F32)
    mid = r1.astype(BF16)
    lo = (r1 - mid.astype(F32)).astype(BF16)
    return hi, mid, lo


def _dot_exact_lhs01(sel, x):
    hi, mid, lo = _split3(x)
    return _dot(sel, hi) + _dot(sel, mid) + _dot(sel, lo)


def _iota2(shape, axis):
    return lax.broadcasted_iota(jnp.int32, shape, axis)


def _ffn_kernel(x_ref, gpre_ref, gpost_ref, wg_ref, wu_ref, wd_ref, o_ref, xn_ref, acc_ref):
    j = pl.program_id(1)

    @pl.when(j == 0)
    def _():
        xn_ref[...] = _rms(x_ref[...], gpre_ref[...]).astype(BF16)
        acc_ref[...] = jnp.zeros_like(acc_ref)

    xn = xn_ref[...]
    g = _dot(xn, wg_ref[...])
    u = _dot(xn, wu_ref[...])
    h = (g * jax.nn.sigmoid(g) * u).astype(BF16)
    acc_ref[...] += _dot(h, wd_ref[...])

    @pl.when(j == pl.num_programs(1) - 1)
    def _():
        o_ref[...] = x_ref[...] + MACARON_W * _rms(acc_ref[...], gpost_ref[...])


def _ffn(x, g_pre, g_post, wg, wu, wd, layer, *, tm, tf):
    r, d = x.shape
    f = wg.shape[2]
    return pl.pallas_call(
        _ffn_kernel,
        grid=(r // tm, f // tf),
        in_specs=[
            pl.BlockSpec((tm, d), lambda i, j: (i, 0)),
            pl.BlockSpec((1, d), lambda i, j: (0, 0)),
            pl.BlockSpec((1, d), lambda i, j: (0, 0)),
            pl.BlockSpec((None, d, tf), lambda i, j: (layer, 0, j)),
            pl.BlockSpec((None, d, tf), lambda i, j: (layer, 0, j)),
            pl.BlockSpec((None, tf, d), lambda i, j: (layer, j, 0)),
        ],
        out_specs=pl.BlockSpec((tm, d), lambda i, j: (i, 0)),
        out_shape=jax.ShapeDtypeStruct((r, d), F32),
        scratch_shapes=[pltpu.VMEM((tm, d), BF16), pltpu.VMEM((tm, d), F32)],
        compiler_params=_cparams("parallel", "arbitrary"),
        name="ffn",
    )(x, g_pre, g_post, wg, wu, wd)


def _in_kernel(x_ref, g_ref, w_ref, wf_ref, bf_ref, qf_ref, kf_ref, vf_ref, uf_ref, qa_ref, ka_ref, vt_ref, logf_ref,
               xn_ref, carry_ref, qaug_ref, kaug_ref, *, rep):
    i = pl.program_id(0)
    j = pl.program_id(1)
    tm = x_ref.shape[0]
    aw = w_ref.shape[2]
    nh = aw // HEAD_DIM
    nb = N_BIAS_TERMS

    @pl.when(j == 0)
    def _():
        xn = _rms(x_ref[...], g_ref[...]).astype(BF16)
        xn_ref[...] = xn
        logf = jax.nn.log_sigmoid(_dot(xn, wf_ref[...]) + bf_ref[...])
        logf_ref[...] = logf

        @pl.when(i == 0)
        def _():
            carry_ref[...] = jnp.zeros_like(carry_ref)

        tril = (_iota2((tm, tm), 1) <= _iota2((tm, tm), 0)).astype(BF16)
        fcum = _dot_exact_lhs01(tril, logf) + carry_ref[...]
        carry_ref[...] = fcum[tm - 1:tm, :]

        terms = jnp.concatenate(_split3(fcum * LOG2E), axis=1)
        src = _iota2((nb * LANES, aw), 0)
        dst = _iota2((nb * LANES, aw), 1)
        head_src = src % LANES == (dst // HEAD_DIM) * rep
        lane = dst % HEAD_DIM
        kaug = -_dot(terms, jnp.logical_and(head_src, lane == src // LANES).astype(BF16))
        qaug = _dot(terms, jnp.logical_and(head_src, lane == nb + src // LANES).astype(BF16))
        lane_t = _iota2((tm, aw), 1) % HEAD_DIM
        kaug_ref[...] = (kaug + jnp.logical_and(lane_t >= nb, lane_t < 2 * nb).astype(F32)).astype(BF16)
        qaug_ref[...] = (qaug + (lane_t < nb).astype(F32)).astype(BF16)

    z = _dot(xn_ref[...], w_ref[0])
    for sec, dst_ref in ((0, qf_ref), (3, uf_ref)):
        @pl.when(j == sec)
        def _(dst_ref=dst_ref):
            dst_ref[...] = z

    for sec, dst_ref in ((1, kf_ref), (2, vf_ref)):
        @pl.when(j == sec)
        def _(dst_ref=dst_ref):
            for h in range(nh):
                dst_ref[pl.ds(h, tm, stride=nh), :] = z[:, h * HEAD_DIM:(h + 1) * HEAD_DIM]

    def interleave(dst_ref, zb, aug_ref):
        for h in range(nh):
            dst_ref[:, 2 * h * HEAD_DIM:(2 * h + 1) * HEAD_DIM] = zb[:, h * HEAD_DIM:(h + 1) * HEAD_DIM]
            dst_ref[:, (2 * h + 1) * HEAD_DIM:(2 * h + 2) * HEAD_DIM] = aug_ref[:, h * HEAD_DIM:(h + 1) * HEAD_DIM]

    @pl.when(j == 0)
    def _():
        interleave(qa_ref, (z * (ATTN_SCALE * LOG2E)).astype(BF16), qaug_ref)

    @pl.when(j == 1)
    def _():
        interleave(ka_ref, z.astype(BF16), kaug_ref)

    @pl.when(j == 2)
    def _():
        for h in range(nh):
            vt_ref[h] = jnp.transpose(z[:, h * HEAD_DIM:(h + 1) * HEAD_DIM]).astype(BF16)


def _in_proj(x, g, w4, wf, bf, *, tm, rep):
    r, d = x.shape
    nsec, _, aw = w4.shape
    nh = aw // HEAD_DIM
    return pl.pallas_call(
        functools.partial(_in_kernel, rep=rep),
        grid=(r // tm, nsec),
        in_specs=[
            pl.BlockSpec((tm, d), lambda i, j: (i, 0)),
            pl.BlockSpec((1, d), lambda i, j: (0, 0)),
            pl.BlockSpec((1, d, aw), lambda i, j: (j, 0, 0)),
            pl.BlockSpec((d, LANES), lambda i, j: (0, 0)),
            pl.BlockSpec((1, LANES), lambda i, j: (0, 0)),
        ],
        out_specs=[
            pl.BlockSpec((tm, aw), lambda i, j: (i, 0)),
            pl.BlockSpec((tm * nh, HEAD_DIM), lambda i, j: (i, 0)),
            pl.BlockSpec((tm * nh, HEAD_DIM), lambda i, j: (i, 0)),
            pl.BlockSpec((tm, aw), lambda i, j: (i, 0)),
            pl.BlockSpec((tm, 2 * aw), lambda i, j: (i, 0)),
            pl.BlockSpec((tm, 2 * aw), lambda i, j: (i, 0)),
            pl.BlockSpec((aw // HEAD_DIM, None, HEAD_DIM, tm), lambda i, j: (0, i, 0, 0)),
            pl.BlockSpec((tm, LANES), lambda i, j: (i, 0)),
        ],
        out_shape=[
            jax.ShapeDtypeStruct((r, aw), F32),
            jax.ShapeDtypeStruct((r * nh, HEAD_DIM), F32),
            jax.ShapeDtypeStruct((r * nh, HEAD_DIM), F32),
            jax.ShapeDtypeStruct((r, aw), F32),
            jax.ShapeDtypeStruct((r, 2 * aw), BF16),
            jax.ShapeDtypeStruct((r, 2 * aw), BF16),
            jax.ShapeDtypeStruct((aw // HEAD_DIM, r // tm, HEAD_DIM, tm), BF16),
            jax.ShapeDtypeStruct((r, LANES), F32),
        ],
        scratch_shapes=[pltpu.VMEM((tm, d), BF16), pltpu.VMEM((1, LANES), F32),
                        pltpu.VMEM((tm, aw), BF16), pltpu.VMEM((tm, aw), BF16)],
        compiler_params=_cparams("arbitrary", "arbitrary"),
        name="in_proj",
    )(x, g, w4, wf, bf)


def _attn_p_kernel(qa_ref, ka_ref, vt_ref, o_ref, sa_ref, sb_ref, acc_ref):
    qi = pl.program_id(1)
    tq = qa_ref.shape[0]
    tk = sa_ref.shape[0]
    n_diag = tq // tk
    n_off = qi * n_diag

    def produce(s_ref, kb):
        s = _dot_nt(ka_ref[pl.ds(pl.multiple_of(kb * tk, tk), tk), :], qa_ref[...])
        s_ref[...] = s
        return jnp.max(s, axis=0, keepdims=True)

    def consume(s_ref, m_blk, kb, carry):
        m, l = carry
        m_new = jnp.maximum(m, m_blk)
        p = jnp.exp2(s_ref[...] - m_new)
        alpha = jnp.exp2(m - m_new)
        acc_ref[...] = alpha * acc_ref[...] + _dot(vt_ref[kb], p.astype(BF16))
        return m_new, alpha * l + jnp.sum(p, axis=0, keepdims=True)

    acc_ref[...] = jnp.zeros_like(acc_ref)
    m_first = produce(sa_ref, 0)

    def pair(kb, carry):
        m_a, m, l = carry
        m_b = produce(sb_ref, kb + 1)
        m, l = consume(sa_ref, m_a, kb, (m, l))
        m_a = produce(sa_ref, kb + 2)
        m, l = consume(sb_ref, m_b, kb + 1, (m, l))
        return m_a, m, l

    init = (m_first, jnp.full((1, tq), NEG_INF, F32), jnp.zeros((1, tq), F32))
    carry = lax.fori_loop(0, n_off // 4, lambda i, c: pair(4 * i + 2, pair(4 * i, c)), init)
    done = (n_off // 4) * 4
    m_a, m, l = lax.fori_loop(0, (n_off % 4) // 2, lambda i, c: pair(done + 2 * i, c), carry)
    lead = _iota2((tk, tq), 1) - _iota2((tk, tq), 0)

    def consume_diag(s_ref, kb, carry):
        s = jnp.where(lead >= kb * tk - qi * tq, s_ref[...], NEG_INF)
        s_ref[...] = s
        return consume(s_ref, jnp.max(s, axis=0, keepdims=True), kb, carry)

    def tail(first, m_first, carry):
        bufs = (sa_ref, sb_ref)
        count = n_diag + (1 if first is not None else 0)
        start = n_off - (1 if first is not None else 0)
        for idx in range(count):
            cur = bufs[idx % 2]
            if idx + 1 < count:
                m_next = produce(bufs[(idx + 1) % 2], start + idx + 1)
            if first is not None and idx == 0:
                carry = consume(cur, m_first, start, carry)
            else:
                carry = consume_diag(cur, start + idx, carry)
        o_ref[...] = jnp.transpose(acc_ref[...] / carry[1])

    @pl.when(n_off % 2 == 0)
    def _():
        tail(None, None, (m, l))

    @pl.when(n_off % 2 == 1)
    def _():
        tail(n_off - 1, m_a, (m, l))


def _attn_prompt(qa, ka, vt, *, tq):
    nh, nkb, _, tk = vt.shape
    r = nkb * tk
    return pl.pallas_call(
        _attn_p_kernel,
        grid=(nh, r // tq),
        in_specs=[
            pl.BlockSpec((tq, 2 * HEAD_DIM), lambda h, i: (i, h)),
            pl.BlockSpec((r, 2 * HEAD_DIM), lambda h, i: (0, h)),
            pl.BlockSpec((None, nkb, HEAD_DIM, tk), lambda h, i: (h, 0, 0, 0)),
        ],
        out_specs=pl.BlockSpec((tq, HEAD_DIM), lambda h, i: (i, h)),
        out_shape=jax.ShapeDtypeStruct((r, nh * HEAD_DIM), F32),
        scratch_shapes=[pltpu.VMEM((tk, tq), F32), pltpu.VMEM((tk, tq), F32), pltpu.VMEM((HEAD_DIM, tq), F32)],
        compiler_params=_cparams("parallel", "arbitrary"),
        name="attn_p",
    )(qa, ka, vt)


def _attn_s_kernel(qbd_ref, ck_ref, cv_ref, cf_ref, kn_ref, vn_ref, fn_ref, attn_in_ref, o_ref,
                   m_ref, l_ref, acc_ref, suf_ref, cq_ref, *, rep, seg, nh):
    del attn_in_ref
    step = pl.program_id(1)
    s_new = fn_ref.shape[0]
    tk = cf_ref.shape[0]
    qbd = qbd_ref[...]

    def heads_side_by_side(ref, n):
        return jnp.concatenate([ref[pl.ds(h, n, stride=nh), :].astype(BF16) for h in range(nh)], axis=1)

    def update(t, v_bf):
        m_old = m_ref[...]
        m_new = jnp.maximum(m_old, jnp.max(t, axis=0, keepdims=True))
        p = jnp.exp(t - m_new)
        alpha = jnp.exp(m_old - m_new)
        l_ref[...] = alpha * l_ref[...] + jnp.sum(p, axis=0, keepdims=True)
        m_ref[...] = m_new
        alpha_col = jnp.transpose(jnp.broadcast_to(alpha, (LANES, LANES)))
        pv = _dot(jnp.transpose(p).astype(BF16), v_bf)
        for c in range(nh):
            sl = slice(c * HEAD_DIM, (c + 1) * HEAD_DIM)
            acc_ref[:, sl] = acc_ref[:, sl] * alpha_col + pv[:, sl]

    @pl.when(step == 0)
    def _():
        m_ref[...] = jnp.full_like(m_ref, NEG_INF)
        l_ref[...] = jnp.zeros_like(l_ref)
        acc_ref[...] = jnp.zeros_like(acc_ref)
        tril = (_iota2((s_new, s_new), 1) <= _iota2((s_new, s_new), 0)).astype(BF16)
        cn = _dot_exact_lhs01(tril, fn_ref[...])
        krow = _iota2((s_new, LANES), 0)
        qlane = _iota2((s_new, LANES), 1) % rep
        cq = jnp.sum(jnp.where(krow == qlane, cn, 0.0), axis=0, keepdims=True)
        cq_ref[...] = cq
        suf_ref[...] = jnp.zeros_like(suf_ref)
        st = _dot(heads_side_by_side(kn_ref, s_new), qbd)
        t = jnp.where(krow <= qlane, st + (cq - cn), NEG_INF)
        update(t, heads_side_by_side(vn_ref, s_new))

    triu = (_iota2((seg, seg), 1) > _iota2((seg, seg), 0)).astype(BF16)
    carry = suf_ref[...]
    sufs = [None] * (tk // seg)
    for sidx in reversed(range(tk // seg)):
        x = cf_ref[sidx * seg:(sidx + 1) * seg, :]
        sfx = _dot_exact_lhs01(triu, x) + carry
        sufs[sidx] = sfx
        carry = sfx[0:1, :] + x[0:1, :]
    suf_ref[...] = carry
    bias = jnp.concatenate(sufs, axis=0) + cq_ref[...]
    update(_dot(heads_side_by_side(ck_ref, tk), qbd) + bias, heads_side_by_side(cv_ref, tk))

    @pl.when(step == pl.num_programs(1) - 1)
    def _():
        l_col = jnp.transpose(jnp.broadcast_to(l_ref[...], (LANES, LANES)))
        for c in range(nh):
            rws = slice(c * rep, c * rep + s_new)
            sl = slice(c * HEAD_DIM, (c + 1) * HEAD_DIM)
            o_ref[:, sl] = acc_ref[rws, sl] / l_col[rws, :]


def _attn_sample(layer, qbd, cache_k, cache_v, cf, kf, vf, logf, attn, *, np_rows, tk, rep):
    depth, nb, past, nh, _ = cache_k.shape
    aw = nh * HEAD_DIM
    s_new = S5_CHUNK
    nkb = past // tk
    base = np_rows // s_new
    seg = min(tk, LANES)

    ck = cache_k.reshape(depth, nb, past * nh, HEAD_DIM)
    cv = cache_v.reshape(depth, nb, past * nh, HEAD_DIM)
    cache_spec = pl.BlockSpec((None, None, tk * nh, HEAD_DIM), lambda b, s: (layer, b, nkb - 1 - s, 0))
    return pl.pallas_call(
        functools.partial(_attn_s_kernel, rep=rep, seg=seg, nh=nh),
        grid=(nb, nkb),
        in_specs=[
            pl.BlockSpec((None, aw, LANES), lambda b, s: (b, 0, 0)),
            cache_spec,
            cache_spec,
            pl.BlockSpec((None, None, tk, LANES), lambda b, s: (layer, b, nkb - 1 - s, 0)),
            pl.BlockSpec((s_new * nh, HEAD_DIM), lambda b, s: (base + b, 0)),
            pl.BlockSpec((s_new * nh, HEAD_DIM), lambda b, s: (base + b, 0)),
            pl.BlockSpec((s_new, LANES), lambda b, s: (base + b, 0)),
            pl.BlockSpec(memory_space=pl.ANY),
        ],
        out_specs=pl.BlockSpec((s_new, aw), lambda b, s: (base + b, 0)),
        out_shape=jax.ShapeDtypeStruct(attn.shape, F32),
        scratch_shapes=[
            pltpu.VMEM((1, LANES), F32), pltpu.VMEM((1, LANES), F32), pltpu.VMEM((LANES, aw), F32),
            pltpu.VMEM((1, LANES), F32), pltpu.VMEM((1, LANES), F32),
        ],
        input_output_aliases={7: 0},
        compiler_params=_cparams("parallel", "arbitrary"),
        name="attn_s",
    )(qbd, ck, cv, cf, kf, vf, logf, attn)


def _s5_kernel(u_ref, kds2_ref, wc_ref, vc_ref, a_ref, hinit_ref, d_ref, wglu2_ref, o_ref, hend_ref,
               wfull_ref, vfull_ref, s_ref, hprev_ref, hc_ref, *, n_prompt_chunks):
    rt = pl.program_id(1)
    ct, sd = s_ref.shape
    half = sd // 2
    L = S5_CHUNK
    P = SSM_STATE

    @pl.when(rt == 0)
    def _():
        hc_ref[...] = jnp.zeros_like(hc_ref)
        r_, c_ = _iota2((2 * P, sd), 0), _iota2((2 * P, sd), 1)
        rep_w = jnp.logical_and(r_ // P == c_ // half, r_ % P == c_ % P).astype(BF16)
        r_, c_ = _iota2((L * LANES, sd), 0), _iota2((L * LANES, sd), 1)
        same = (r_ // SSM_CH) % GROUPS_PER_TILE == (c_ % half) // P
        wfull_ref[...] = jnp.where(same, _dot(wc_ref[...], rep_w), 0.0).astype(BF16)
        r_, c_ = _iota2((L * SSM_CH, L * LANES), 0), _iota2((L * SSM_CH, L * LANES), 1)
        rep_v = jnp.logical_and(r_ // SSM_CH == c_ // LANES, r_ % SSM_CH == c_ % SSM_CH).astype(BF16)
        r_, c_ = _iota2((sd, L * LANES), 0), _iota2((sd, L * LANES), 1)
        same = (r_ % half) // P == (c_ % LANES) // SSM_CH
        vfull_ref[...] = jnp.where(same, _dot(vc_ref[...], rep_v), 0.0).astype(BF16)

    def u_at(tau):
        return u_ref[pl.ds(tau, ct, stride=L), :]

    xr = jnp.concatenate([u_at(L - 1 - j).astype(BF16) for j in range(L)], axis=1)
    s_ref[...] = _dot(xr, wfull_ref[...])

    ar = a_ref[:, :half]
    ai = a_ref[:, half:]

    def scan(c, hcar):
        cg = rt * ct + c
        reset = jnp.logical_or(cg == 0, cg >= n_prompt_chunks)
        hp = jnp.where(reset, hinit_ref[pl.ds(c, 1), :], hcar)
        hprev_ref[pl.ds(c, 1), :] = hp
        s = s_ref[pl.ds(c, 1), :]
        hr = hp[:, :half]
        hi = hp[:, half:]
        hn = jnp.concatenate([ar * hr - ai * hi + s[:, :half], ar * hi + ai * hr + s[:, half:]], axis=1)
        hend_ref[pl.ds(c, 1), :] = hn
        return hn

    hc_ref[...] = lax.fori_loop(0, ct, scan, hc_ref[...], unroll=SCAN_UNROLL)

    ystate = _dot(hprev_ref[...].astype(BF16), vfull_ref[...])
    d2 = jnp.concatenate([d_ref[...], d_ref[...]], axis=1)
    wglu2 = wglu2_ref[...]
    for tau in range(0, L, 2):
        lag_rows = (tau + 2) * LANES
        yy = _dot(xr[:, (L - 2 - tau) * LANES:], kds2_ref[:lag_rows, :])
        yy = yy + jnp.concatenate([ystate[:, (tau + 1) * LANES:(tau + 2) * LANES],
                                   ystate[:, tau * LANES:(tau + 1) * LANES]], axis=1)
        yy = yy + d2 * jnp.concatenate([u_at(tau + 1), u_at(tau)], axis=1)
        yy = 0.5 * yy * (1.0 + jnp.tanh(math.sqrt(2.0 / math.pi) * (yy + 0.044715 * (yy * yy * yy))))
        out = yy * jax.nn.sigmoid(_dot(yy.astype(BF16), wglu2))
        o_ref[pl.ds(tau + 1, ct, stride=L), :] = out[:, :LANES]
        o_ref[pl.ds(tau, ct, stride=L), :] = out[:, LANES:]


def _s5(u, kds2, wc, vc, a16, hinit, d, wglu2, *, rows_tile, n_prompt_chunks):
    r, sw = u.shape
    no = sw // LANES
    L = S5_CHUNK
    ct = rows_tile // L
    nrt = r // rows_tile
    sd = a16.shape[2]
    return pl.pallas_call(
        functools.partial(_s5_kernel, n_prompt_chunks=n_prompt_chunks),
        grid=(no, nrt),
        in_specs=[
            pl.BlockSpec((rows_tile, LANES), lambda o, t: (t, o)),
            pl.BlockSpec((None, L * LANES, 2 * LANES), lambda o, t: (o, 0, 0)),
            pl.BlockSpec((None, L * LANES, 2 * SSM_STATE), lambda o, t: (o, 0, 0)),
            pl.BlockSpec((None, sd, L * SSM_CH), lambda o, t: (o, 0, 0)),
            pl.BlockSpec((None, 1, sd), lambda o, t: (o, 0, 0)),
            pl.BlockSpec((ct, sd), lambda o, t: (t, o)),
            pl.BlockSpec((1, LANES), lambda o, t: (0, o)),
            pl.BlockSpec((None, 2 * LANES, 2 * LANES), lambda o, t: (o, 0, 0)),
        ],
        out_specs=[
            pl.BlockSpec((rows_tile, LANES), lambda o, t: (t, o)),
            pl.BlockSpec((ct, sd), lambda o, t: (t, o)),
        ],
        out_shape=[
            jax.ShapeDtypeStruct((r, sw), F32),
            jax.ShapeDtypeStruct((r // L, no * sd), F32),
        ],
        scratch_shapes=[pltpu.VMEM((L * LANES, sd), BF16), pltpu.VMEM((sd, L * LANES), BF16),
                        pltpu.VMEM((ct, sd), F32), pltpu.VMEM((ct, sd), F32), pltpu.VMEM((1, sd), F32)],
        compiler_params=_cparams("parallel", "arbitrary"),
        name="s5",
    )(u, kds2, wc, vc, a16, hinit, d, wglu2)


def _s5_operators(a_re, a_im, log_dt, b_re, b_im, c_re, c_im, w_glu):
    g, p = a_re.shape
    c = SSM_CH
    L = S5_CHUNK
    gt = GROUPS_PER_TILE
    no = g // gt
    hp = lax.Precision.HIGHEST
    cmul = lambda xr, xi, yr, yi: (xr * yr - xi * yi, xr * yi + xi * yr)
    dt = jnp.exp(log_dt)[:, None]
    mag = jnp.exp(a_re * dt)
    ar, ai = mag * jnp.cos(a_im * dt), mag * jnp.sin(a_im * dt)
    den = a_re * a_re + a_im * a_im
    fr = ((ar - 1.0) * a_re + ai * a_im) / den
    fi = (ai * a_re - (ar - 1.0) * a_im) / den
    bbr, bbi = cmul(fr[..., None], fi[..., None], b_re, b_im)
    pr, pi = [jnp.ones_like(ar)], [jnp.zeros_like(ar)]
    for _ in range(L):
        nr, ni = cmul(pr[-1], pi[-1], ar, ai)
        pr.append(nr)
        pi.append(ni)
    pr, pi = jnp.stack(pr), jnp.stack(pi)

    abr, abi = cmul(pr[:L, :, :, None], pi[:L, :, :, None], bbr[None], bbi[None])
    kd = (jnp.einsum('gop,dgpi->gdio', c_re, abr, precision=hp)
          - jnp.einsum('gop,dgpi->gdio', c_im, abi, precision=hp))
    eye = jnp.eye(gt, dtype=F32)
    kd = kd.reshape(no, gt, L, c, c).transpose(0, 2, 1, 3, 4)
    kds = (kd[:, :, :, :, None, :] * eye[None, None, :, None, :, None]).reshape(no, L * LANES, LANES)
    kds2 = jnp.concatenate([kds, jnp.pad(kds, ((0, 0), (LANES, 0), (0, 0)))[:, :L * LANES]], axis=2)

    wc = jnp.stack([abr, abi], axis=0).reshape(2, L, no, gt, p, c)
    wc = wc.transpose(2, 1, 3, 5, 0, 4).reshape(no, L * LANES, 2 * p)

    zr, zi = cmul(c_re.transpose(0, 2, 1)[None], c_im.transpose(0, 2, 1)[None],
                  pr[1:, :, :, None], pi[1:, :, :, None])
    vc = jnp.stack([zr, -zi], axis=0).reshape(2, L, no, gt, p, c)
    vc = vc.transpose(2, 0, 3, 4, 1, 5).reshape(no, 2 * gt * p, L * c)

    a16 = jnp.concatenate([pr[L].reshape(no, 1, gt * p), pi[L].reshape(no, 1, gt * p)], axis=2)
    wg = w_glu.reshape(no, gt, c, c)
    wglu = (wg[:, :, :, None, :] * eye[None, :, None, :, None]).reshape(no, LANES, LANES)
    zero = jnp.zeros_like(wglu)
    wglu2 = jnp.concatenate([jnp.concatenate([wglu, zero], axis=2), jnp.concatenate([zero, wglu], axis=2)], axis=1)
    return kds2.astype(BF16), wc.astype(BF16), vc.astype(BF16), a16, wglu2.astype(BF16)


def _out_kernel(x_ref, attn_ref, ssm_ref, ga_ref, gs_ref, gpost_ref, wa_ref, ws_ref, o_ref):
    an = _rms(attn_ref[...], ga_ref[...]).astype(BF16)
    sn = _rms(ssm_ref[...], gs_ref[...]).astype(BF16)
    mixed = _dot(an, wa_ref[...]) + _dot(sn, ws_ref[...])
    o_ref[...] = x_ref[...] + _rms(mixed, gpost_ref[...])


def _out_proj(x, attn, ssm, ga, gs, gpost, w_out, layer, *, tm):
    r, d = x.shape
    aw = attn.shape[1]
    sw = ssm.shape[1]
    assert aw == sw
    once = dict(pipeline_mode=pl.Buffered(1))
    return pl.pallas_call(
        _out_kernel,
        grid=(r // tm,),
        in_specs=[
            pl.BlockSpec((tm, d), lambda i: (i, 0)),
            pl.BlockSpec((tm, aw), lambda i: (i, 0)),
            pl.BlockSpec((tm, sw), lambda i: (i, 0)),
            pl.BlockSpec((1, aw), lambda i: (0, 0)),
            pl.BlockSpec((1, sw), lambda i: (0, 0)),
            pl.BlockSpec((1, d), lambda i: (0, 0)),
            pl.BlockSpec((None, aw, d), lambda i: (layer, 0, 0), **once),
            pl.BlockSpec((None, sw, d), lambda i: (layer, 1, 0), **once),
        ],
        out_specs=pl.BlockSpec((tm, d), lambda i: (i, 0)),
        out_shape=jax.ShapeDtypeStruct((r, d), F32),
        compiler_params=_cparams("parallel"),
        name="out_proj",
    )(x, attn, ssm, ga, gs, gpost, w_out, w_out)


def _tiles(r_min):
    tm = 512
    r = -(-r_min // tm) * tm
    s5_tiles = 4 if (r // 4) % (8 * S5_CHUNK) == 0 else 1
    tq = 3 * tm if r % (3 * tm) == 0 else tm
    return dict(r=r, tm=tm, tq=tq, tf=512, s5_rows=r // s5_tiles, tk_cache=1024)


def _forward(x_prompt, x_sample, cache_k, cache_v, cache_logf, state_ssm_re, state_ssm_im, meta_tokens,
             ffn1_norm_pre, ffn1_norm_post, ffn1_w_gate, ffn1_w_up, ffn1_w_down,
             mix_norm_pre, mix_norm_post, w_in, b_forget,
             ssm_a_re, ssm_a_im, ssm_log_dt, ssm_b_re, ssm_b_im, ssm_c_re, ssm_c_im, ssm_d, ssm_w_glu,
             attn_out_norm, ssm_out_norm, w_out,
             ffn2_norm_pre, ffn2_norm_post, ffn2_w_gate, ffn2_w_up, ffn2_w_down, tiles=None):
    bsz, seq, d = x_prompt.shape
    nb, s_new, _ = x_sample.shape
    depth, _, past, nh, _ = cache_k.shape
    n_meta = meta_tokens.shape[0]
    aw = nh * HEAD_DIM
    g, p = ssm_a_re.shape[1:]
    L = S5_CHUNK
    assert bsz == 1 and s_new == L and n_meta % L == 0 and seq % L == 0 and p == SSM_STATE
    assert LANES % nh == 0 and s_new <= LANES // nh and g % GROUPS_PER_TILE == 0
    rep = LANES // nh
    no = g // GROUPS_PER_TILE
    sd = 2 * GROUPS_PER_TILE * p
    np_rows = n_meta + seq
    ns_rows = nb * s_new
    t = tiles or _tiles(np_rows + ns_rows)
    r = t["r"]
    npc = np_rows // L
    n_chunks = r // L

    x = jnp.concatenate([meta_tokens.astype(F32), x_prompt[0], x_sample.reshape(ns_rows, d),
                         jnp.zeros((r - np_rows - ns_rows, d), F32)], axis=0)
    cf = jnp.repeat(cache_logf, rep, axis=-1)

    row2 = lambda a: a.reshape(1, -1)
    ffn1_w = [w.astype(BF16) for w in (ffn1_w_gate, ffn1_w_up, ffn1_w_down)]
    ffn2_w = [w.astype(BF16) for w in (ffn2_w_gate, ffn2_w_up, ffn2_w_down)]
    w_out_b = w_out.astype(BF16)
    outs = dict(k=[], v=[], logf=[], hend=[])
    for l in range(depth):
        x = _ffn(x, row2(ffn1_norm_pre[l]), row2(ffn1_norm_post[l]), *ffn1_w, l, tm=t["tm"], tf=t["tf"])

        wl = w_in[l]
        w4 = jnp.stack([wl[:, :aw], wl[:, aw:2 * aw], wl[:, 2 * aw:3 * aw], wl[:, 3 * aw + nh:]]).astype(BF16)
        wf = jnp.repeat(wl[:, 3 * aw:3 * aw + nh], rep, axis=1).astype(BF16)
        bf = jnp.repeat(b_forget[l], rep).reshape(1, LANES)
        qf, kf, vf, uf, qa, ka, vt, logf = _in_proj(x, row2(mix_norm_pre[l]), w4, wf, bf, tm=t["tm"], rep=rep)

        attn = _attn_prompt(qa, ka, vt, tq=t["tq"])

        qs = (qf[np_rows:np_rows + ns_rows] * ATTN_SCALE).astype(BF16).reshape(nb, s_new, nh, HEAD_DIM)
        qs = jnp.pad(qs.transpose(0, 2, 3, 1), ((0, 0), (0, 0), (0, 0), (0, rep - s_new)))
        qbd = (qs[:, :, :, None, :] * jnp.eye(nh, dtype=BF16)[None, :, None, :, None]).reshape(nb, aw, LANES)
        attn = _attn_sample(l, qbd, cache_k, cache_v, cf, kf, vf, logf, attn,
                            np_rows=np_rows, tk=t["tk_cache"], rep=rep)

        kds2, wc, vc, a16, wglu2 = _s5_operators(ssm_a_re[l], ssm_a_im[l], ssm_log_dt[l], ssm_b_re[l],
                                                 ssm_b_im[l], ssm_c_re[l], ssm_c_im[l], ssm_w_glu[l])
        h0 = jnp.concatenate([state_ssm_re[l].reshape(nb, no, 1, sd // 2),
                              state_ssm_im[l].reshape(nb, no, 1, sd // 2)], axis=2).reshape(nb, no * sd)
        hinit = jnp.zeros((n_chunks, no * sd), F32).at[npc:npc + nb].set(h0)
        ssm, hend = _s5(uf, kds2, wc, vc, a16, hinit, row2(ssm_d[l]), wglu2,
                        rows_tile=t["s5_rows"], n_prompt_chunks=npc)

        x = _out_proj(x, attn, ssm, row2(attn_out_norm[l]), row2(ssm_out_norm[l]), row2(mix_norm_post[l]),
                      w_out_b, l, tm=t["tm"])

        x = _ffn(x, row2(ffn2_norm_pre[l]), row2(ffn2_norm_post[l]), *ffn2_w, l, tm=t["tm"], tf=t["tf"])

        outs["k"].append(kf.reshape(r, nh, HEAD_DIM))
        outs["v"].append(vf.reshape(r, nh, HEAD_DIM))
        outs["logf"].append(logf[:, ::rep])
        outs["hend"].append(hend[npc - 1:npc + nb].reshape(1 + nb, no, 2, GROUPS_PER_TILE, p))

    sl_p = slice(0, np_rows)
    sl_s = slice(np_rows, np_rows + ns_rows)
    rows = lambda arrs, sl: jnp.stack([a[sl] for a in arrs])
    heads = lambda a, n: a.reshape(depth, -1, n, nh, HEAD_DIM)
    states = lambda a: a.reshape(depth, -1, g, p)
    h_all = jnp.stack(outs["hend"])
    return (x[n_meta:np_rows][None], x[sl_s].reshape(nb, s_new, d),
            heads(rows(outs["k"], sl_p), np_rows), heads(rows(outs["v"], sl_p), np_rows),
            rows(outs["logf"], sl_p).reshape(depth, 1, np_rows, nh),
            states(h_all[:, :1, :, 0]), states(h_all[:, :1, :, 1]),
            heads(rows(outs["k"], sl_s), s_new), heads(rows(outs["v"], sl_s), s_new),
            rows(outs["logf"], sl_s).reshape(depth, nb, s_new, nh),
            states(h_all[:, 1:, :, 0]), states(h_all[:, 1:, :, 1]))


def kernel(x_prompt, x_sample, cache_k, cache_v, cache_logf, state_ssm_re, state_ssm_im, meta_tokens, ffn1_norm_pre, ffn1_norm_post, ffn1_w_gate, ffn1_w_up, ffn1_w_down, mix_norm_pre, mix_norm_post, w_in, b_forget, ssm_a_re, ssm_a_im, ssm_log_dt, ssm_b_re, ssm_b_im, ssm_c_re, ssm_c_im, ssm_d, ssm_w_glu, attn_out_norm, ssm_out_norm, w_out, ffn2_norm_pre, ffn2_norm_post, ffn2_w_gate, ffn2_w_up, ffn2_w_down):
    return _forward(x_prompt, x_sample, cache_k, cache_v, cache_logf, state_ssm_re, state_ssm_im, meta_tokens,
                    ffn1_norm_pre, ffn1_norm_post, ffn1_w_gate, ffn1_w_up, ffn1_w_down,
                    mix_norm_pre, mix_norm_post, w_in, b_forget,
                    ssm_a_re, ssm_a_im, ssm_log_dt, ssm_b_re, ssm_b_im, ssm_c_re, ssm_c_im, ssm_d, ssm_w_glu,
                    attn_out_norm, ssm_out_norm, w_out,
                    ffn2_norm_pre, ffn2_norm_post, ffn2_w_gate, ffn2_w_up, ffn2_w_down)
```

```python
import functools
import math

import jax
import jax.numpy as jnp
from jax import lax
from jax.experimental import pallas as pl
from jax.experimental.pallas import tpu as pltpu

F32 = jnp.float32
BF16 = jnp.bfloat16

HEAD_DIM = 128
SSM_CH = 16
SSM_STATE = 64
S5_CHUNK = 16
LANES = 128
GROUPS_PER_TILE = LANES // SSM_CH
MACARON_W = 0.5
NORM_EPS = 1e-6
ATTN_SCALE = HEAD_DIM ** -0.5
LOG2E = math.log2(math.e)
NEG_INF = -1e30
VMEM_LIMIT = 56 * 1024 * 1024
N_BIAS_TERMS = 3
SCAN_UNROLL = 4


def _cparams(*sem):
    return pltpu.CompilerParams(dimension_semantics=sem, vmem_limit_bytes=VMEM_LIMIT)


def _rms(x, g):
    return x * lax.rsqrt(jnp.mean(x * x, axis=-1, keepdims=True) + NORM_EPS) * g


def _dot(a, b):
    return jnp.dot(a, b, preferred_element_type=F32)


def _dot_nt(a, b):
    return lax.dot_general(a, b, (((1,), (1,)), ((), ())), preferred_element_type=F32)


def _split3(x):
    hi = x.astype(BF16)
    r1 = x - hi.astype(F32)
    mid = r1.astype(BF16)
    lo = (r1 - mid.astype(F32)).astype(BF16)
    return hi, mid, lo


def _dot_exact_lhs01(sel, x):
    hi, mid, lo = _split3(x)
    return _dot(sel, hi) + _dot(sel, mid) + _dot(sel, lo)


def _iota2(shape, axis):
    return lax.broadcasted_iota(jnp.int32, shape, axis)


def _ffn_kernel(x_ref, gpre_ref, gpost_ref, wg_ref, wu_ref, wd_ref, o_ref, xn_ref, acc_ref):
    j = pl.program_id(1)

    @pl.when(j == 0)
    def _():
        xn_ref[...] = _rms(x_ref[...], gpre_ref[...]).astype(BF16)
        acc_ref[...] = jnp.zeros_like(acc_ref)

    xn = xn_ref[...]
    g = _dot(xn, wg_ref[...])
    u = _dot(xn, wu_ref[...])
    h = (g * jax.nn.sigmoid(g) * u).astype(BF16)
    acc_ref[...] += _dot(h, wd_ref[...])

    @pl.when(j == pl.num_programs(1) - 1)
    def _():
        o_ref[...] = x_ref[...] + MACARON_W * _rms(acc_ref[...], gpost_ref[...])


def _ffn(x, g_pre, g_post, wg, wu, wd, layer, *, tm, tf):
    r, d = x.shape
    f = wg.shape[2]
    return pl.pallas_call(
        _ffn_kernel,
        grid=(r // tm, f // tf),
        in_specs=[
            pl.BlockSpec((tm, d), lambda i, j: (i, 0)),
            pl.BlockSpec((1, d), lambda i, j: (0, 0)),
            pl.BlockSpec((1, d), lambda i, j: (0, 0)),
            pl.BlockSpec((None, d, tf), lambda i, j: (layer, 0, j)),
            pl.BlockSpec((None, d, tf), lambda i, j: (layer, 0, j)),
            pl.BlockSpec((None, tf, d), lambda i, j: (layer, j, 0)),
        ],
        out_specs=pl.BlockSpec((tm, d), lambda i, j: (i, 0)),
        out_shape=jax.ShapeDtypeStruct((r, d), F32),
        scratch_shapes=[pltpu.VMEM((tm, d), BF16), pltpu.VMEM((tm, d), F32)],
        compiler_params=_cparams("parallel", "arbitrary"),
        name="ffn",
    )(x, g_pre, g_post, wg, wu, wd)


def _in_kernel(x_ref, g_ref, w_ref, wf_ref, bf_ref, *refs, rep, n_alias, sample_tile, sample_off, n_sample):
    (qf_ref, kp_ref, vp_ref, kt_ref, vt_tail_ref, ks_ref, vs_ref, uf_ref, qa_ref, ka_ref, vt_ref, logf_ref,
     xn_ref, carry_ref, qaug_ref, kaug_ref) = refs[n_alias:]
    i = pl.program_id(0)
    j = pl.program_id(1)
    tm = x_ref.shape[0]
    aw = w_ref.shape[2]
    nh = aw // HEAD_DIM
    nb = N_BIAS_TERMS

    @pl.when(j == 0)
    def _():
        xn = _rms(x_ref[...], g_ref[...]).astype(BF16)
        xn_ref[...] = xn
        logf = jax.nn.log_sigmoid(_dot(xn, wf_ref[...]) + bf_ref[...])
        logf_ref[...] = logf

        @pl.when(i == 0)
        def _():
            carry_ref[...] = jnp.zeros_like(carry_ref)

        tril = (_iota2((tm, tm), 1) <= _iota2((tm, tm), 0)).astype(BF16)
        fcum = _dot_exact_lhs01(tril, logf) + carry_ref[...]
        carry_ref[...] = fcum[tm - 1:tm, :]

        terms = jnp.concatenate(_split3(fcum * LOG2E), axis=1)
        src = _iota2((nb * LANES, aw), 0)
        dst = _iota2((nb * LANES, aw), 1)
        head_src = src % LANES == (dst // HEAD_DIM) * rep
        lane = dst % HEAD_DIM
        kaug = -_dot(terms, jnp.logical_and(head_src, lane == src // LANES).astype(BF16))
        qaug = _dot(terms, jnp.logical_and(head_src, lane == nb + src // LANES).astype(BF16))
        lane_t = _iota2((tm, aw), 1) % HEAD_DIM
        kaug_ref[...] = (kaug + jnp.logical_and(lane_t >= nb, lane_t < 2 * nb).astype(F32)).astype(BF16)
        qaug_ref[...] = (qaug + (lane_t < nb).astype(F32)).astype(BF16)

    z = _dot(xn_ref[...], w_ref[0])
    for sec, dst_ref in ((0, qf_ref), (3, uf_ref)):
        @pl.when(j == sec)
        def _(dst_ref=dst_ref):
            dst_ref[...] = z

    for sec, p_ref, t_ref, s_ref in ((1, kp_ref, kt_ref, ks_ref), (2, vp_ref, vt_tail_ref, vs_ref)):
        @pl.when(j == sec)
        def _(p_ref=p_ref, t_ref=t_ref, s_ref=s_ref):
            @pl.when(i < sample_tile)
            def _():
                for h in range(nh):
                    p_ref[pl.ds(h, tm, stride=nh), :] = z[:, h * HEAD_DIM:(h + 1) * HEAD_DIM]

            @pl.when(i == sample_tile)
            def _():
                for h in range(nh):
                    cols = slice(h * HEAD_DIM, (h + 1) * HEAD_DIM)
                    t_ref[pl.ds(h, sample_off, stride=nh), :] = z[:sample_off, cols]
                    s_ref[pl.ds(h, n_sample, stride=nh), :] = z[sample_off:sample_off + n_sample, cols]

    def interleave(dst_ref, zb, aug_ref):
        for h in range(nh):
            dst_ref[:, 2 * h * HEAD_DIM:(2 * h + 1) * HEAD_DIM] = zb[:, h * HEAD_DIM:(h + 1) * HEAD_DIM]
            dst_ref[:, (2 * h + 1) * HEAD_DIM:(2 * h + 2) * HEAD_DIM] = aug_ref[:, h * HEAD_DIM:(h + 1) * HEAD_DIM]

    @pl.when(j == 0)
    def _():
        interleave(qa_ref, (z * (ATTN_SCALE * LOG2E)).astype(BF16), qaug_ref)

    @pl.when(j == 1)
    def _():
        interleave(ka_ref, z.astype(BF16), kaug_ref)

    @pl.when(j == 2)
    def _():
        for h in range(nh):
            vt_ref[h] = jnp.transpose(z[:, h * HEAD_DIM:(h + 1) * HEAD_DIM]).astype(BF16)


def _in_proj(x, g, w4, wf, bf, kv_prev, *, layer, depth, np_rows, ns_rows, tm, rep):
    r, d = x.shape
    nsec, _, aw = w4.shape
    nh = aw // HEAD_DIM
    sample_tile, sample_off = divmod(np_rows, tm)
    assert sample_off + ns_rows <= tm
    assert sample_off > 0 and sample_tile == r // tm - 1
    n_alias = 0 if kv_prev is None else len(kv_prev)
    kernel_fn = functools.partial(_in_kernel, rep=rep, n_alias=n_alias, sample_tile=sample_tile,
                                  sample_off=sample_off, n_sample=ns_rows)
    prompt_spec = pl.BlockSpec((None, tm * nh, HEAD_DIM), lambda i, j: (layer, jnp.minimum(i, sample_tile - 1), 0))
    tail_spec = pl.BlockSpec((None, sample_off * nh, HEAD_DIM), lambda i, j: (layer, 0, 0))
    sample_spec = pl.BlockSpec((None, ns_rows * nh, HEAD_DIM), lambda i, j: (layer, 0, 0))
    prompt_shape = jax.ShapeDtypeStruct((depth, np_rows * nh, HEAD_DIM), F32)
    tail_shape = jax.ShapeDtypeStruct((depth, sample_off * nh, HEAD_DIM), F32)
    sample_shape = jax.ShapeDtypeStruct((depth, ns_rows * nh, HEAD_DIM), F32)
    return pl.pallas_call(
        kernel_fn,
        grid=(r // tm, nsec),
        in_specs=[
            pl.BlockSpec((tm, d), lambda i, j: (i, 0)),
            pl.BlockSpec((1, d), lambda i, j: (0, 0)),
            pl.BlockSpec((1, d, aw), lambda i, j: (j, 0, 0)),
            pl.BlockSpec((d, LANES), lambda i, j: (0, 0)),
            pl.BlockSpec((1, LANES), lambda i, j: (0, 0)),
        ] + [pl.BlockSpec(memory_space=pl.ANY)] * n_alias,
        out_specs=[
            pl.BlockSpec((tm, aw), lambda i, j: (i, 0)),
            prompt_spec, prompt_spec, tail_spec, tail_spec, sample_spec, sample_spec,
            pl.BlockSpec((tm, aw), lambda i, j: (i, 0)),
            pl.BlockSpec((tm, 2 * aw), lambda i, j: (i, 0)),
            pl.BlockSpec((tm, 2 * aw), lambda i, j: (i, 0)),
            pl.BlockSpec((nh, None, HEAD_DIM, tm), lambda i, j: (0, i, 0, 0)),
            pl.BlockSpec((tm, LANES), lambda i, j: (i, 0)),
        ],
        out_shape=[
            jax.ShapeDtypeStruct((r, aw), F32),
            prompt_shape, prompt_shape, tail_shape, tail_shape, sample_shape, sample_shape,
            jax.ShapeDtypeStruct((r, aw), F32),
            jax.ShapeDtypeStruct((r, 2 * aw), BF16),
            jax.ShapeDtypeStruct((r, 2 * aw), BF16),
            jax.ShapeDtypeStruct((nh, r // tm, HEAD_DIM, tm), BF16),
            jax.ShapeDtypeStruct((r, LANES), F32),
        ],
        scratch_shapes=[pltpu.VMEM((tm, d), BF16), pltpu.VMEM((1, LANES), F32),
                        pltpu.VMEM((tm, aw), BF16), pltpu.VMEM((tm, aw), BF16)],
        input_output_aliases={5 + n: 1 + n for n in range(n_alias)},
        compiler_params=_cparams("arbitrary", "arbitrary"),
        name="in_proj",
    )(x, g, w4, wf, bf, *(kv_prev or ()))


def _attn_p_kernel(qa_ref, ka_ref, vt_ref, o_ref, sa_ref, sb_ref, acc_ref):
    qi = pl.program_id(1)
    tq = qa_ref.shape[0]
    tk = sa_ref.shape[0]
    n_diag = tq // tk
    n_off = qi * n_diag

    def produce(s_ref, kb):
        s = _dot_nt(ka_ref[pl.ds(pl.multiple_of(kb * tk, tk), tk), :], qa_ref[...])
        s_ref[...] = s
        return jnp.max(s, axis=0, keepdims=True)

    def consume(s_ref, m_blk, kb, carry):
        m, l = carry
        m_new = jnp.maximum(m, m_blk)
        p = jnp.exp2(s_ref[...] - m_new)
        alpha = jnp.exp2(m - m_new)
        acc_ref[...] = alpha * acc_ref[...] + _dot(vt_ref[kb], p.astype(BF16))
        return m_new, alpha * l + jnp.sum(p, axis=0, keepdims=True)

    acc_ref[...] = jnp.zeros_like(acc_ref)
    m_first = produce(sa_ref, 0)

    def pair(kb, carry):
        m_a, m, l = carry
        m_b = produce(sb_ref, kb + 1)
        m, l = consume(sa_ref, m_a, kb, (m, l))
        m_a = produce(sa_ref, kb + 2)
        m, l = consume(sb_ref, m_b, kb + 1, (m, l))
        return m_a, m, l

    init = (m_first, jnp.full((1, tq), NEG_INF, F32), jnp.zeros((1, tq), F32))
    carry = lax.fori_loop(0, n_off // 4, lambda i, c: pair(4 * i + 2, pair(4 * i, c)), init)
    done = (n_off // 4) * 4
    m_a, m, l = lax.fori_loop(0, (n_off % 4) // 2, lambda i, c: pair(done + 2 * i, c), carry)
    def produce_diag(s_ref, j):
        c0 = j * tk
        kb = n_off + j
        s_ref[:, :tq - c0] = _dot_nt(ka_ref[pl.ds(pl.multiple_of(kb * tk, tk), tk), :], qa_ref[c0:, :])

    def consume_diag(s_ref, j, carry):
        m, l = carry
        c0 = j * tk
        w = tq - c0
        visible = _iota2((tk, w), 1) >= _iota2((tk, w), 0)
        s = jnp.where(visible, s_ref[:, :w], NEG_INF)
        m_old = m[:, c0:]
        m_new = jnp.maximum(m_old, jnp.max(s, axis=0, keepdims=True))
        p = jnp.exp2(s - m_new)
        alpha = jnp.exp2(m_old - m_new)
        acc_ref[:, c0:] = alpha * acc_ref[:, c0:] + _dot(vt_ref[n_off + j], p.astype(BF16))
        l_new = alpha * l[:, c0:] + jnp.sum(p, axis=0, keepdims=True)
        if c0 == 0:
            return m_new, l_new
        return (jnp.concatenate([m[:, :c0], m_new], axis=1), jnp.concatenate([l[:, :c0], l_new], axis=1))

    def tail(lead_block, m_lead, carry):
        bufs = (sa_ref, sb_ref)
        has_lead = lead_block is not None
        for idx in range(n_diag + has_lead):
            cur = bufs[idx % 2]
            j = idx - has_lead
            if j + 1 < n_diag:
                produce_diag(bufs[(idx + 1) % 2], j + 1)
            if j < 0:
                carry = consume(cur, m_lead, lead_block, carry)
            else:
                carry = consume_diag(cur, j, carry)
        o_ref[...] = jnp.transpose(acc_ref[...] / carry[1])

    @pl.when(n_off % 2 == 0)
    def _():
        tail(None, None, (m, l))

    @pl.when(n_off % 2 == 1)
    def _():
        tail(n_off - 1, m_a, (m, l))


def _attn_prompt(qa, ka, vt, *, tq):
    nh, nkb, _, tk = vt.shape
    r = nkb * tk
    return pl.pallas_call(
        _attn_p_kernel,
        grid=(nh, r // tq),
        in_specs=[
            pl.BlockSpec((tq, 2 * HEAD_DIM), lambda h, i: (i, h)),
            pl.BlockSpec((r, 2 * HEAD_DIM), lambda h, i: (0, h)),
            pl.BlockSpec((None, nkb, HEAD_DIM, tk), lambda h, i: (h, 0, 0, 0)),
        ],
        out_specs=pl.BlockSpec((tq, HEAD_DIM), lambda h, i: (i, h)),
        out_shape=jax.ShapeDtypeStruct((r, nh * HEAD_DIM), F32),
        scratch_shapes=[pltpu.VMEM((tk, tq), F32), pltpu.VMEM((tk, tq), F32), pltpu.VMEM((HEAD_DIM, tq), F32)],
        compiler_params=_cparams("parallel", "arbitrary"),
        name="attn_p",
    )(qa, ka, vt)


def _attn_s_kernel(qbd_ref, ck_ref, cv_ref, cf_ref, kn_ref, vn_ref, fn_ref, attn_in_ref, o_ref,
                   m_ref, l_ref, acc_ref, suf_ref, cq_ref, *, rep, seg, nh):
    del attn_in_ref
    step = pl.program_id(1)
    s_new = fn_ref.shape[0]
    tk = cf_ref.shape[0]
    qbd = qbd_ref[...]

    def heads_side_by_side(ref, n):
        return jnp.concatenate([ref[pl.ds(h, n, stride=nh), :].astype(BF16) for h in range(nh)], axis=1)

    def update(t, v_bf):
        m_old = m_ref[...]
        m_new = jnp.maximum(m_old, jnp.max(t, axis=0, keepdims=True))
        p = jnp.exp(t - m_new)
        alpha = jnp.exp(m_old - m_new)
        l_ref[...] = alpha * l_ref[...] + jnp.sum(p, axis=0, keepdims=True)
        m_ref[...] = m_new
        alpha_col = jnp.transpose(jnp.broadcast_to(alpha, (LANES, LANES)))
        pv = _dot(jnp.transpose(p).astype(BF16), v_bf)
        for c in range(nh):
            sl = slice(c * HEAD_DIM, (c + 1) * HEAD_DIM)
            acc_ref[:, sl] = acc_ref[:, sl] * alpha_col + pv[:, sl]

    @pl.when(step == 0)
    def _():
        m_ref[...] = jnp.full_like(m_ref, NEG_INF)
        l_ref[...] = jnp.zeros_like(l_ref)
        acc_ref[...] = jnp.zeros_like(acc_ref)
        tril = (_iota2((s_new, s_new), 1) <= _iota2((s_new, s_new), 0)).astype(BF16)
        cn = _dot_exact_lhs01(tril, fn_ref[...])
        krow = _iota2((s_new, LANES), 0)
        qlane = _iota2((s_new, LANES), 1) % rep
        cq = jnp.sum(jnp.where(krow == qlane, cn, 0.0), axis=0, keepdims=True)
        cq_ref[...] = cq
        suf_ref[...] = jnp.zeros_like(suf_ref)
        st = _dot(heads_side_by_side(kn_ref, s_new), qbd)
        t = jnp.where(krow <= qlane, st + (cq - cn), NEG_INF)
        update(t, heads_side_by_side(vn_ref, s_new))

    triu = (_iota2((seg, seg), 1) > _iota2((seg, seg), 0)).astype(BF16)
    carry = suf_ref[...]
    sufs = [None] * (tk // seg)
    for sidx in reversed(range(tk // seg)):
        x = cf_ref[sidx * seg:(sidx + 1) * seg, :]
        sfx = _dot_exact_lhs01(triu, x) + carry
        sufs[sidx] = sfx
        carry = sfx[0:1, :] + x[0:1, :]
    suf_ref[...] = carry
    bias = jnp.concatenate(sufs, axis=0) + cq_ref[...]
    update(_dot(heads_side_by_side(ck_ref, tk), qbd) + bias, heads_side_by_side(cv_ref, tk))

    @pl.when(step == pl.num_programs(1) - 1)
    def _():
        l_col = jnp.transpose(jnp.broadcast_to(l_ref[...], (LANES, LANES)))
        for c in range(nh):
            rws = slice(c * rep, c * rep + s_new)
            sl = slice(c * HEAD_DIM, (c + 1) * HEAD_DIM)
            o_ref[:, sl] = acc_ref[rws, sl] / l_col[rws, :]


def _attn_sample(layer, qbd, cache_k, cache_v, cf, kf, vf, logf, attn, *, np_rows, tk, rep):
    depth, nb, past, nh, _ = cache_k.shape
    aw = nh * HEAD_DIM
    s_new = S5_CHUNK
    nkb = past // tk
    base = np_rows // s_new
    seg = min(tk, LANES)

    ck = cache_k.reshape(depth, nb, past * nh, HEAD_DIM)
    cv = cache_v.reshape(depth, nb, past * nh, HEAD_DIM)
    cache_spec = pl.BlockSpec((None, None, tk * nh, HEAD_DIM), lambda b, s: (layer, b, nkb - 1 - s, 0))
    return pl.pallas_call(
        functools.partial(_attn_s_kernel, rep=rep, seg=seg, nh=nh),
        grid=(nb, nkb),
        in_specs=[
            pl.BlockSpec((None, aw, LANES), lambda b, s: (b, 0, 0)),
            cache_spec,
            cache_spec,
            pl.BlockSpec((None, None, tk, LANES), lambda b, s: (layer, b, nkb - 1 - s, 0)),
            pl.BlockSpec((None, s_new * nh, HEAD_DIM), lambda b, s: (layer, b, 0)),
            pl.BlockSpec((None, s_new * nh, HEAD_DIM), lambda b, s: (layer, b, 0)),
            pl.BlockSpec((s_new, LANES), lambda b, s: (base + b, 0)),
            pl.BlockSpec(memory_space=pl.ANY),
        ],
        out_specs=pl.BlockSpec((s_new, aw), lambda b, s: (base + b, 0)),
        out_shape=jax.ShapeDtypeStruct(attn.shape, F32),
        scratch_shapes=[
            pltpu.VMEM((1, LANES), F32), pltpu.VMEM((1, LANES), F32), pltpu.VMEM((LANES, aw), F32),
            pltpu.VMEM((1, LANES), F32), pltpu.VMEM((1, LANES), F32),
        ],
        input_output_aliases={7: 0},
        compiler_params=_cparams("parallel", "arbitrary"),
        name="attn_s",
    )(qbd, ck, cv, cf, kf, vf, logf, attn)


def _s5_kernel(u_ref, kc_ref, wc_ref, vc_ref, a_ref, hinit_ref, d_ref, wglu2_ref, o_ref, hend_ref,
               kds2_ref, wfull_ref, vfull_ref, s_ref, hprev_ref, hc_ref, *, n_prompt_chunks):
    rt = pl.program_id(1)
    ct, sd = s_ref.shape
    half = sd // 2
    L = S5_CHUNK
    P = SSM_STATE

    @pl.when(rt == 0)
    def _():
        hc_ref[...] = jnp.zeros_like(hc_ref)
        r_, c_ = _iota2((2 * P, sd), 0), _iota2((2 * P, sd), 1)
        rep_w = jnp.logical_and(r_ // P == c_ // half, r_ % P == c_ % P).astype(BF16)
        r_, c_ = _iota2((L * LANES, sd), 0), _iota2((L * LANES, sd), 1)
        same = (r_ // SSM_CH) % GROUPS_PER_TILE == (c_ % half) // P
        wfull_ref[...] = jnp.where(same, _dot(wc_ref[...], rep_w), 0.0).astype(BF16)
        r_, c_ = _iota2((L * SSM_CH, L * LANES), 0), _iota2((L * SSM_CH, L * LANES), 1)
        rep_v = jnp.logical_and(r_ // SSM_CH == c_ // LANES, r_ % SSM_CH == c_ % SSM_CH).astype(BF16)
        r_, c_ = _iota2((sd, L * LANES), 0), _iota2((sd, L * LANES), 1)
        same = (r_ % half) // P == (c_ % LANES) // SSM_CH
        vfull_ref[...] = jnp.where(same, _dot(vc_ref[...], rep_v), 0.0).astype(BF16)
        r_, c_ = _iota2((SSM_CH, LANES), 0), _iota2((SSM_CH, LANES), 1)
        rep_k = (r_ == c_ % SSM_CH).astype(BF16)
        r_, c_ = _iota2((L * LANES, LANES), 0), _iota2((L * LANES, LANES), 1)
        same = (r_ // SSM_CH) % GROUPS_PER_TILE == c_ // SSM_CH
        kds = jnp.where(same, _dot(kc_ref[...], rep_k), 0.0).astype(BF16)
        kds2_ref[:, :LANES] = kds
        kds2_ref[:LANES, LANES:] = jnp.zeros((LANES, LANES), BF16)
        kds2_ref[LANES:, LANES:] = kds[:(L - 1) * LANES, :]

    def u_at(tau):
        return u_ref[pl.ds(tau, ct, stride=L), :]

    xr = jnp.concatenate([u_at(L - 1 - j).astype(BF16) for j in range(L)], axis=1)
    s_ref[...] = _dot(xr, wfull_ref[...])

    ar = a_ref[:, :half]
    ai = a_ref[:, half:]

    def scan(c, hcar):
        cg = rt * ct + c
        reset = jnp.logical_or(cg == 0, cg >= n_prompt_chunks)
        hp = jnp.where(reset, hinit_ref[pl.ds(c, 1), :], hcar)
        hprev_ref[pl.ds(c, 1), :] = hp
        s = s_ref[pl.ds(c, 1), :]
        hr = hp[:, :half]
        hi = hp[:, half:]
        hn = jnp.concatenate([ar * hr - ai * hi + s[:, :half], ar * hi + ai * hr + s[:, half:]], axis=1)
        hend_ref[pl.ds(c, 1), :] = hn
        return hn

    hc_ref[...] = lax.fori_loop(0, ct, scan, hc_ref[...], unroll=SCAN_UNROLL)

    ystate = _dot(hprev_ref[...].astype(BF16), vfull_ref[...])
    d2 = jnp.concatenate([d_ref[...], d_ref[...]], axis=1)
    wglu2 = wglu2_ref[...]
    for tau in range(0, L, 2):
        lag_rows = (tau + 2) * LANES
        yy = _dot(xr[:, (L - 2 - tau) * LANES:], kds2_ref[:lag_rows, :])
        yy = yy + jnp.concatenate([ystate[:, (tau + 1) * LANES:(tau + 2) * LANES],
                                   ystate[:, tau * LANES:(tau + 1) * LANES]], axis=1)
        yy = yy + d2 * jnp.concatenate([u_at(tau + 1), u_at(tau)], axis=1)
        yy = 0.5 * yy * (1.0 + jnp.tanh(math.sqrt(2.0 / math.pi) * (yy + 0.044715 * (yy * yy * yy))))
        out = yy * jax.nn.sigmoid(_dot(yy.astype(BF16), wglu2))
        o_ref[pl.ds(tau + 1, ct, stride=L), :] = out[:, :LANES]
        o_ref[pl.ds(tau, ct, stride=L), :] = out[:, LANES:]


def _s5(u, kc, wc, vc, a16, hinit, d, wglu2, *, rows_tile, n_prompt_chunks):
    r, sw = u.shape
    no = sw // LANES
    L = S5_CHUNK
    ct = rows_tile // L
    nrt = r // rows_tile
    sd = a16.shape[2]
    return pl.pallas_call(
        functools.partial(_s5_kernel, n_prompt_chunks=n_prompt_chunks),
        grid=(no, nrt),
        in_specs=[
            pl.BlockSpec((rows_tile, LANES), lambda o, t: (t, o)),
            pl.BlockSpec((None, L * LANES, SSM_CH), lambda o, t: (o, 0, 0)),
            pl.BlockSpec((None, L * LANES, 2 * SSM_STATE), lambda o, t: (o, 0, 0)),
            pl.BlockSpec((None, sd, L * SSM_CH), lambda o, t: (o, 0, 0)),
            pl.BlockSpec((None, 1, sd), lambda o, t: (o, 0, 0)),
            pl.BlockSpec((ct, sd), lambda o, t: (t, o)),
            pl.BlockSpec((1, LANES), lambda o, t: (0, o)),
            pl.BlockSpec((None, 2 * LANES, 2 * LANES), lambda o, t: (o, 0, 0)),
        ],
        out_specs=[
            pl.BlockSpec((rows_tile, LANES), lambda o, t: (t, o)),
            pl.BlockSpec((ct, sd), lambda o, t: (t, o)),
        ],
        out_shape=[
            jax.ShapeDtypeStruct((r, sw), F32),
            jax.ShapeDtypeStruct((r // L, no * sd), F32),
        ],
        scratch_shapes=[pltpu.VMEM((L * LANES, 2 * LANES), BF16),
                        pltpu.VMEM((L * LANES, sd), BF16), pltpu.VMEM((sd, L * LANES), BF16),
                        pltpu.VMEM((ct, sd), F32), pltpu.VMEM((ct, sd), F32), pltpu.VMEM((1, sd), F32)],
        compiler_params=_cparams("parallel", "arbitrary"),
        name="s5",
    )(u, kc, wc, vc, a16, hinit, d, wglu2)


def _s5_operators(a_re, a_im, log_dt, b_re, b_im, c_re, c_im, w_glu):
    g, p = a_re.shape
    c = SSM_CH
    L = S5_CHUNK
    gt = GROUPS_PER_TILE
    no = g // gt
    hp = lax.Precision.HIGHEST
    cmul = lambda xr, xi, yr, yi: (xr * yr - xi * yi, xr * yi + xi * yr)
    dt = jnp.exp(log_dt)[:, None]
    mag = jnp.exp(a_re * dt)
    ar, ai = mag * jnp.cos(a_im * dt), mag * jnp.sin(a_im * dt)
    den = a_re * a_re + a_im * a_im
    fr = ((ar - 1.0) * a_re + ai * a_im) / den
    fi = (ai * a_re - (ar - 1.0) * a_im) / den
    bbr, bbi = cmul(fr[..., None], fi[..., None], b_re, b_im)
    pr, pi = [jnp.ones_like(ar)], [jnp.zeros_like(ar)]
    for _ in range(L):
        nr, ni = cmul(pr[-1], pi[-1], ar, ai)
        pr.append(nr)
        pi.append(ni)
    pr, pi = jnp.stack(pr), jnp.stack(pi)

    abr, abi = cmul(pr[:L, :, :, None], pi[:L, :, :, None], bbr[None], bbi[None])
    kd = (jnp.einsum('gop,dgpi->gdio', c_re, abr, precision=hp)
          - jnp.einsum('gop,dgpi->gdio', c_im, abi, precision=hp))
    eye = jnp.eye(gt, dtype=F32)
    kc = kd.reshape(no, gt, L, c, c).transpose(0, 2, 1, 3, 4).reshape(no, L * LANES, c)

    wc = jnp.stack([abr, abi], axis=0).reshape(2, L, no, gt, p, c)
    wc = wc.transpose(2, 1, 3, 5, 0, 4).reshape(no, L * LANES, 2 * p)

    zr, zi = cmul(c_re.transpose(0, 2, 1)[None], c_im.transpose(0, 2, 1)[None],
                  pr[1:, :, :, None], pi[1:, :, :, None])
    vc = jnp.stack([zr, -zi], axis=0).reshape(2, L, no, gt, p, c)
    vc = vc.transpose(2, 0, 3, 4, 1, 5).reshape(no, 2 * gt * p, L * c)

    a16 = jnp.concatenate([pr[L].reshape(no, 1, gt * p), pi[L].reshape(no, 1, gt * p)], axis=2)
    wg = w_glu.reshape(no, gt, c, c)
    wglu = (wg[:, :, :, None, :] * eye[None, :, None, :, None]).reshape(no, LANES, LANES)
    zero = jnp.zeros_like(wglu)
    wglu2 = jnp.concatenate([jnp.concatenate([wglu, zero], axis=2), jnp.concatenate([zero, wglu], axis=2)], axis=1)
    return kc.astype(BF16), wc.astype(BF16), vc.astype(BF16), a16, wglu2.astype(BF16)


def _out_kernel(x_ref, attn_ref, ssm_ref, ga_ref, gs_ref, gpost_ref, wa_ref, ws_ref, o_ref):
    an = _rms(attn_ref[...], ga_ref[...]).astype(BF16)
    sn = _rms(ssm_ref[...], gs_ref[...]).astype(BF16)
    mixed = _dot(an, wa_ref[...]) + _dot(sn, ws_ref[...])
    o_ref[...] = x_ref[...] + _rms(mixed, gpost_ref[...])


def _out_proj(x, attn, ssm, ga, gs, gpost, w_out, layer, *, tm):
    r, d = x.shape
    aw = attn.shape[1]
    sw = ssm.shape[1]
    assert aw == sw
    once = dict(pipeline_mode=pl.Buffered(1))
    return pl.pallas_call(
        _out_kernel,
        grid=(r // tm,),
        in_specs=[
            pl.BlockSpec((tm, d), lambda i: (i, 0)),
            pl.BlockSpec((tm, aw), lambda i: (i, 0)),
            pl.BlockSpec((tm, sw), lambda i: (i, 0)),
            pl.BlockSpec((1, aw), lambda i: (0, 0)),
            pl.BlockSpec((1, sw), lambda i: (0, 0)),
            pl.BlockSpec((1, d), lambda i: (0, 0)),
            pl.BlockSpec((None, aw, d), lambda i: (layer, 0, 0), **once),
            pl.BlockSpec((None, sw, d), lambda i: (layer, 1, 0), **once),
        ],
        out_specs=pl.BlockSpec((tm, d), lambda i: (i, 0)),
        out_shape=jax.ShapeDtypeStruct((r, d), F32),
        compiler_params=_cparams("parallel"),
        name="out_proj",
    )(x, attn, ssm, ga, gs, gpost, w_out, w_out)


def _tiles(r_min):
    tm = 512
    r = -(-r_min // tm) * tm
    s5_tiles = 4 if (r // 4) % (8 * S5_CHUNK) == 0 else 1
    tq = 3 * tm if r % (3 * tm) == 0 else tm
    return dict(r=r, tm=tm, tq=tq, tf=512, s5_rows=r // s5_tiles, tk_cache=1024)


def _forward(x_prompt, x_sample, cache_k, cache_v, cache_logf, state_ssm_re, state_ssm_im, meta_tokens,
             ffn1_norm_pre, ffn1_norm_post, ffn1_w_gate, ffn1_w_up, ffn1_w_down,
             mix_norm_pre, mix_norm_post, w_in, b_forget,
             ssm_a_re, ssm_a_im, ssm_log_dt, ssm_b_re, ssm_b_im, ssm_c_re, ssm_c_im, ssm_d, ssm_w_glu,
             attn_out_norm, ssm_out_norm, w_out,
             ffn2_norm_pre, ffn2_norm_post, ffn2_w_gate, ffn2_w_up, ffn2_w_down, tiles=None):
    bsz, seq, d = x_prompt.shape
    nb, s_new, _ = x_sample.shape
    depth, _, past, nh, _ = cache_k.shape
    n_meta = meta_tokens.shape[0]
    aw = nh * HEAD_DIM
    g, p = ssm_a_re.shape[1:]
    L = S5_CHUNK
    assert bsz == 1 and s_new == L and n_meta % L == 0 and seq % L == 0 and p == SSM_STATE
    assert LANES % nh == 0 and s_new <= LANES // nh and g % GROUPS_PER_TILE == 0
    rep = LANES // nh
    no = g // GROUPS_PER_TILE
    sd = 2 * GROUPS_PER_TILE * p
    np_rows = n_meta + seq
    ns_rows = nb * s_new
    t = tiles or _tiles(np_rows + ns_rows)
    r = t["r"]
    npc = np_rows // L
    n_chunks = r // L

    x = jnp.concatenate([meta_tokens.astype(F32), x_prompt[0], x_sample.reshape(ns_rows, d),
                         jnp.zeros((r - np_rows - ns_rows, d), F32)], axis=0)
    cf = jnp.repeat(cache_logf, rep, axis=-1)

    row2 = lambda a: a.reshape(1, -1)
    ffn1_w = [w.astype(BF16) for w in (ffn1_w_gate, ffn1_w_up, ffn1_w_down)]
    ffn2_w = [w.astype(BF16) for w in (ffn2_w_gate, ffn2_w_up, ffn2_w_down)]
    w_out_b = w_out.astype(BF16)
    outs = dict(logf=[], hend=[])
    kv = None
    for l in range(depth):
        x = _ffn(x, row2(ffn1_norm_pre[l]), row2(ffn1_norm_post[l]), *ffn1_w, l, tm=t["tm"], tf=t["tf"])

        wl = w_in[l]
        w4 = jnp.stack([wl[:, :aw], wl[:, aw:2 * aw], wl[:, 2 * aw:3 * aw], wl[:, 3 * aw + nh:]]).astype(BF16)
        wf = jnp.repeat(wl[:, 3 * aw:3 * aw + nh], rep, axis=1).astype(BF16)
        bf = jnp.repeat(b_forget[l], rep).reshape(1, LANES)
        qf, *kv, uf, qa, ka, vt, logf = _in_proj(x, row2(mix_norm_pre[l]), w4, wf, bf, kv, layer=l, depth=depth,
                                                 np_rows=np_rows, ns_rows=ns_rows, tm=t["tm"], rep=rep)

        attn = _attn_prompt(qa, ka, vt, tq=t["tq"])

        qs = (qf[np_rows:np_rows + ns_rows] * ATTN_SCALE).astype(BF16).reshape(nb, s_new, nh, HEAD_DIM)
        qs = jnp.pad(qs.transpose(0, 2, 3, 1), ((0, 0), (0, 0), (0, 0), (0, rep - s_new)))
        qbd = (qs[:, :, :, None, :] * jnp.eye(nh, dtype=BF16)[None, :, None, :, None]).reshape(nb, aw, LANES)
        attn = _attn_sample(l, qbd, cache_k, cache_v, cf, kv[4], kv[5], logf, attn,
                            np_rows=np_rows, tk=t["tk_cache"], rep=rep)

        kc, wc, vc, a16, wglu2 = _s5_operators(ssm_a_re[l], ssm_a_im[l], ssm_log_dt[l], ssm_b_re[l],
                                               ssm_b_im[l], ssm_c_re[l], ssm_c_im[l], ssm_w_glu[l])
        h0 = jnp.concatenate([state_ssm_re[l].reshape(nb, no, 1, sd // 2),
                              state_ssm_im[l].reshape(nb, no, 1, sd // 2)], axis=2).reshape(nb, no * sd)
        hinit = jnp.zeros((n_chunks, no * sd), F32).at[npc:npc + nb].set(h0)
        ssm, hend = _s5(uf, kc, wc, vc, a16, hinit, row2(ssm_d[l]), wglu2,
                        rows_tile=t["s5_rows"], n_prompt_chunks=npc)

        x = _out_proj(x, attn, ssm, row2(attn_out_norm[l]), row2(ssm_out_norm[l]), row2(mix_norm_post[l]),
                      w_out_b, l, tm=t["tm"])

        x = _ffn(x, row2(ffn2_norm_pre[l]), row2(ffn2_norm_post[l]), *ffn2_w, l, tm=t["tm"], tf=t["tf"])

        outs["logf"].append(logf[:, ::rep])
        outs["hend"].append(hend[npc - 1:npc + nb].reshape(1 + nb, no, 2, GROUPS_PER_TILE, p))

    sl_p = slice(0, np_rows)
    sl_s = slice(np_rows, np_rows + ns_rows)
    rows = lambda arrs, sl: jnp.stack([a[sl] for a in arrs])
    heads = lambda a, n: a.reshape(depth, -1, n, nh, HEAD_DIM)
    states = lambda a: a.reshape(depth, -1, g, p)
    h_all = jnp.stack(outs["hend"])
    tail_at = (0, (np_rows // t["tm"]) * t["tm"] * nh, 0)
    k_prompt = lax.dynamic_update_slice(kv[0], kv[2], tail_at)
    v_prompt = lax.dynamic_update_slice(kv[1], kv[3], tail_at)
    return (x[n_meta:np_rows][None], x[sl_s].reshape(nb, s_new, d),
            heads(k_prompt, np_rows), heads(v_prompt, np_rows),
            rows(outs["logf"], sl_p).reshape(depth, 1, np_rows, nh),
            states(h_all[:, :1, :, 0]), states(h_all[:, :1, :, 1]),
            heads(kv[4], s_new), heads(kv[5], s_new),
            rows(outs["logf"], sl_s).reshape(depth, nb, s_new, nh),
            states(h_all[:, 1:, :, 0]), states(h_all[:, 1:, :, 1]))


def kernel(x_prompt, x_sample, cache_k, cache_v, cache_logf, state_ssm_re, state_ssm_im, meta_tokens, ffn1_norm_pre, ffn1_norm_post, ffn1_w_gate, ffn1_w_up, ffn1_w_down, mix_norm_pre, mix_norm_post, w_in, b_forget, ssm_a_re, ssm_a_im, ssm_log_dt, ssm_b_re, ssm_b_im, ssm_c_re, ssm_c_im, ssm_d, ssm_w_glu, attn_out_norm, ssm_out_norm, w_out, ffn2_norm_pre, ffn2_norm_post, ffn2_w_gate, ffn2_w_up, ffn2_w_down):
    return _forward(x_prompt, x_sample, cache_k, cache_v, cache_logf, state_ssm_re, state_ssm_im, meta_tokens,
                    ffn1_norm_pre, ffn1_norm_post, ffn1_w_gate, ffn1_w_up, ffn1_w_down,
                    mix_norm_pre, mix_norm_post, w_in, b_forget,
                    ssm_a_re, ssm_a_im, ssm_log_dt, ssm_b_re, ssm_b_im, ssm_c_re, ssm_c_im, ssm_d, ssm_w_glu,
                    attn_out_norm, ssm_out_norm, w_out,
                    ffn2_norm_pre, ffn2_norm_post, ffn2_w_gate, ffn2_w_up, ffn2_w_down)
```

```python
import functools
import math

import jax
import jax.numpy as jnp
from jax import lax
from jax.experimental import pallas as pl
from jax.experimental.pallas import tpu as pltpu

F32 = jnp.float32
BF16 = jnp.bfloat16

HEAD_DIM = 128
SSM_CH = 16
SSM_STATE = 64
S5_CHUNK = 16
LANES = 128
GROUPS_PER_TILE = LANES // SSM_CH
MACARON_W = 0.5
NORM_EPS = 1e-6
ATTN_SCALE = HEAD_DIM ** -0.5
LOG2E = math.log2(math.e)
NEG_INF = -1e30
VMEM_LIMIT = 56 * 1024 * 1024
N_BIAS_TERMS = 3
SCAN_UNROLL = 4
VT_ROWS = HEAD_DIM + 16


def _cparams(*sem):
    return pltpu.CompilerParams(dimension_semantics=sem, vmem_limit_bytes=VMEM_LIMIT)


def _rms(x, g):
    return x * lax.rsqrt(jnp.mean(x * x, axis=-1, keepdims=True) + NORM_EPS) * g


def _dot(a, b):
    return jnp.dot(a, b, preferred_element_type=F32)


def _dot_nt(a, b):
    return lax.dot_general(a, b, (((1,), (1,)), ((), ())), preferred_element_type=F32)


def _split3(x):
    hi = x.astype(BF16)
    r1 = x - hi.astype(F32)
    mid = r1.astype(BF16)
    lo = (r1 - mid.astype(F32)).astype(BF16)
    return hi, mid, lo


def _dot_exact_lhs01(sel, x):
    hi, mid, lo = _split3(x)
    return _dot(sel, hi) + _dot(sel, mid) + _dot(sel, lo)


def _iota2(shape, axis):
    return lax.broadcasted_iota(jnp.int32, shape, axis)


def _ffn_kernel(x_ref, gpre_ref, gpost_ref, wg_ref, wu_ref, wd_ref, o_ref, xn_ref, acc_ref):
    j = pl.program_id(1)

    @pl.when(j == 0)
    def _():
        xn_ref[...] = _rms(x_ref[...], gpre_ref[...]).astype(BF16)
        acc_ref[...] = jnp.zeros_like(acc_ref)

    xn = xn_ref[...]
    g = _dot(xn, wg_ref[...])
    u = _dot(xn, wu_ref[...])
    h = (g * jax.nn.sigmoid(g) * u).astype(BF16)
    acc_ref[...] += _dot(h, wd_ref[...])

    @pl.when(j == pl.num_programs(1) - 1)
    def _():
        o_ref[...] = x_ref[...] + MACARON_W * _rms(acc_ref[...], gpost_ref[...])


def _ffn(x, g_pre, g_post, wg, wu, wd, layer, *, tm, tf):
    r, d = x.shape
    f = wg.shape[2]
    return pl.pallas_call(
        _ffn_kernel,
        grid=(r // tm, f // tf),
        in_specs=[
            pl.BlockSpec((tm, d), lambda i, j: (i, 0)),
            pl.BlockSpec((1, d), lambda i, j: (0, 0)),
            pl.BlockSpec((1, d), lambda i, j: (0, 0)),
            pl.BlockSpec((None, d, tf), lambda i, j: (layer, 0, j)),
            pl.BlockSpec((None, d, tf), lambda i, j: (layer, 0, j)),
            pl.BlockSpec((None, tf, d), lambda i, j: (layer, j, 0)),
        ],
        out_specs=pl.BlockSpec((tm, d), lambda i, j: (i, 0)),
        out_shape=jax.ShapeDtypeStruct((r, d), F32),
        scratch_shapes=[pltpu.VMEM((tm, d), BF16), pltpu.VMEM((tm, d), F32)],
        compiler_params=_cparams("parallel", "arbitrary"),
        name="ffn",
    )(x, g_pre, g_post, wg, wu, wd)


def _in_kernel(x_ref, g_ref, w_ref, wf_ref, bf_ref, *refs, rep, n_alias, sample_tile, sample_off, n_sample):
    (qf_ref, kp_ref, vp_ref, kt_ref, vt_tail_ref, ks_ref, vs_ref, uf_ref, qa_ref, ka_ref, vt_ref, logf_ref,
     xn_ref, carry_ref, qaug_ref, kaug_ref) = refs[n_alias:]
    i = pl.program_id(0)
    j = pl.program_id(1)
    tm = x_ref.shape[0]
    aw = w_ref.shape[2]
    nh = aw // HEAD_DIM
    nb = N_BIAS_TERMS

    @pl.when(j == 0)
    def _():
        xn = _rms(x_ref[...], g_ref[...]).astype(BF16)
        xn_ref[...] = xn
        logf = jax.nn.log_sigmoid(_dot(xn, wf_ref[...]) + bf_ref[...])
        logf_ref[...] = logf

        @pl.when(i == 0)
        def _():
            carry_ref[...] = jnp.zeros_like(carry_ref)

        tril = (_iota2((tm, tm), 1) <= _iota2((tm, tm), 0)).astype(BF16)
        fcum = _dot_exact_lhs01(tril, logf) + carry_ref[...]
        carry_ref[...] = fcum[tm - 1:tm, :]

        terms = jnp.concatenate(_split3(fcum * LOG2E), axis=1)
        src = _iota2((nb * LANES, aw), 0)
        dst = _iota2((nb * LANES, aw), 1)
        head_src = src % LANES == (dst // HEAD_DIM) * rep
        lane = dst % HEAD_DIM
        kaug = -_dot(terms, jnp.logical_and(head_src, lane == src // LANES).astype(BF16))
        qaug = _dot(terms, jnp.logical_and(head_src, lane == nb + src // LANES).astype(BF16))
        lane_t = _iota2((tm, aw), 1) % HEAD_DIM
        kaug_ref[...] = (kaug + jnp.logical_and(lane_t >= nb, lane_t < 2 * nb).astype(F32)).astype(BF16)
        qaug_ref[...] = (qaug + (lane_t < nb).astype(F32)).astype(BF16)

    z = _dot(xn_ref[...], w_ref[0])
    for sec, dst_ref in ((0, qf_ref), (3, uf_ref)):
        @pl.when(j == sec)
        def _(dst_ref=dst_ref):
            dst_ref[...] = z

    for sec, p_ref, t_ref, s_ref in ((1, kp_ref, kt_ref, ks_ref), (2, vp_ref, vt_tail_ref, vs_ref)):
        @pl.when(j == sec)
        def _(p_ref=p_ref, t_ref=t_ref, s_ref=s_ref):
            @pl.when(i < sample_tile)
            def _():
                for h in range(nh):
                    p_ref[pl.ds(h, tm, stride=nh), :] = z[:, h * HEAD_DIM:(h + 1) * HEAD_DIM]

            @pl.when(i == sample_tile)
            def _():
                for h in range(nh):
                    cols = slice(h * HEAD_DIM, (h + 1) * HEAD_DIM)
                    t_ref[pl.ds(h, sample_off, stride=nh), :] = z[:sample_off, cols]
                    s_ref[pl.ds(h, n_sample, stride=nh), :] = z[sample_off:sample_off + n_sample, cols]

    def interleave(dst_ref, zb, aug_ref):
        for h in range(nh):
            dst_ref[:, 2 * h * HEAD_DIM:(2 * h + 1) * HEAD_DIM] = zb[:, h * HEAD_DIM:(h + 1) * HEAD_DIM]
            dst_ref[:, (2 * h + 1) * HEAD_DIM:(2 * h + 2) * HEAD_DIM] = aug_ref[:, h * HEAD_DIM:(h + 1) * HEAD_DIM]

    @pl.when(j == 0)
    def _():
        interleave(qa_ref, (z * (ATTN_SCALE * LOG2E)).astype(BF16), qaug_ref)

    @pl.when(j == 1)
    def _():
        interleave(ka_ref, z.astype(BF16), kaug_ref)

    @pl.when(j == 2)
    def _():
        ones_row = (_iota2((VT_ROWS - HEAD_DIM, tm), 0) == 0).astype(BF16)
        for h in range(nh):
            vt_ref[h, :HEAD_DIM, :] = jnp.transpose(z[:, h * HEAD_DIM:(h + 1) * HEAD_DIM]).astype(BF16)
            vt_ref[h, HEAD_DIM:, :] = ones_row


def _in_proj(x, g, w4, wf, bf, kv_prev, *, layer, depth, np_rows, ns_rows, tm, rep):
    r, d = x.shape
    nsec, _, aw = w4.shape
    nh = aw // HEAD_DIM
    sample_tile, sample_off = divmod(np_rows, tm)
    assert sample_off + ns_rows <= tm
    assert sample_off > 0 and sample_tile == r // tm - 1
    n_alias = 0 if kv_prev is None else len(kv_prev)
    kernel_fn = functools.partial(_in_kernel, rep=rep, n_alias=n_alias, sample_tile=sample_tile,
                                  sample_off=sample_off, n_sample=ns_rows)
    prompt_spec = pl.BlockSpec((None, tm * nh, HEAD_DIM), lambda i, j: (layer, jnp.minimum(i, sample_tile - 1), 0))
    tail_spec = pl.BlockSpec((None, sample_off * nh, HEAD_DIM), lambda i, j: (layer, 0, 0))
    sample_spec = pl.BlockSpec((None, ns_rows * nh, HEAD_DIM), lambda i, j: (layer, 0, 0))
    prompt_shape = jax.ShapeDtypeStruct((depth, np_rows * nh, HEAD_DIM), F32)
    tail_shape = jax.ShapeDtypeStruct((depth, sample_off * nh, HEAD_DIM), F32)
    sample_shape = jax.ShapeDtypeStruct((depth, ns_rows * nh, HEAD_DIM), F32)
    return pl.pallas_call(
        kernel_fn,
        grid=(r // tm, nsec),
        in_specs=[
            pl.BlockSpec((tm, d), lambda i, j: (i, 0)),
            pl.BlockSpec((1, d), lambda i, j: (0, 0)),
            pl.BlockSpec((1, d, aw), lambda i, j: (j, 0, 0)),
            pl.BlockSpec((d, LANES), lambda i, j: (0, 0)),
            pl.BlockSpec((1, LANES), lambda i, j: (0, 0)),
        ] + [pl.BlockSpec(memory_space=pl.ANY)] * n_alias,
        out_specs=[
            pl.BlockSpec((tm, aw), lambda i, j: (i, 0)),
            prompt_spec, prompt_spec, tail_spec, tail_spec, sample_spec, sample_spec,
            pl.BlockSpec((tm, aw), lambda i, j: (i, 0)),
            pl.BlockSpec((tm, 2 * aw), lambda i, j: (i, 0)),
            pl.BlockSpec((tm, 2 * aw), lambda i, j: (i, 0)),
            pl.BlockSpec((nh, None, VT_ROWS, tm), lambda i, j: (0, i, 0, 0)),
            pl.BlockSpec((tm, LANES), lambda i, j: (i, 0)),
        ],
        out_shape=[
            jax.ShapeDtypeStruct((r, aw), F32),
            prompt_shape, prompt_shape, tail_shape, tail_shape, sample_shape, sample_shape,
            jax.ShapeDtypeStruct((r, aw), F32),
            jax.ShapeDtypeStruct((r, 2 * aw), BF16),
            jax.ShapeDtypeStruct((r, 2 * aw), BF16),
            jax.ShapeDtypeStruct((nh, r // tm, VT_ROWS, tm), BF16),
            jax.ShapeDtypeStruct((r, LANES), F32),
        ],
        scratch_shapes=[pltpu.VMEM((tm, d), BF16), pltpu.VMEM((1, LANES), F32),
                        pltpu.VMEM((tm, aw), BF16), pltpu.VMEM((tm, aw), BF16)],
        input_output_aliases={5 + n: 1 + n for n in range(n_alias)},
        compiler_params=_cparams("arbitrary", "arbitrary"),
        name="in_proj",
    )(x, g, w4, wf, bf, *(kv_prev or ()))


def _attn_p_kernel(qa_ref, ka_ref, vt_ref, o_ref, sa_ref, sb_ref, acc_ref):
    qi = pl.program_id(1)
    tq = qa_ref.shape[0]
    tk = sa_ref.shape[0]
    n_diag = tq // tk
    n_off = qi * n_diag

    def produce(s_ref, kb):
        s = _dot_nt(ka_ref[pl.ds(pl.multiple_of(kb * tk, tk), tk), :], qa_ref[...])
        s_ref[...] = s
        return jnp.max(s, axis=0, keepdims=True)

    def consume(s_ref, m_blk, kb, m):
        m_new = jnp.maximum(m, m_blk)
        p = jnp.exp2(s_ref[...] - m_new)
        acc_ref[...] = jnp.exp2(m - m_new) * acc_ref[...] + _dot(vt_ref[kb], p.astype(BF16))
        return m_new

    acc_ref[...] = jnp.zeros_like(acc_ref)
    m_first = produce(sa_ref, 0)

    def pair(kb, carry):
        m_a, m = carry
        m_b = produce(sb_ref, kb + 1)
        m = consume(sa_ref, m_a, kb, m)
        m_a = produce(sa_ref, kb + 2)
        m = consume(sb_ref, m_b, kb + 1, m)
        return m_a, m

    init = (m_first, jnp.full((1, tq), NEG_INF, F32))
    carry = lax.fori_loop(0, n_off // 4, lambda i, c: pair(4 * i + 2, pair(4 * i, c)), init)
    done = (n_off // 4) * 4
    m_a, m = lax.fori_loop(0, (n_off % 4) // 2, lambda i, c: pair(done + 2 * i, c), carry)
    def produce_diag(s_ref, j):
        c0 = j * tk
        kb = n_off + j
        s_ref[:, :tq - c0] = _dot_nt(ka_ref[pl.ds(pl.multiple_of(kb * tk, tk), tk), :], qa_ref[c0:, :])

    def consume_diag(s_ref, j, m):
        c0 = j * tk
        w = tq - c0
        visible = _iota2((tk, w), 1) >= _iota2((tk, w), 0)
        s = jnp.where(visible, s_ref[:, :w], NEG_INF)
        m_old = m[:, c0:]
        m_new = jnp.maximum(m_old, jnp.max(s, axis=0, keepdims=True))
        p = jnp.exp2(s - m_new)
        acc_ref[:, c0:] = jnp.exp2(m_old - m_new) * acc_ref[:, c0:] + _dot(vt_ref[n_off + j], p.astype(BF16))
        return m_new if c0 == 0 else jnp.concatenate([m[:, :c0], m_new], axis=1)

    def tail(lead_block, m_lead, m):
        bufs = (sa_ref, sb_ref)
        has_lead = lead_block is not None
        for idx in range(n_diag + has_lead):
            cur = bufs[idx % 2]
            j = idx - has_lead
            if j + 1 < n_diag:
                produce_diag(bufs[(idx + 1) % 2], j + 1)
            if j < 0:
                m = consume(cur, m_lead, lead_block, m)
            else:
                m = consume_diag(cur, j, m)
        o_ref[...] = jnp.transpose(acc_ref[:HEAD_DIM, :] / acc_ref[HEAD_DIM:HEAD_DIM + 1, :])

    @pl.when(n_off % 2 == 0)
    def _():
        tail(None, None, m)

    @pl.when(n_off % 2 == 1)
    def _():
        tail(n_off - 1, m_a, m)


def _attn_prompt(qa, ka, vt, *, tq):
    nh, nkb, _, tk = vt.shape
    r = nkb * tk
    return pl.pallas_call(
        _attn_p_kernel,
        grid=(nh, r // tq),
        in_specs=[
            pl.BlockSpec((tq, 2 * HEAD_DIM), lambda h, i: (i, h)),
            pl.BlockSpec((r, 2 * HEAD_DIM), lambda h, i: (0, h)),
            pl.BlockSpec((None, nkb, VT_ROWS, tk), lambda h, i: (h, 0, 0, 0)),
        ],
        out_specs=pl.BlockSpec((tq, HEAD_DIM), lambda h, i: (i, h)),
        out_shape=jax.ShapeDtypeStruct((r, nh * HEAD_DIM), F32),
        scratch_shapes=[pltpu.VMEM((tk, tq), F32), pltpu.VMEM((tk, tq), F32), pltpu.VMEM((VT_ROWS, tq), F32)],
        compiler_params=_cparams("parallel", "arbitrary"),
        name="attn_p",
    )(qa, ka, vt)


def _attn_s_kernel(qbd_ref, ck_ref, cv_ref, cf_ref, kn_ref, vn_ref, fn_ref, attn_in_ref, o_ref,
                   m_ref, l_ref, acc_ref, suf_ref, cq_ref, *, rep, seg, nh):
    del attn_in_ref
    step = pl.program_id(1)
    s_new = fn_ref.shape[0]
    tk = cf_ref.shape[0]
    qbd = qbd_ref[...]

    def heads_side_by_side(ref, n):
        return jnp.concatenate([ref[pl.ds(h, n, stride=nh), :].astype(BF16) for h in range(nh)], axis=1)

    def update(t, v_bf):
        m_old = m_ref[...]
        m_new = jnp.maximum(m_old, jnp.max(t, axis=0, keepdims=True))
        p = jnp.exp(t - m_new)
        alpha = jnp.exp(m_old - m_new)
        l_ref[...] = alpha * l_ref[...] + jnp.sum(p, axis=0, keepdims=True)
        m_ref[...] = m_new
        alpha_col = jnp.transpose(jnp.broadcast_to(alpha, (LANES, LANES)))
        pv = _dot(jnp.transpose(p).astype(BF16), v_bf)
        for c in range(nh):
            sl = slice(c * HEAD_DIM, (c + 1) * HEAD_DIM)
            acc_ref[:, sl] = acc_ref[:, sl] * alpha_col + pv[:, sl]

    @pl.when(step == 0)
    def _():
        m_ref[...] = jnp.full_like(m_ref, NEG_INF)
        l_ref[...] = jnp.zeros_like(l_ref)
        acc_ref[...] = jnp.zeros_like(acc_ref)
        tril = (_iota2((s_new, s_new), 1) <= _iota2((s_new, s_new), 0)).astype(BF16)
        cn = _dot_exact_lhs01(tril, fn_ref[...])
        krow = _iota2((s_new, LANES), 0)
        qlane = _iota2((s_new, LANES), 1) % rep
        cq = jnp.sum(jnp.where(krow == qlane, cn, 0.0), axis=0, keepdims=True)
        cq_ref[...] = cq
        suf_ref[...] = jnp.zeros_like(suf_ref)
        st = _dot(heads_side_by_side(kn_ref, s_new), qbd)
        t = jnp.where(krow <= qlane, st + (cq - cn), NEG_INF)
        update(t, heads_side_by_side(vn_ref, s_new))

    triu = (_iota2((seg, seg), 1) > _iota2((seg, seg), 0)).astype(BF16)
    carry = suf_ref[...]
    sufs = [None] * (tk // seg)
    for sidx in reversed(range(tk // seg)):
        x = cf_ref[sidx * seg:(sidx + 1) * seg, :]
        sfx = _dot_exact_lhs01(triu, x) + carry
        sufs[sidx] = sfx
        carry = sfx[0:1, :] + x[0:1, :]
    suf_ref[...] = carry
    bias = jnp.concatenate(sufs, axis=0) + cq_ref[...]
    update(_dot(heads_side_by_side(ck_ref, tk), qbd) + bias, heads_side_by_side(cv_ref, tk))

    @pl.when(step == pl.num_programs(1) - 1)
    def _():
        l_col = jnp.transpose(jnp.broadcast_to(l_ref[...], (LANES, LANES)))
        for c in range(nh):
            rws = slice(c * rep, c * rep + s_new)
            sl = slice(c * HEAD_DIM, (c + 1) * HEAD_DIM)
            o_ref[:, sl] = acc_ref[rws, sl] / l_col[rws, :]


def _attn_sample(layer, qbd, cache_k, cache_v, cf, kf, vf, logf, attn, *, np_rows, tk, rep):
    depth, nb, past, nh, _ = cache_k.shape
    aw = nh * HEAD_DIM
    s_new = S5_CHUNK
    nkb = past // tk
    base = np_rows // s_new
    seg = min(tk, LANES)

    ck = cache_k.reshape(depth, nb, past * nh, HEAD_DIM)
    cv = cache_v.reshape(depth, nb, past * nh, HEAD_DIM)
    cache_spec = pl.BlockSpec((None, None, tk * nh, HEAD_DIM), lambda b, s: (layer, b, nkb - 1 - s, 0))
    return pl.pallas_call(
        functools.partial(_attn_s_kernel, rep=rep, seg=seg, nh=nh),
        grid=(nb, nkb),
        in_specs=[
            pl.BlockSpec((None, aw, LANES), lambda b, s: (b, 0, 0)),
            cache_spec,
            cache_spec,
            pl.BlockSpec((None, None, tk, LANES), lambda b, s: (layer, b, nkb - 1 - s, 0)),
            pl.BlockSpec((None, s_new * nh, HEAD_DIM), lambda b, s: (layer, b, 0)),
            pl.BlockSpec((None, s_new * nh, HEAD_DIM), lambda b, s: (layer, b, 0)),
            pl.BlockSpec((s_new, LANES), lambda b, s: (base + b, 0)),
            pl.BlockSpec(memory_space=pl.ANY),
        ],
        out_specs=pl.BlockSpec((s_new, aw), lambda b, s: (base + b, 0)),
        out_shape=jax.ShapeDtypeStruct(attn.shape, F32),
        scratch_shapes=[
            pltpu.VMEM((1, LANES), F32), pltpu.VMEM((1, LANES), F32), pltpu.VMEM((LANES, aw), F32),
            pltpu.VMEM((1, LANES), F32), pltpu.VMEM((1, LANES), F32),
        ],
        input_output_aliases={7: 0},
        compiler_params=_cparams("parallel", "arbitrary"),
        name="attn_s",
    )(qbd, ck, cv, cf, kf, vf, logf, attn)


def _s5_kernel(u_ref, kc_ref, wc_ref, vc_ref, a_ref, hinit_ref, d_ref, wglu2_ref, o_ref, hend_ref,
               kds2_ref, wfull_ref, vfull_ref, s_ref, hprev_ref, hc_ref, *, n_prompt_chunks):
    rt = pl.program_id(1)
    ct, sd = s_ref.shape
    half = sd // 2
    L = S5_CHUNK
    P = SSM_STATE

    @pl.when(rt == 0)
    def _():
        hc_ref[...] = jnp.zeros_like(hc_ref)
        r_, c_ = _iota2((2 * P, sd), 0), _iota2((2 * P, sd), 1)
        rep_w = jnp.logical_and(r_ // P == c_ // half, r_ % P == c_ % P).astype(BF16)
        r_, c_ = _iota2((L * LANES, sd), 0), _iota2((L * LANES, sd), 1)
        same = (r_ // SSM_CH) % GROUPS_PER_TILE == (c_ % half) // P
        wfull_ref[...] = jnp.where(same, _dot(wc_ref[...], rep_w), 0.0).astype(BF16)
        r_, c_ = _iota2((L * SSM_CH, L * LANES), 0), _iota2((L * SSM_CH, L * LANES), 1)
        rep_v = jnp.logical_and(r_ // SSM_CH == c_ // LANES, r_ % SSM_CH == c_ % SSM_CH).astype(BF16)
        r_, c_ = _iota2((sd, L * LANES), 0), _iota2((sd, L * LANES), 1)
        same = (r_ % half) // P == (c_ % LANES) // SSM_CH
        vfull_ref[...] = jnp.where(same, _dot(vc_ref[...], rep_v), 0.0).astype(BF16)
        r_, c_ = _iota2((SSM_CH, LANES), 0), _iota2((SSM_CH, LANES), 1)
        rep_k = (r_ == c_ % SSM_CH).astype(BF16)
        r_, c_ = _iota2((L * LANES, LANES), 0), _iota2((L * LANES, LANES), 1)
        same = (r_ // SSM_CH) % GROUPS_PER_TILE == c_ // SSM_CH
        kds = jnp.where(same, _dot(kc_ref[...], rep_k), 0.0).astype(BF16)
        kds2_ref[:, :LANES] = kds
        kds2_ref[:LANES, LANES:] = jnp.zeros((LANES, LANES), BF16)
        kds2_ref[LANES:, LANES:] = kds[:(L - 1) * LANES, :]

    def u_at(tau):
        return u_ref[pl.ds(tau, ct, stride=L), :]

    xr = jnp.concatenate([u_at(L - 1 - j).astype(BF16) for j in range(L)], axis=1)
    s_ref[...] = _dot(xr, wfull_ref[...])

    ar = a_ref[:, :half]
    ai = a_ref[:, half:]

    def scan(c, hcar):
        cg = rt * ct + c
        reset = jnp.logical_or(cg == 0, cg >= n_prompt_chunks)
        hp = jnp.where(reset, hinit_ref[pl.ds(c, 1), :], hcar)
        hprev_ref[pl.ds(c, 1), :] = hp
        s = s_ref[pl.ds(c, 1), :]
        hr = hp[:, :half]
        hi = hp[:, half:]
        hn = jnp.concatenate([ar * hr - ai * hi + s[:, :half], ar * hi + ai * hr + s[:, half:]], axis=1)
        hend_ref[pl.ds(c, 1), :] = hn
        return hn

    hc_ref[...] = lax.fori_loop(0, ct, scan, hc_ref[...], unroll=SCAN_UNROLL)

    ystate = _dot(hprev_ref[...].astype(BF16), vfull_ref[...])
    d2 = jnp.concatenate([d_ref[...], d_ref[...]], axis=1)
    wglu2 = wglu2_ref[...]
    for tau in range(0, L, 2):
        lag_rows = (tau + 2) * LANES
        yy = _dot(xr[:, (L - 2 - tau) * LANES:], kds2_ref[:lag_rows, :])
        yy = yy + jnp.concatenate([ystate[:, (tau + 1) * LANES:(tau + 2) * LANES],
                                   ystate[:, tau * LANES:(tau + 1) * LANES]], axis=1)
        yy = yy + d2 * jnp.concatenate([u_at(tau + 1), u_at(tau)], axis=1)
        yy = 0.5 * yy * (1.0 + jnp.tanh(math.sqrt(2.0 / math.pi) * (yy + 0.044715 * (yy * yy * yy))))
        out = yy * jax.nn.sigmoid(_dot(yy.astype(BF16), wglu2))
        o_ref[pl.ds(tau + 1, ct, stride=L), :] = out[:, :LANES]
        o_ref[pl.ds(tau, ct, stride=L), :] = out[:, LANES:]


def _s5(u, kc, wc, vc, a16, hinit, d, wglu2, *, rows_tile, n_prompt_chunks):
    r, sw = u.shape
    no = sw // LANES
    L = S5_CHUNK
    ct = rows_tile // L
    nrt = r // rows_tile
    sd = a16.shape[2]
    return pl.pallas_call(
        functools.partial(_s5_kernel, n_prompt_chunks=n_prompt_chunks),
        grid=(no, nrt),
        in_specs=[
            pl.BlockSpec((rows_tile, LANES), lambda o, t: (t, o)),
            pl.BlockSpec((None, L * LANES, SSM_CH), lambda o, t: (o, 0, 0)),
            pl.BlockSpec((None, L * LANES, 2 * SSM_STATE), lambda o, t: (o, 0, 0)),
            pl.BlockSpec((None, sd, L * SSM_CH), lambda o, t: (o, 0, 0)),
            pl.BlockSpec((None, 1, sd), lambda o, t: (o, 0, 0)),
            pl.BlockSpec((ct, sd), lambda o, t: (t, o)),
            pl.BlockSpec((1, LANES), lambda o, t: (0, o)),
            pl.BlockSpec((None, 2 * LANES, 2 * LANES), lambda o, t: (o, 0, 0)),
        ],
        out_specs=[
            pl.BlockSpec((rows_tile, LANES), lambda o, t: (t, o)),
            pl.BlockSpec((ct, sd), lambda o, t: (t, o)),
        ],
        out_shape=[
            jax.ShapeDtypeStruct((r, sw), F32),
            jax.ShapeDtypeStruct((r // L, no * sd), F32),
        ],
        scratch_shapes=[pltpu.VMEM((L * LANES, 2 * LANES), BF16),
                        pltpu.VMEM((L * LANES, sd), BF16), pltpu.VMEM((sd, L * LANES), BF16),
                        pltpu.VMEM((ct, sd), F32), pltpu.VMEM((ct, sd), F32), pltpu.VMEM((1, sd), F32)],
        compiler_params=_cparams("parallel", "arbitrary"),
        name="s5",
    )(u, kc, wc, vc, a16, hinit, d, wglu2)


def _s5_operators(a_re, a_im, log_dt, b_re, b_im, c_re, c_im, w_glu):
    g, p = a_re.shape
    c = SSM_CH
    L = S5_CHUNK
    gt = GROUPS_PER_TILE
    no = g // gt
    hp = lax.Precision.HIGHEST
    cmul = lambda xr, xi, yr, yi: (xr * yr - xi * yi, xr * yi + xi * yr)
    dt = jnp.exp(log_dt)[:, None]
    mag = jnp.exp(a_re * dt)
    ar, ai = mag * jnp.cos(a_im * dt), mag * jnp.sin(a_im * dt)
    den = a_re * a_re + a_im * a_im
    fr = ((ar - 1.0) * a_re + ai * a_im) / den
    fi = (ai * a_re - (ar - 1.0) * a_im) / den
    bbr, bbi = cmul(fr[..., None], fi[..., None], b_re, b_im)
    pr, pi = [jnp.ones_like(ar)], [jnp.zeros_like(ar)]
    for _ in range(L):
        nr, ni = cmul(pr[-1], pi[-1], ar, ai)
        pr.append(nr)
        pi.append(ni)
    pr, pi = jnp.stack(pr), jnp.stack(pi)

    abr, abi = cmul(pr[:L, :, :, None], pi[:L, :, :, None], bbr[None], bbi[None])
    kd = (jnp.einsum('gop,dgpi->gdio', c_re, abr, precision=hp)
          - jnp.einsum('gop,dgpi->gdio', c_im, abi, precision=hp))
    eye = jnp.eye(gt, dtype=F32)
    kc = kd.reshape(no, gt, L, c, c).transpose(0, 2, 1, 3, 4).reshape(no, L * LANES, c)

    wc = jnp.stack([abr, abi], axis=0).reshape(2, L, no, gt, p, c)
    wc = wc.transpose(2, 1, 3, 5, 0, 4).reshape(no, L * LANES, 2 * p)

    zr, zi = cmul(c_re.transpose(0, 2, 1)[None], c_im.transpose(0, 2, 1)[None],
                  pr[1:, :, :, None], pi[1:, :, :, None])
    vc = jnp.stack([zr, -zi], axis=0).reshape(2, L, no, gt, p, c)
    vc = vc.transpose(2, 0, 3, 4, 1, 5).reshape(no, 2 * gt * p, L * c)

    a16 = jnp.concatenate([pr[L].reshape(no, 1, gt * p), pi[L].reshape(no, 1, gt * p)], axis=2)
    wg = w_glu.reshape(no, gt, c, c)
    wglu = (wg[:, :, :, None, :] * eye[None, :, None, :, None]).reshape(no, LANES, LANES)
    zero = jnp.zeros_like(wglu)
    wglu2 = jnp.concatenate([jnp.concatenate([wglu, zero], axis=2), jnp.concatenate([zero, wglu], axis=2)], axis=1)
    return kc.astype(BF16), wc.astype(BF16), vc.astype(BF16), a16, wglu2.astype(BF16)


def _out_kernel(x_ref, attn_ref, ssm_ref, ga_ref, gs_ref, gpost_ref, wa_ref, ws_ref, o_ref):
    an = _rms(attn_ref[...], ga_ref[...]).astype(BF16)
    sn = _rms(ssm_ref[...], gs_ref[...]).astype(BF16)
    mixed = _dot(an, wa_ref[...]) + _dot(sn, ws_ref[...])
    o_ref[...] = x_ref[...] + _rms(mixed, gpost_ref[...])


def _out_proj(x, attn, ssm, ga, gs, gpost, w_out, layer, *, tm):
    r, d = x.shape
    aw = attn.shape[1]
    sw = ssm.shape[1]
    assert aw == sw
    once = dict(pipeline_mode=pl.Buffered(1))
    return pl.pallas_call(
        _out_kernel,
        grid=(r // tm,),
        in_specs=[
            pl.BlockSpec((tm, d), lambda i: (i, 0)),
            pl.BlockSpec((tm, aw), lambda i: (i, 0)),
            pl.BlockSpec((tm, sw), lambda i: (i, 0)),
            pl.BlockSpec((1, aw), lambda i: (0, 0)),
            pl.BlockSpec((1, sw), lambda i: (0, 0)),
            pl.BlockSpec((1, d), lambda i: (0, 0)),
            pl.BlockSpec((None, aw, d), lambda i: (layer, 0, 0), **once),
            pl.BlockSpec((None, sw, d), lambda i: (layer, 1, 0), **once),
        ],
        out_specs=pl.BlockSpec((tm, d), lambda i: (i, 0)),
        out_shape=jax.ShapeDtypeStruct((r, d), F32),
        compiler_params=_cparams("parallel"),
        name="out_proj",
    )(x, attn, ssm, ga, gs, gpost, w_out, w_out)


def _tiles(r_min):
    tm = 512
    r = -(-r_min // tm) * tm
    s5_tiles = 4 if (r // 4) % (8 * S5_CHUNK) == 0 else 1
    tq = 3 * tm if r % (3 * tm) == 0 else tm
    tm_dense = 768 if r % 768 == 0 else tm
    return dict(r=r, tm=tm, tm_dense=tm_dense, tq=tq, tf=512, s5_rows=r // s5_tiles, tk_cache=1024)


def _forward(x_prompt, x_sample, cache_k, cache_v, cache_logf, state_ssm_re, state_ssm_im, meta_tokens,
             ffn1_norm_pre, ffn1_norm_post, ffn1_w_gate, ffn1_w_up, ffn1_w_down,
             mix_norm_pre, mix_norm_post, w_in, b_forget,
             ssm_a_re, ssm_a_im, ssm_log_dt, ssm_b_re, ssm_b_im, ssm_c_re, ssm_c_im, ssm_d, ssm_w_glu,
             attn_out_norm, ssm_out_norm, w_out,
             ffn2_norm_pre, ffn2_norm_post, ffn2_w_gate, ffn2_w_up, ffn2_w_down, tiles=None):
    bsz, seq, d = x_prompt.shape
    nb, s_new, _ = x_sample.shape
    depth, _, past, nh, _ = cache_k.shape
    n_meta = meta_tokens.shape[0]
    aw = nh * HEAD_DIM
    g, p = ssm_a_re.shape[1:]
    L = S5_CHUNK
    assert bsz == 1 and s_new == L and n_meta % L == 0 and seq % L == 0 and p == SSM_STATE
    assert LANES % nh == 0 and s_new <= LANES // nh and g % GROUPS_PER_TILE == 0
    rep = LANES // nh
    no = g // GROUPS_PER_TILE
    sd = 2 * GROUPS_PER_TILE * p
    np_rows = n_meta + seq
    ns_rows = nb * s_new
    t = tiles or _tiles(np_rows + ns_rows)
    r = t["r"]
    npc = np_rows // L
    n_chunks = r // L

    x = jnp.concatenate([meta_tokens.astype(F32), x_prompt[0], x_sample.reshape(ns_rows, d),
                         jnp.zeros((r - np_rows - ns_rows, d), F32)], axis=0)
    cf = jnp.repeat(cache_logf, rep, axis=-1)

    row2 = lambda a: a.reshape(1, -1)
    ffn1_w = [w.astype(BF16) for w in (ffn1_w_gate, ffn1_w_up, ffn1_w_down)]
    ffn2_w = [w.astype(BF16) for w in (ffn2_w_gate, ffn2_w_up, ffn2_w_down)]
    w_out_b = w_out.astype(BF16)
    outs = dict(logf=[], hend=[])
    kv = None
    for l in range(depth):
        x = _ffn(x, row2(ffn1_norm_pre[l]), row2(ffn1_norm_post[l]), *ffn1_w, l, tm=t["tm_dense"], tf=t["tf"])

        wl = w_in[l]
        w4 = jnp.stack([wl[:, :aw], wl[:, aw:2 * aw], wl[:, 2 * aw:3 * aw], wl[:, 3 * aw + nh:]]).astype(BF16)
        wf = jnp.repeat(wl[:, 3 * aw:3 * aw + nh], rep, axis=1).astype(BF16)
        bf = jnp.repeat(b_forget[l], rep).reshape(1, LANES)
        qf, *kv, uf, qa, ka, vt, logf = _in_proj(x, row2(mix_norm_pre[l]), w4, wf, bf, kv, layer=l, depth=depth,
                                                 np_rows=np_rows, ns_rows=ns_rows, tm=t["tm"], rep=rep)

        attn = _attn_prompt(qa, ka, vt, tq=t["tq"])

        qs = (qf[np_rows:np_rows + ns_rows] * ATTN_SCALE).astype(BF16).reshape(nb, s_new, nh, HEAD_DIM)
        qs = jnp.pad(qs.transpose(0, 2, 3, 1), ((0, 0), (0, 0), (0, 0), (0, rep - s_new)))
        qbd = (qs[:, :, :, None, :] * jnp.eye(nh, dtype=BF16)[None, :, None, :, None]).reshape(nb, aw, LANES)
        attn = _attn_sample(l, qbd, cache_k, cache_v, cf, kv[4], kv[5], logf, attn,
                            np_rows=np_rows, tk=t["tk_cache"], rep=rep)

        kc, wc, vc, a16, wglu2 = _s5_operators(ssm_a_re[l], ssm_a_im[l], ssm_log_dt[l], ssm_b_re[l],
                                               ssm_b_im[l], ssm_c_re[l], ssm_c_im[l], ssm_w_glu[l])
        h0 = jnp.concatenate([state_ssm_re[l].reshape(nb, no, 1, sd // 2),
                              state_ssm_im[l].reshape(nb, no, 1, sd // 2)], axis=2).reshape(nb, no * sd)
        hinit = jnp.zeros((n_chunks, no * sd), F32).at[npc:npc + nb].set(h0)
        ssm, hend = _s5(uf, kc, wc, vc, a16, hinit, row2(ssm_d[l]), wglu2,
                        rows_tile=t["s5_rows"], n_prompt_chunks=npc)

        x = _out_proj(x, attn, ssm, row2(attn_out_norm[l]), row2(ssm_out_norm[l]), row2(mix_norm_post[l]),
                      w_out_b, l, tm=t["tm_dense"])

        x = _ffn(x, row2(ffn2_norm_pre[l]), row2(ffn2_norm_post[l]), *ffn2_w, l, tm=t["tm_dense"], tf=t["tf"])

        outs["logf"].append(logf[:, ::rep])
        outs["hend"].append(hend[npc - 1:npc + nb].reshape(1 + nb, no, 2, GROUPS_PER_TILE, p))

    sl_p = slice(0, np_rows)
    sl_s = slice(np_rows, np_rows + ns_rows)
    rows = lambda arrs, sl: jnp.stack([a[sl] for a in arrs])
    heads = lambda a, n: a.reshape(depth, -1, n, nh, HEAD_DIM)
    states = lambda a: a.reshape(depth, -1, g, p)
    h_all = jnp.stack(outs["hend"])
    tail_at = (0, (np_rows // t["tm"]) * t["tm"] * nh, 0)
    k_prompt = lax.dynamic_update_slice(kv[0], kv[2], tail_at)
    v_prompt = lax.dynamic_update_slice(kv[1], kv[3], tail_at)
    return (x[n_meta:np_rows][None], x[sl_s].reshape(nb, s_new, d),
            heads(k_prompt, np_rows), heads(v_prompt, np_rows),
            rows(outs["logf"], sl_p).reshape(depth, 1, np_rows, nh),
            states(h_all[:, :1, :, 0]), states(h_all[:, :1, :, 1]),
            heads(kv[4], s_new), heads(kv[5], s_new),
            rows(outs["logf"], sl_s).reshape(depth, nb, s_new, nh),
            states(h_all[:, 1:, :, 0]), states(h_all[:, 1:, :, 1]))


def kernel(x_prompt, x_sample, cache_k, cache_v, cache_logf, state_ssm_re, state_ssm_im, meta_tokens, ffn1_norm_pre, ffn1_norm_post, ffn1_w_gate, ffn1_w_up, ffn1_w_down, mix_norm_pre, mix_norm_post, w_in, b_forget, ssm_a_re, ssm_a_im, ssm_log_dt, ssm_b_re, ssm_b_im, ssm_c_re, ssm_c_im, ssm_d, ssm_w_glu, attn_out_norm, ssm_out_norm, w_out, ffn2_norm_pre, ffn2_norm_post, ffn2_w_gate, ffn2_w_up, ffn2_w_down):
    return _forward(x_prompt, x_sample, cache_k, cache_v, cache_logf, state_ssm_re, state_ssm_im, meta_tokens,
                    ffn1_norm_pre, ffn1_norm_post, ffn1_w_gate, ffn1_w_up, ffn1_w_down,
                    mix_norm_pre, mix_norm_post, w_in, b_forget,
                    ssm_a_re, ssm_a_im, ssm_log_dt, ssm_b_re, ssm_b_im, ssm_c_re, ssm_c_im, ssm_d, ssm_w_glu,
                    attn_out_norm, ssm_out_norm, w_out,
                    ffn2_norm_pre, ffn2_norm_post, ffn2_w_gate, ffn2_w_up, ffn2_w_down)
```

```python
import functools
import math

import jax
import jax.numpy as jnp
from jax import lax
from jax.experimental import pallas as pl
from jax.experimental.pallas import tpu as pltpu

F32 = jnp.float32
BF16 = jnp.bfloat16

HEAD_DIM = 128
SSM_CH = 16
SSM_STATE = 64
S5_CHUNK = 16
LANES = 128
GROUPS_PER_TILE = LANES // SSM_CH
MACARON_W = 0.5
NORM_EPS = 1e-6
ATTN_SCALE = HEAD_DIM ** -0.5
LOG2E = math.log2(math.e)
NEG_INF = -1e30
VMEM_LIMIT = 56 * 1024 * 1024
N_BIAS_TERMS = 3
SCAN_UNROLL = 4
VT_ROWS = HEAD_DIM + 16


def _cparams(*sem):
    return pltpu.CompilerParams(dimension_semantics=sem, vmem_limit_bytes=VMEM_LIMIT)


def _rms(x, g):
    return x * lax.rsqrt(jnp.mean(x * x, axis=-1, keepdims=True) + NORM_EPS) * g


def _dot(a, b):
    return jnp.dot(a, b, preferred_element_type=F32)


def _dot_nt(a, b):
    return lax.dot_general(a, b, (((1,), (1,)), ((), ())), preferred_element_type=F32)


def _split3(x):
    hi = x.astype(BF16)
    r1 = x - hi.astype(F32)
    mid = r1.astype(BF16)
    lo = (r1 - mid.astype(F32)).astype(BF16)
    return hi, mid, lo


def _dot_exact_lhs01(sel, x):
    hi, mid, lo = _split3(x)
    return _dot(sel, hi) + _dot(sel, mid) + _dot(sel, lo)


def _iota2(shape, axis):
    return lax.broadcasted_iota(jnp.int32, shape, axis)


def _ffn_kernel(x_ref, gpre_ref, gpost_ref, wg_ref, wu_ref, wd_ref, o_ref, xn_ref, acc_ref):
    j = pl.program_id(1)

    @pl.when(j == 0)
    def _():
        xn_ref[...] = _rms(x_ref[...], gpre_ref[...]).astype(BF16)
        acc_ref[...] = jnp.zeros_like(acc_ref)

    xn = xn_ref[...]
    g = _dot(xn, wg_ref[...])
    u = _dot(xn, wu_ref[...])
    h = (g * jax.nn.sigmoid(g) * u).astype(BF16)
    acc_ref[...] += _dot(h, wd_ref[...])

    @pl.when(j == pl.num_programs(1) - 1)
    def _():
        o_ref[...] = x_ref[...] + MACARON_W * _rms(acc_ref[...], gpost_ref[...])


def _ffn(x, g_pre, g_post, wg, wu, wd, layer, *, tm, tf):
    r, d = x.shape
    f = wg.shape[2]
    return pl.pallas_call(
        _ffn_kernel,
        grid=(r // tm, f // tf),
        in_specs=[
            pl.BlockSpec((tm, d), lambda i, j: (i, 0)),
            pl.BlockSpec((1, d), lambda i, j: (0, 0)),
            pl.BlockSpec((1, d), lambda i, j: (0, 0)),
            pl.BlockSpec((None, d, tf), lambda i, j: (layer, 0, j)),
            pl.BlockSpec((None, d, tf), lambda i, j: (layer, 0, j)),
            pl.BlockSpec((None, tf, d), lambda i, j: (layer, j, 0)),
        ],
        out_specs=pl.BlockSpec((tm, d), lambda i, j: (i, 0)),
        out_shape=jax.ShapeDtypeStruct((r, d), F32),
        scratch_shapes=[pltpu.VMEM((tm, d), BF16), pltpu.VMEM((tm, d), F32)],
        compiler_params=_cparams("parallel", "arbitrary"),
        name="ffn",
    )(x, g_pre, g_post, wg, wu, wd)


def _in_kernel(x_ref, g_ref, w_ref, wf_ref, bf_ref, tril_ref, selk_ref, selq_ref, *refs,
               n_alias, np_rows, ns_rows):
    (qs_ref, kp_ref, vp_ref, kt_ref, vt_tail_ref, ks_ref, vs_ref, uf_ref, qa_ref, ka_ref, vt_ref, logf_ref,
     carry_ref) = refs[n_alias:]
    i = pl.program_id(0)
    tm = x_ref.shape[0]
    aw = w_ref.shape[2]
    nh = aw // HEAD_DIM
    nb = N_BIAS_TERMS
    n_full, n_tail = divmod(np_rows, tm)

    xn = _rms(x_ref[...], g_ref[...]).astype(BF16)
    logf = jax.nn.log_sigmoid(_dot(xn, wf_ref[...]) + bf_ref[...])
    logf_ref[...] = logf

    @pl.when(i == 0)
    def _():
        carry_ref[...] = jnp.zeros_like(carry_ref)

    fcum = _dot_exact_lhs01(tril_ref[...], logf) + carry_ref[...]
    carry_ref[...] = fcum[tm - 1:tm, :]
    terms = jnp.concatenate(_split3(fcum * LOG2E), axis=1)
    lane_t = _iota2((tm, aw), 1) % HEAD_DIM
    kaug = (jnp.logical_and(lane_t >= nb, lane_t < 2 * nb).astype(F32) - _dot(terms, selk_ref[...])).astype(BF16)
    qaug = ((lane_t < nb).astype(F32) + _dot(terms, selq_ref[...])).astype(BF16)

    def head(a, h):
        return a[:, h * HEAD_DIM:(h + 1) * HEAD_DIM]

    def interleave(dst_ref, zb, aug):
        for h in range(nh):
            dst_ref[:, 2 * h * HEAD_DIM:(2 * h + 1) * HEAD_DIM] = head(zb, h)
            dst_ref[:, (2 * h + 1) * HEAD_DIM:(2 * h + 2) * HEAD_DIM] = head(aug, h)

    def sample_rows(z, dst_ref, by_head):
        for tile in range(np_rows // tm, (np_rows + ns_rows - 1) // tm + 1):
            lo, hi = max(np_rows, tile * tm), min(np_rows + ns_rows, (tile + 1) * tm)

            @pl.when(i == tile)
            def _(tile=tile, lo=lo, hi=hi):
                rows = z[lo - tile * tm:hi - tile * tm, :]
                if by_head:
                    for h in range(nh):
                        dst_ref[pl.ds((lo - np_rows) * nh + h, hi - lo, stride=nh), :] = head(rows, h)
                else:
                    dst_ref[lo - np_rows:hi - np_rows, :] = rows

    def returned_rows(z, p_ref, t_ref, s_ref):
        @pl.when(i < n_full)
        def _():
            for h in range(nh):
                p_ref[pl.ds(h, tm, stride=nh), :] = head(z, h)

        @pl.when(i == n_full)
        def _():
            for h in range(nh):
                t_ref[pl.ds(h, n_tail, stride=nh), :] = head(z[:n_tail, :], h)

        sample_rows(z, s_ref, True)

    z = _dot(xn, w_ref[0])
    interleave(qa_ref, (z * (ATTN_SCALE * LOG2E)).astype(BF16), qaug)
    sample_rows(z, qs_ref, False)

    z = _dot(xn, w_ref[1])
    interleave(ka_ref, z.astype(BF16), kaug)
    returned_rows(z, kp_ref, kt_ref, ks_ref)

    z = _dot(xn, w_ref[2])
    returned_rows(z, vp_ref, vt_tail_ref, vs_ref)
    ones_row = (_iota2((VT_ROWS - HEAD_DIM, tm), 0) == 0).astype(BF16)
    for h in range(nh):
        vt_ref[h, :HEAD_DIM, :] = jnp.transpose(head(z, h)).astype(BF16)
        vt_ref[h, HEAD_DIM:, :] = ones_row

    uf_ref[...] = _dot(xn, w_ref[3])


def _in_proj_constants(tm, aw, rep):
    nb = N_BIAS_TERMS
    tril = (jnp.arange(tm)[None, :] <= jnp.arange(tm)[:, None]).astype(BF16)
    src = jnp.arange(nb * LANES)[:, None]
    dst = jnp.arange(aw)[None, :]
    head_src = src % LANES == (dst // HEAD_DIM) * rep
    selk = jnp.logical_and(head_src, dst % HEAD_DIM == src // LANES).astype(BF16)
    selq = jnp.logical_and(head_src, dst % HEAD_DIM == nb + src // LANES).astype(BF16)
    return tril, selk, selq


def _in_proj(x, g, w4, wf, bf, consts, kv_prev, *, layer, depth, np_rows, ns_rows, tm, tk):
    r, d = x.shape
    aw = w4.shape[-1]
    nh = aw // HEAD_DIM
    n_full, n_tail = divmod(np_rows, tm)
    assert n_tail > 0 and n_full > 0 and tk % tm == 0
    per_block = tk // tm
    n_alias = 0 if kv_prev is None else len(kv_prev)
    kernel_fn = functools.partial(_in_kernel, n_alias=n_alias, np_rows=np_rows, ns_rows=ns_rows)
    once = dict(pipeline_mode=pl.Buffered(1))
    const = lambda a: pl.BlockSpec(a.shape, lambda i: (0,) * a.ndim, **once)
    layered = lambda a: pl.BlockSpec((None,) + a.shape[1:], lambda i: (layer,) + (0,) * (a.ndim - 1), **once)
    prompt_spec = pl.BlockSpec((None, tm * nh, HEAD_DIM), lambda i: (layer, jnp.minimum(i, n_full - 1), 0))
    tail_spec = pl.BlockSpec((None, n_tail * nh, HEAD_DIM), lambda i: (layer, 0, 0))
    sample_spec = pl.BlockSpec((None, ns_rows * nh, HEAD_DIM), lambda i: (layer, 0, 0))
    prompt_shape = jax.ShapeDtypeStruct((depth, np_rows * nh, HEAD_DIM), F32)
    tail_shape = jax.ShapeDtypeStruct((depth, n_tail * nh, HEAD_DIM), F32)
    sample_shape = jax.ShapeDtypeStruct((depth, ns_rows * nh, HEAD_DIM), F32)
    return pl.pallas_call(
        kernel_fn,
        grid=(r // tm,),
        in_specs=[
            pl.BlockSpec((tm, d), lambda i: (i, 0)),
            layered(g), layered(w4), layered(wf), layered(bf), *[const(c) for c in consts],
        ] + [pl.BlockSpec(memory_space=pl.ANY)] * n_alias,
        out_specs=[
            pl.BlockSpec((ns_rows, aw), lambda i: (0, 0)),
            prompt_spec, prompt_spec, tail_spec, tail_spec, sample_spec, sample_spec,
            pl.BlockSpec((tm, aw), lambda i: (i, 0)),
            pl.BlockSpec((tm, 2 * aw), lambda i: (i, 0)),
            pl.BlockSpec((tm, 2 * aw), lambda i: (i, 0)),
            pl.BlockSpec((nh, None, VT_ROWS, tm), lambda i: (0, i // per_block, 0, i % per_block)),
            pl.BlockSpec((tm, LANES), lambda i: (i, 0)),
        ],
        out_shape=[
            jax.ShapeDtypeStruct((ns_rows, aw), F32),
            prompt_shape, prompt_shape, tail_shape, tail_shape, sample_shape, sample_shape,
            jax.ShapeDtypeStruct((r, aw), F32),
            jax.ShapeDtypeStruct((r, 2 * aw), BF16),
            jax.ShapeDtypeStruct((r, 2 * aw), BF16),
            jax.ShapeDtypeStruct((nh, r // tk, VT_ROWS, tk), BF16),
            jax.ShapeDtypeStruct((r, LANES), F32),
        ],
        scratch_shapes=[pltpu.VMEM((1, LANES), F32)],
        input_output_aliases={5 + len(consts) + n: 1 + n for n in range(n_alias)},
        compiler_params=_cparams("arbitrary"),
        name="in_proj",
    )(x, g, w4, wf, bf, *consts, *(kv_prev or ()))


def _attn_p_kernel(qa_ref, ka_ref, vt_ref, o_ref, sa_ref, sb_ref, acc_ref):
    qi = pl.program_id(1)
    tq = qa_ref.shape[0]
    tk = sa_ref.shape[0]
    n_diag = tq // tk
    n_off = qi * n_diag

    def produce(s_ref, kb):
        s = _dot_nt(ka_ref[pl.ds(pl.multiple_of(kb * tk, tk), tk), :], qa_ref[...])
        s_ref[...] = s
        return jnp.max(s, axis=0, keepdims=True)

    def consume(s_ref, m_blk, kb, m):
        m_new = jnp.maximum(m, m_blk)
        p = jnp.exp2(s_ref[...] - m_new)
        acc_ref[...] = jnp.exp2(m - m_new) * acc_ref[...] + _dot(vt_ref[kb], p.astype(BF16))
        return m_new

    acc_ref[...] = jnp.zeros_like(acc_ref)
    m_first = produce(sa_ref, 0)

    def pair(kb, carry):
        m_a, m = carry
        m_b = produce(sb_ref, kb + 1)
        m = consume(sa_ref, m_a, kb, m)
        m_a = produce(sa_ref, kb + 2)
        m = consume(sb_ref, m_b, kb + 1, m)
        return m_a, m

    init = (m_first, jnp.full((1, tq), NEG_INF, F32))
    carry = lax.fori_loop(0, n_off // 4, lambda i, c: pair(4 * i + 2, pair(4 * i, c)), init)
    done = (n_off // 4) * 4
    m_a, m = lax.fori_loop(0, (n_off % 4) // 2, lambda i, c: pair(done + 2 * i, c), carry)
    def produce_diag(s_ref, j):
        c0 = j * tk
        kb = n_off + j
        s_ref[:, :tq - c0] = _dot_nt(ka_ref[pl.ds(pl.multiple_of(kb * tk, tk), tk), :], qa_ref[c0:, :])

    def consume_diag(s_ref, j, m):
        c0 = j * tk
        w = tq - c0
        visible = _iota2((tk, w), 1) >= _iota2((tk, w), 0)
        s = jnp.where(visible, s_ref[:, :w], NEG_INF)
        m_old = m[:, c0:]
        m_new = jnp.maximum(m_old, jnp.max(s, axis=0, keepdims=True))
        p = jnp.exp2(s - m_new)
        acc_ref[:, c0:] = jnp.exp2(m_old - m_new) * acc_ref[:, c0:] + _dot(vt_ref[n_off + j], p.astype(BF16))
        return m_new if c0 == 0 else jnp.concatenate([m[:, :c0], m_new], axis=1)

    def tail(lead_block, m_lead, m):
        bufs = (sa_ref, sb_ref)
        has_lead = lead_block is not None
        for idx in range(n_diag + has_lead):
            cur = bufs[idx % 2]
            j = idx - has_lead
            if j + 1 < n_diag:
                produce_diag(bufs[(idx + 1) % 2], j + 1)
            if j < 0:
                m = consume(cur, m_lead, lead_block, m)
            else:
                m = consume_diag(cur, j, m)
        o_ref[...] = jnp.transpose(acc_ref[:HEAD_DIM, :] / acc_ref[HEAD_DIM:HEAD_DIM + 1, :])

    @pl.when(n_off % 2 == 0)
    def _():
        tail(None, None, m)

    @pl.when(n_off % 2 == 1)
    def _():
        tail(n_off - 1, m_a, m)


def _attn_prompt(qa, ka, vt, *, tq):
    nh, nkb, _, tk = vt.shape
    r = nkb * tk
    return pl.pallas_call(
        _attn_p_kernel,
        grid=(nh, r // tq),
        in_specs=[
            pl.BlockSpec((tq, 2 * HEAD_DIM), lambda h, i: (i, h)),
            pl.BlockSpec((r, 2 * HEAD_DIM), lambda h, i: (0, h)),
            pl.BlockSpec((None, nkb, VT_ROWS, tk), lambda h, i: (h, 0, 0, 0)),
        ],
        out_specs=pl.BlockSpec((tq, HEAD_DIM), lambda h, i: (i, h)),
        out_shape=jax.ShapeDtypeStruct((r, nh * HEAD_DIM), F32),
        scratch_shapes=[pltpu.VMEM((tk, tq), F32), pltpu.VMEM((tk, tq), F32), pltpu.VMEM((VT_ROWS, tq), F32)],
        compiler_params=_cparams("parallel", "arbitrary"),
        name="attn_p",
    )(qa, ka, vt)


def _attn_s_kernel(qbd_ref, ck_ref, cv_ref, cf_ref, kn_ref, vn_ref, fn_ref, attn_in_ref, o_ref,
                   m_ref, l_ref, acc_ref, suf_ref, cq_ref, *, rep, seg, nh):
    del attn_in_ref
    step = pl.program_id(1)
    s_new = fn_ref.shape[0]
    tk = cf_ref.shape[0]
    qbd = qbd_ref[...]

    def heads_side_by_side(ref, n):
        return jnp.concatenate([ref[pl.ds(h, n, stride=nh), :].astype(BF16) for h in range(nh)], axis=1)

    def update(t, v_bf):
        m_old = m_ref[...]
        m_new = jnp.maximum(m_old, jnp.max(t, axis=0, keepdims=True))
        p = jnp.exp(t - m_new)
        alpha = jnp.exp(m_old - m_new)
        l_ref[...] = alpha * l_ref[...] + jnp.sum(p, axis=0, keepdims=True)
        m_ref[...] = m_new
        alpha_col = jnp.transpose(jnp.broadcast_to(alpha, (LANES, LANES)))
        pv = _dot(jnp.transpose(p).astype(BF16), v_bf)
        for c in range(nh):
            sl = slice(c * HEAD_DIM, (c + 1) * HEAD_DIM)
            acc_ref[:, sl] = acc_ref[:, sl] * alpha_col + pv[:, sl]

    @pl.when(step == 0)
    def _():
        m_ref[...] = jnp.full_like(m_ref, NEG_INF)
        l_ref[...] = jnp.zeros_like(l_ref)
        acc_ref[...] = jnp.zeros_like(acc_ref)
        tril = (_iota2((s_new, s_new), 1) <= _iota2((s_new, s_new), 0)).astype(BF16)
        cn = _dot_exact_lhs01(tril, fn_ref[...])
        krow = _iota2((s_new, LANES), 0)
        qlane = _iota2((s_new, LANES), 1) % rep
        cq = jnp.sum(jnp.where(krow == qlane, cn, 0.0), axis=0, keepdims=True)
        cq_ref[...] = cq
        suf_ref[...] = jnp.zeros_like(suf_ref)
        st = _dot(heads_side_by_side(kn_ref, s_new), qbd)
        t = jnp.where(krow <= qlane, st + (cq - cn), NEG_INF)
        update(t, heads_side_by_side(vn_ref, s_new))

    triu = (_iota2((seg, seg), 1) > _iota2((seg, seg), 0)).astype(BF16)
    carry = suf_ref[...]
    sufs = [None] * (tk // seg)
    for sidx in reversed(range(tk // seg)):
        x = cf_ref[sidx * seg:(sidx + 1) * seg, :]
        sfx = _dot_exact_lhs01(triu, x) + carry
        sufs[sidx] = sfx
        carry = sfx[0:1, :] + x[0:1, :]
    suf_ref[...] = carry
    bias = jnp.concatenate(sufs, axis=0) + cq_ref[...]
    update(_dot(heads_side_by_side(ck_ref, tk), qbd) + bias, heads_side_by_side(cv_ref, tk))

    @pl.when(step == pl.num_programs(1) - 1)
    def _():
        l_col = jnp.transpose(jnp.broadcast_to(l_ref[...], (LANES, LANES)))
        for c in range(nh):
            rws = slice(c * rep, c * rep + s_new)
            sl = slice(c * HEAD_DIM, (c + 1) * HEAD_DIM)
            o_ref[:, sl] = acc_ref[rws, sl] / l_col[rws, :]


def _attn_sample(layer, qbd, cache_k, cache_v, cf, kf, vf, logf, attn, *, np_rows, tk, rep):
    depth, nb, past, nh, _ = cache_k.shape
    aw = nh * HEAD_DIM
    s_new = S5_CHUNK
    nkb = past // tk
    base = np_rows // s_new
    seg = min(tk, LANES)

    ck = cache_k.reshape(depth, nb, past * nh, HEAD_DIM)
    cv = cache_v.reshape(depth, nb, past * nh, HEAD_DIM)
    cache_spec = pl.BlockSpec((None, None, tk * nh, HEAD_DIM), lambda b, s: (layer, b, nkb - 1 - s, 0))
    return pl.pallas_call(
        functools.partial(_attn_s_kernel, rep=rep, seg=seg, nh=nh),
        grid=(nb, nkb),
        in_specs=[
            pl.BlockSpec((None, aw, LANES), lambda b, s: (b, 0, 0)),
            cache_spec,
            cache_spec,
            pl.BlockSpec((None, None, tk, LANES), lambda b, s: (layer, b, nkb - 1 - s, 0)),
            pl.BlockSpec((None, s_new * nh, HEAD_DIM), lambda b, s: (layer, b, 0)),
            pl.BlockSpec((None, s_new * nh, HEAD_DIM), lambda b, s: (layer, b, 0)),
            pl.BlockSpec((s_new, LANES), lambda b, s: (base + b, 0)),
            pl.BlockSpec(memory_space=pl.ANY),
        ],
        out_specs=pl.BlockSpec((s_new, aw), lambda b, s: (base + b, 0)),
        out_shape=jax.ShapeDtypeStruct(attn.shape, F32),
        scratch_shapes=[
            pltpu.VMEM((1, LANES), F32), pltpu.VMEM((1, LANES), F32), pltpu.VMEM((LANES, aw), F32),
            pltpu.VMEM((1, LANES), F32), pltpu.VMEM((1, LANES), F32),
        ],
        input_output_aliases={7: 0},
        compiler_params=_cparams("parallel", "arbitrary"),
        name="attn_s",
    )(qbd, ck, cv, cf, kf, vf, logf, attn)


def _s5_kernel(u_ref, kc_ref, wc_ref, vc_ref, a_ref, hinit_ref, d_ref, wglu2_ref, o_ref, hend_ref,
               kds2_ref, wfull_ref, vfull_ref, s_ref, hprev_ref, hc_ref, *, n_prompt_chunks):
    rt = pl.program_id(1)
    ct, sd = s_ref.shape
    half = sd // 2
    L = S5_CHUNK
    P = SSM_STATE

    @pl.when(rt == 0)
    def _():
        hc_ref[...] = jnp.zeros_like(hc_ref)
        r_, c_ = _iota2((2 * P, sd), 0), _iota2((2 * P, sd), 1)
        rep_w = jnp.logical_and(r_ // P == c_ // half, r_ % P == c_ % P).astype(BF16)
        r_, c_ = _iota2((L * LANES, sd), 0), _iota2((L * LANES, sd), 1)
        same = (r_ // SSM_CH) % GROUPS_PER_TILE == (c_ % half) // P
        wfull_ref[...] = jnp.where(same, _dot(wc_ref[...], rep_w), 0.0).astype(BF16)
        r_, c_ = _iota2((L * SSM_CH, L * LANES), 0), _iota2((L * SSM_CH, L * LANES), 1)
        rep_v = jnp.logical_and(r_ // SSM_CH == c_ // LANES, r_ % SSM_CH == c_ % SSM_CH).astype(BF16)
        r_, c_ = _iota2((sd, L * LANES), 0), _iota2((sd, L * LANES), 1)
        same = (r_ % half) // P == (c_ % LANES) // SSM_CH
        vfull_ref[...] = jnp.where(same, _dot(vc_ref[...], rep_v), 0.0).astype(BF16)
        r_, c_ = _iota2((SSM_CH, LANES), 0), _iota2((SSM_CH, LANES), 1)
        rep_k = (r_ == c_ % SSM_CH).astype(BF16)
        r_, c_ = _iota2((L * LANES, LANES), 0), _iota2((L * LANES, LANES), 1)
        same = (r_ // SSM_CH) % GROUPS_PER_TILE == c_ // SSM_CH
        kds = jnp.where(same, _dot(kc_ref[...], rep_k), 0.0).astype(BF16)
        kds2_ref[:, :LANES] = kds
        kds2_ref[:LANES, LANES:] = jnp.zeros((LANES, LANES), BF16)
        kds2_ref[LANES:, LANES:] = kds[:(L - 1) * LANES, :]

    def u_at(tau):
        return u_ref[pl.ds(tau, ct, stride=L), :]

    xr = jnp.concatenate([u_at(L - 1 - j).astype(BF16) for j in range(L)], axis=1)
    s_ref[...] = _dot(xr, wfull_ref[...])

    ar = a_ref[:, :half]
    ai = a_ref[:, half:]

    def scan(c, hcar):
        cg = rt * ct + c
        reset = jnp.logical_or(cg == 0, cg >= n_prompt_chunks)
        hp = jnp.where(reset, hinit_ref[pl.ds(c, 1), :], hcar)
        hprev_ref[pl.ds(c, 1), :] = hp
        s = s_ref[pl.ds(c, 1), :]
        hr = hp[:, :half]
        hi = hp[:, half:]
        hn = jnp.concatenate([ar * hr - ai * hi + s[:, :half], ar * hi + ai * hr + s[:, half:]], axis=1)
        hend_ref[pl.ds(c, 1), :] = hn
        return hn

    hc_ref[...] = lax.fori_loop(0, ct, scan, hc_ref[...], unroll=SCAN_UNROLL)

    ystate = _dot(hprev_ref[...].astype(BF16), vfull_ref[...])
    d2 = jnp.concatenate([d_ref[...], d_ref[...]], axis=1)
    wglu2 = wglu2_ref[...]
    for tau in range(0, L, 2):
        lag_rows = (tau + 2) * LANES
        yy = _dot(xr[:, (L - 2 - tau) * LANES:], kds2_ref[:lag_rows, :])
        yy = yy + jnp.concatenate([ystate[:, (tau + 1) * LANES:(tau + 2) * LANES],
                                   ystate[:, tau * LANES:(tau + 1) * LANES]], axis=1)
        yy = yy + d2 * jnp.concatenate([u_at(tau + 1), u_at(tau)], axis=1)
        yy = 0.5 * yy * (1.0 + jnp.tanh(math.sqrt(2.0 / math.pi) * (yy + 0.044715 * (yy * yy * yy))))
        out = yy * jax.nn.sigmoid(_dot(yy.astype(BF16), wglu2))
        o_ref[pl.ds(tau + 1, ct, stride=L), :] = out[:, :LANES]
        o_ref[pl.ds(tau, ct, stride=L), :] = out[:, LANES:]


def _s5(u, kc, wc, vc, a16, hinit, d, wglu2, layer, *, rows_tile, n_prompt_chunks):
    r, sw = u.shape
    no = sw // LANES
    L = S5_CHUNK
    ct = rows_tile // L
    nrt = r // rows_tile
    sd = a16.shape[-1]
    return pl.pallas_call(
        functools.partial(_s5_kernel, n_prompt_chunks=n_prompt_chunks),
        grid=(no, nrt),
        in_specs=[
            pl.BlockSpec((rows_tile, LANES), lambda o, t: (t, o)),
            pl.BlockSpec((None, None, L * LANES, SSM_CH), lambda o, t: (layer, o, 0, 0)),
            pl.BlockSpec((None, None, L * LANES, 2 * SSM_STATE), lambda o, t: (layer, o, 0, 0)),
            pl.BlockSpec((None, None, sd, L * SSM_CH), lambda o, t: (layer, o, 0, 0)),
            pl.BlockSpec((None, None, 1, sd), lambda o, t: (layer, o, 0, 0)),
            pl.BlockSpec((None, ct, sd), lambda o, t: (layer, t, o)),
            pl.BlockSpec((None, 1, LANES), lambda o, t: (layer, 0, o)),
            pl.BlockSpec((None, None, 2 * LANES, 2 * LANES), lambda o, t: (layer, o, 0, 0)),
        ],
        out_specs=[
            pl.BlockSpec((rows_tile, LANES), lambda o, t: (t, o)),
            pl.BlockSpec((ct, sd), lambda o, t: (t, o)),
        ],
        out_shape=[
            jax.ShapeDtypeStruct((r, sw), F32),
            jax.ShapeDtypeStruct((r // L, no * sd), F32),
        ],
        scratch_shapes=[pltpu.VMEM((L * LANES, 2 * LANES), BF16),
                        pltpu.VMEM((L * LANES, sd), BF16), pltpu.VMEM((sd, L * LANES), BF16),
                        pltpu.VMEM((ct, sd), F32), pltpu.VMEM((ct, sd), F32), pltpu.VMEM((1, sd), F32)],
        compiler_params=_cparams("parallel", "arbitrary"),
        name="s5",
    )(u, kc, wc, vc, a16, hinit, d, wglu2)


def _s5_operators(a_re, a_im, log_dt, b_re, b_im, c_re, c_im, w_glu):
    g, p = a_re.shape
    c = SSM_CH
    L = S5_CHUNK
    gt = GROUPS_PER_TILE
    no = g // gt
    hp = lax.Precision.HIGHEST
    cmul = lambda xr, xi, yr, yi: (xr * yr - xi * yi, xr * yi + xi * yr)
    dt = jnp.exp(log_dt)[:, None]
    mag = jnp.exp(a_re * dt)
    ar, ai = mag * jnp.cos(a_im * dt), mag * jnp.sin(a_im * dt)
    den = a_re * a_re + a_im * a_im
    fr = ((ar - 1.0) * a_re + ai * a_im) / den
    fi = (ai * a_re - (ar - 1.0) * a_im) / den
    bbr, bbi = cmul(fr[..., None], fi[..., None], b_re, b_im)
    pr, pi = [jnp.ones_like(ar)], [jnp.zeros_like(ar)]
    for _ in range(L):
        nr, ni = cmul(pr[-1], pi[-1], ar, ai)
        pr.append(nr)
        pi.append(ni)
    pr, pi = jnp.stack(pr), jnp.stack(pi)

    abr, abi = cmul(pr[:L, :, :, None], pi[:L, :, :, None], bbr[None], bbi[None])
    kd = (jnp.einsum('gop,dgpi->gdio', c_re, abr, precision=hp)
          - jnp.einsum('gop,dgpi->gdio', c_im, abi, precision=hp))
    eye = jnp.eye(gt, dtype=F32)
    kc = kd.reshape(no, gt, L, c, c).transpose(0, 2, 1, 3, 4).reshape(no, L * LANES, c)

    wc = jnp.stack([abr, abi], axis=0).reshape(2, L, no, gt, p, c)
    wc = wc.transpose(2, 1, 3, 5, 0, 4).reshape(no, L * LANES, 2 * p)

    zr, zi = cmul(c_re.transpose(0, 2, 1)[None], c_im.transpose(0, 2, 1)[None],
                  pr[1:, :, :, None], pi[1:, :, :, None])
    vc = jnp.stack([zr, -zi], axis=0).reshape(2, L, no, gt, p, c)
    vc = vc.transpose(2, 0, 3, 4, 1, 5).reshape(no, 2 * gt * p, L * c)

    a16 = jnp.concatenate([pr[L].reshape(no, 1, gt * p), pi[L].reshape(no, 1, gt * p)], axis=2)
    wg = w_glu.reshape(no, gt, c, c)
    wglu = (wg[:, :, :, None, :] * eye[None, :, None, :, None]).reshape(no, LANES, LANES)
    zero = jnp.zeros_like(wglu)
    wglu2 = jnp.concatenate([jnp.concatenate([wglu, zero], axis=2), jnp.concatenate([zero, wglu], axis=2)], axis=1)
    return kc.astype(BF16), wc.astype(BF16), vc.astype(BF16), a16, wglu2.astype(BF16)


def _out_kernel(x_ref, attn_ref, ssm_ref, ga_ref, gs_ref, gpost_ref, wa_ref, ws_ref, o_ref):
    an = _rms(attn_ref[...], ga_ref[...]).astype(BF16)
    sn = _rms(ssm_ref[...], gs_ref[...]).astype(BF16)
    mixed = _dot(an, wa_ref[...]) + _dot(sn, ws_ref[...])
    o_ref[...] = x_ref[...] + _rms(mixed, gpost_ref[...])


def _out_proj(x, attn, ssm, ga, gs, gpost, w_out, layer, *, tm):
    r, d = x.shape
    aw = attn.shape[1]
    sw = ssm.shape[1]
    assert aw == sw
    once = dict(pipeline_mode=pl.Buffered(1))
    return pl.pallas_call(
        _out_kernel,
        grid=(r // tm,),
        in_specs=[
            pl.BlockSpec((tm, d), lambda i: (i, 0)),
            pl.BlockSpec((tm, aw), lambda i: (i, 0)),
            pl.BlockSpec((tm, sw), lambda i: (i, 0)),
            pl.BlockSpec((1, aw), lambda i: (0, 0)),
            pl.BlockSpec((1, sw), lambda i: (0, 0)),
            pl.BlockSpec((1, d), lambda i: (0, 0)),
            pl.BlockSpec((None, aw, d), lambda i: (layer, 0, 0), **once),
            pl.BlockSpec((None, sw, d), lambda i: (layer, 1, 0), **once),
        ],
        out_specs=pl.BlockSpec((tm, d), lambda i: (i, 0)),
        out_shape=jax.ShapeDtypeStruct((r, d), F32),
        compiler_params=_cparams("parallel"),
        name="out_proj",
    )(x, attn, ssm, ga, gs, gpost, w_out, w_out)


def _tiles(r_min):
    tm = 512
    r = -(-r_min // tm) * tm
    s5_tiles = 4 if (r // 4) % (8 * S5_CHUNK) == 0 else 1
    tq = 3 * tm if r % (3 * tm) == 0 else tm
    tm_dense = 768 if r % 768 == 0 else tm
    return dict(r=r, tm=tm, tm_in=tm // 2, tm_dense=tm_dense, tq=tq, tf=512, s5_rows=r // s5_tiles, tk_cache=1024)


def _forward(x_prompt, x_sample, cache_k, cache_v, cache_logf, state_ssm_re, state_ssm_im, meta_tokens,
             ffn1_norm_pre, ffn1_norm_post, ffn1_w_gate, ffn1_w_up, ffn1_w_down,
             mix_norm_pre, mix_norm_post, w_in, b_forget,
             ssm_a_re, ssm_a_im, ssm_log_dt, ssm_b_re, ssm_b_im, ssm_c_re, ssm_c_im, ssm_d, ssm_w_glu,
             attn_out_norm, ssm_out_norm, w_out,
             ffn2_norm_pre, ffn2_norm_post, ffn2_w_gate, ffn2_w_up, ffn2_w_down, tiles=None):
    bsz, seq, d = x_prompt.shape
    nb, s_new, _ = x_sample.shape
    depth, _, past, nh, _ = cache_k.shape
    n_meta = meta_tokens.shape[0]
    aw = nh * HEAD_DIM
    g, p = ssm_a_re.shape[1:]
    L = S5_CHUNK
    assert bsz == 1 and s_new == L and n_meta % L == 0 and seq % L == 0 and p == SSM_STATE
    assert LANES % nh == 0 and s_new <= LANES // nh and g % GROUPS_PER_TILE == 0
    rep = LANES // nh
    no = g // GROUPS_PER_TILE
    sd = 2 * GROUPS_PER_TILE * p
    np_rows = n_meta + seq
    ns_rows = nb * s_new
    t = tiles or _tiles(np_rows + ns_rows)
    r = t["r"]
    npc = np_rows // L
    n_chunks = r // L

    x = jnp.concatenate([meta_tokens.astype(F32), x_prompt[0], x_sample.reshape(ns_rows, d),
                         jnp.zeros((r - np_rows - ns_rows, d), F32)], axis=0)
    cf = jnp.repeat(cache_logf, rep, axis=-1)

    row2 = lambda a: a.reshape(1, -1)
    ffn1_w = [w.astype(BF16) for w in (ffn1_w_gate, ffn1_w_up, ffn1_w_down)]
    ffn2_w = [w.astype(BF16) for w in (ffn2_w_gate, ffn2_w_up, ffn2_w_down)]
    w_out_b = w_out.astype(BF16)
    w4 = jnp.stack([w_in[:, :, :aw], w_in[:, :, aw:2 * aw], w_in[:, :, 2 * aw:3 * aw], w_in[:, :, 3 * aw + nh:]],
                   axis=1).astype(BF16)
    wf = jnp.repeat(w_in[:, :, 3 * aw:3 * aw + nh], rep, axis=2).astype(BF16)
    bf = jnp.repeat(b_forget, rep, axis=1).reshape(depth, 1, LANES)
    in_consts = _in_proj_constants(t["tm_in"], aw, rep)
    s5_ops = jax.vmap(_s5_operators)(ssm_a_re, ssm_a_im, ssm_log_dt, ssm_b_re, ssm_b_im, ssm_c_re, ssm_c_im, ssm_w_glu)
    h0 = jnp.concatenate([state_ssm_re.reshape(depth, nb, no, 1, sd // 2),
                          state_ssm_im.reshape(depth, nb, no, 1, sd // 2)], axis=3).reshape(depth, nb, no * sd)
    hinit = jnp.zeros((depth, n_chunks, no * sd), F32).at[:, npc:npc + nb].set(h0)
    outs = dict(logf=[], hend=[])
    kv = None
    for l in range(depth):
        x = _ffn(x, row2(ffn1_norm_pre[l]), row2(ffn1_norm_post[l]), *ffn1_w, l, tm=t["tm_dense"], tf=t["tf"])

        qs, *kv, uf, qa, ka, vt, logf = _in_proj(x, mix_norm_pre.reshape(depth, 1, d), w4, wf, bf, in_consts, kv,
                                                 layer=l, depth=depth, np_rows=np_rows, ns_rows=ns_rows,
                                                 tm=t["tm_in"], tk=t["tm"])

        attn = _attn_prompt(qa, ka, vt, tq=t["tq"])

        qs = (qs * ATTN_SCALE).astype(BF16).reshape(nb, s_new, nh, HEAD_DIM)
        qs = jnp.pad(qs.transpose(0, 2, 3, 1), ((0, 0), (0, 0), (0, 0), (0, rep - s_new)))
        qbd = (qs[:, :, :, None, :] * jnp.eye(nh, dtype=BF16)[None, :, None, :, None]).reshape(nb, aw, LANES)
        attn = _attn_sample(l, qbd, cache_k, cache_v, cf, kv[4], kv[5], logf, attn,
                            np_rows=np_rows, tk=t["tk_cache"], rep=rep)

        kc, wc, vc, a16, wglu2 = s5_ops
        ssm, hend = _s5(uf, kc, wc, vc, a16, hinit, ssm_d.reshape(depth, 1, -1), wglu2, l,
                        rows_tile=t["s5_rows"], n_prompt_chunks=npc)

        x = _out_proj(x, attn, ssm, row2(attn_out_norm[l]), row2(ssm_out_norm[l]), row2(mix_norm_post[l]),
                      w_out_b, l, tm=t["tm_dense"])

        x = _ffn(x, row2(ffn2_norm_pre[l]), row2(ffn2_norm_post[l]), *ffn2_w, l, tm=t["tm_dense"], tf=t["tf"])

        outs["logf"].append(logf[:, ::rep])
        outs["hend"].append(hend[npc - 1:npc + nb].reshape(1 + nb, no, 2, GROUPS_PER_TILE, p))

    sl_p = slice(0, np_rows)
    sl_s = slice(np_rows, np_rows + ns_rows)
    rows = lambda arrs, sl: jnp.stack([a[sl] for a in arrs])
    heads = lambda a, n: a.reshape(depth, -1, n, nh, HEAD_DIM)
    states = lambda a: a.reshape(depth, -1, g, p)
    h_all = jnp.stack(outs["hend"])
    tail_at = (0, (np_rows // t["tm_in"]) * t["tm_in"] * nh, 0)
    k_prompt = lax.dynamic_update_slice(kv[0], kv[2], tail_at)
    v_prompt = lax.dynamic_update_slice(kv[1], kv[3], tail_at)
    return (x[n_meta:np_rows][None], x[sl_s].reshape(nb, s_new, d),
            heads(k_prompt, np_rows), heads(v_prompt, np_rows),
            rows(outs["logf"], sl_p).reshape(depth, 1, np_rows, nh),
            states(h_all[:, :1, :, 0]), states(h_all[:, :1, :, 1]),
            heads(kv[4], s_new), heads(kv[5], s_new),
            rows(outs["logf"], sl_s).reshape(depth, nb, s_new, nh),
            states(h_all[:, 1:, :, 0]), states(h_all[:, 1:, :, 1]))


def kernel(x_prompt, x_sample, cache_k, cache_v, cache_logf, state_ssm_re, state_ssm_im, meta_tokens, ffn1_norm_pre, ffn1_norm_post, ffn1_w_gate, ffn1_w_up, ffn1_w_down, mix_norm_pre, mix_norm_post, w_in, b_forget, ssm_a_re, ssm_a_im, ssm_log_dt, ssm_b_re, ssm_b_im, ssm_c_re, ssm_c_im, ssm_d, ssm_w_glu, attn_out_norm, ssm_out_norm, w_out, ffn2_norm_pre, ffn2_norm_post, ffn2_w_gate, ffn2_w_up, ffn2_w_down):
    return _forward(x_prompt, x_sample, cache_k, cache_v, cache_logf, state_ssm_re, state_ssm_im, meta_tokens,
                    ffn1_norm_pre, ffn1_norm_post, ffn1_w_gate, ffn1_w_up, ffn1_w_down,
                    mix_norm_pre, mix_norm_post, w_in, b_forget,
                    ssm_a_re, ssm_a_im, ssm_log_dt, ssm_b_re, ssm_b_im, ssm_c_re, ssm_c_im, ssm_d, ssm_w_glu,
                    attn_out_norm, ssm_out_norm, w_out,
                    ffn2_norm_pre, ffn2_norm_post, ffn2_w_gate, ffn2_w_up, ffn2_w_down)
```

```python
import functools
import math

import jax
import jax.numpy as jnp
from jax import lax
from jax.experimental import pallas as pl
from jax.experimental.pallas import tpu as pltpu

F32 = jnp.float32
BF16 = jnp.bfloat16

HEAD_DIM = 128
SSM_CH = 16
SSM_STATE = 64
S5_CHUNK = 16
LANES = 128
GROUPS_PER_TILE = LANES // SSM_CH
MACARON_W = 0.5
NORM_EPS = 1e-6
ATTN_SCALE = HEAD_DIM ** -0.5
LOG2E = math.log2(math.e)
NEG_INF = -1e30
VMEM_LIMIT = 56 * 1024 * 1024
N_BIAS_TERMS = 3
SCAN_UNROLL = 4
VT_ROWS = HEAD_DIM + 16
ATTN_PAIRS_PER_TRIP = 2


def _cparams(*sem):
    return pltpu.CompilerParams(dimension_semantics=sem, vmem_limit_bytes=VMEM_LIMIT)


def _rms(x, g):
    return x * lax.rsqrt(jnp.mean(x * x, axis=-1, keepdims=True) + NORM_EPS) * g


def _dot(a, b):
    return jnp.dot(a, b, preferred_element_type=F32)


def _dot_nt(a, b):
    return lax.dot_general(a, b, (((1,), (1,)), ((), ())), preferred_element_type=F32)


def _split3(x):
    hi = x.astype(BF16)
    r1 = x - hi.astype(F32)
    mid = r1.astype(BF16)
    lo = (r1 - mid.astype(F32)).astype(BF16)
    return hi, mid, lo


def _dot_exact_lhs01(sel, x):
    hi, mid, lo = _split3(x)
    return _dot(sel, hi) + _dot(sel, mid) + _dot(sel, lo)


def _iota2(shape, axis):
    return lax.broadcasted_iota(jnp.int32, shape, axis)


def _ffn_kernel(x_ref, gpre_ref, gpost_ref, wg_ref, wu_ref, wd_ref, o_ref, xn_ref, acc_ref):
    j = pl.program_id(1)

    @pl.when(j == 0)
    def _():
        xn_ref[...] = _rms(x_ref[...], gpre_ref[...]).astype(BF16)
        acc_ref[...] = jnp.zeros_like(acc_ref)

    xn = xn_ref[...]
    g = _dot(xn, wg_ref[...])
    u = _dot(xn, wu_ref[...])
    h = (g * jax.nn.sigmoid(g) * u).astype(BF16)
    acc_ref[...] += _dot(h, wd_ref[...])

    @pl.when(j == pl.num_programs(1) - 1)
    def _():
        o_ref[...] = x_ref[...] + _rms(acc_ref[...], gpost_ref[...])


def _ffn(x, g_pre, g_post, wg, wu, wd, layer, *, tm, tf):
    r, d = x.shape
    f = wg.shape[2]
    return pl.pallas_call(
        _ffn_kernel,
        grid=(r // tm, f // tf),
        in_specs=[
            pl.BlockSpec((tm, d), lambda i, j: (i, 0)),
            pl.BlockSpec((1, d), lambda i, j: (0, 0)),
            pl.BlockSpec((1, d), lambda i, j: (0, 0)),
            pl.BlockSpec((None, d, tf), lambda i, j: (layer, 0, j)),
            pl.BlockSpec((None, d, tf), lambda i, j: (layer, 0, j)),
            pl.BlockSpec((None, tf, d), lambda i, j: (layer, j, 0)),
        ],
        out_specs=pl.BlockSpec((tm, d), lambda i, j: (i, 0)),
        out_shape=jax.ShapeDtypeStruct((r, d), F32),
        scratch_shapes=[pltpu.VMEM((tm, d), BF16), pltpu.VMEM((tm, d), F32)],
        compiler_params=_cparams("parallel", "arbitrary"),
        name="ffn",
    )(x, g_pre, g_post, wg, wu, wd)


def _in_kernel(x_ref, g_ref, w_ref, wf_ref, bf_ref, tril_ref, selk_ref, selq_ref, *refs,
               n_alias, np_rows, ns_rows, rep):
    (qs_ref, kp_ref, vp_ref, kt_ref, vt_tail_ref, ks_ref, vs_ref, uf_ref, qa_ref, ka_ref, vt_ref, logf_ref,
     logf_t_ref, carry_ref) = refs[n_alias:]
    i = pl.program_id(0)
    tm = x_ref.shape[0]
    aw = w_ref.shape[2]
    nh = aw // HEAD_DIM
    nb = N_BIAS_TERMS
    n_full, n_tail = divmod(np_rows, tm)

    xn = _rms(x_ref[...], g_ref[...]).astype(BF16)
    logf = jax.nn.log_sigmoid(_dot(xn, wf_ref[...]) + bf_ref[...])
    logf_ref[...] = logf
    logf_t = jnp.transpose(logf)
    logf_t_ref[...] = jnp.concatenate([logf_t[h * rep:h * rep + 1, :] for h in range(nh)], axis=0)

    @pl.when(i == 0)
    def _():
        carry_ref[...] = jnp.zeros_like(carry_ref)

    fcum = _dot_exact_lhs01(tril_ref[...], logf) + carry_ref[...]
    carry_ref[...] = fcum[tm - 1:tm, :]
    terms = jnp.concatenate(_split3(fcum * LOG2E), axis=1)
    lane_t = _iota2((tm, aw), 1) % HEAD_DIM
    kaug = (jnp.logical_and(lane_t >= nb, lane_t < 2 * nb).astype(F32) - _dot(terms, selk_ref[...])).astype(BF16)
    qaug = ((lane_t < nb).astype(F32) + _dot(terms, selq_ref[...])).astype(BF16)

    def head(a, h):
        return a[:, h * HEAD_DIM:(h + 1) * HEAD_DIM]

    def interleave(dst_ref, zb, aug):
        for h in range(nh):
            dst_ref[:, 2 * h * HEAD_DIM:(2 * h + 1) * HEAD_DIM] = head(zb, h)
            dst_ref[:, (2 * h + 1) * HEAD_DIM:(2 * h + 2) * HEAD_DIM] = head(aug, h)

    def sample_rows(z, dst_ref, by_head):
        for tile in range(np_rows // tm, (np_rows + ns_rows - 1) // tm + 1):
            lo, hi = max(np_rows, tile * tm), min(np_rows + ns_rows, (tile + 1) * tm)

            @pl.when(i == tile)
            def _(tile=tile, lo=lo, hi=hi):
                rows = z[lo - tile * tm:hi - tile * tm, :]
                if by_head:
                    for h in range(nh):
                        dst_ref[pl.ds((lo - np_rows) * nh + h, hi - lo, stride=nh), :] = head(rows, h)
                else:
                    dst_ref[lo - np_rows:hi - np_rows, :] = rows

    def returned_rows(z, p_ref, t_ref, s_ref):
        @pl.when(i < n_full)
        def _():
            for h in range(nh):
                p_ref[pl.ds(h, tm, stride=nh), :] = head(z, h)

        @pl.when(i == n_full)
        def _():
            for h in range(nh):
                t_ref[pl.ds(h, n_tail, stride=nh), :] = head(z[:n_tail, :], h)

        sample_rows(z, s_ref, True)

    z = _dot(xn, w_ref[0])
    interleave(qa_ref, (z * (ATTN_SCALE * LOG2E)).astype(BF16), qaug)
    sample_rows(z, qs_ref, False)

    z = _dot(xn, w_ref[1])
    interleave(ka_ref, z.astype(BF16), kaug)
    returned_rows(z, kp_ref, kt_ref, ks_ref)

    z = _dot(xn, w_ref[2])
    returned_rows(z, vp_ref, vt_tail_ref, vs_ref)
    ones_row = (_iota2((VT_ROWS - HEAD_DIM, tm), 0) == 0).astype(BF16)
    for h in range(nh):
        vt_ref[h, :HEAD_DIM, :] = jnp.transpose(head(z, h)).astype(BF16)
        vt_ref[h, HEAD_DIM:, :] = ones_row

    uf_ref[...] = _dot(xn, w_ref[3])


def _in_proj_constants(tm, aw, rep):
    nb = N_BIAS_TERMS
    tril = (jnp.arange(tm)[None, :] <= jnp.arange(tm)[:, None]).astype(BF16)
    src = jnp.arange(nb * LANES)[:, None]
    dst = jnp.arange(aw)[None, :]
    head_src = src % LANES == (dst // HEAD_DIM) * rep
    selk = jnp.logical_and(head_src, dst % HEAD_DIM == src // LANES).astype(BF16)
    selq = jnp.logical_and(head_src, dst % HEAD_DIM == nb + src // LANES).astype(BF16)
    return tril, selk, selq


def _in_proj(x, g, w4, wf, bf, consts, kv_prev, *, layer, depth, np_rows, ns_rows, tm, tk):
    r, d = x.shape
    aw = w4.shape[-1]
    nh = aw // HEAD_DIM
    n_full, n_tail = divmod(np_rows, tm)
    assert n_tail > 0 and n_full > 0 and tk % tm == 0
    per_block = tk // tm
    n_alias = 0 if kv_prev is None else len(kv_prev)
    kernel_fn = functools.partial(_in_kernel, n_alias=n_alias, np_rows=np_rows, ns_rows=ns_rows, rep=LANES // nh)
    once = dict(pipeline_mode=pl.Buffered(1))
    const = lambda a: pl.BlockSpec(a.shape, lambda i: (0,) * a.ndim, **once)
    layered = lambda a: pl.BlockSpec((None,) + a.shape[1:], lambda i: (layer,) + (0,) * (a.ndim - 1), **once)
    prompt_spec = pl.BlockSpec((None, tm * nh, HEAD_DIM), lambda i: (layer, jnp.minimum(i, n_full - 1), 0))
    tail_spec = pl.BlockSpec((None, n_tail * nh, HEAD_DIM), lambda i: (layer, 0, 0))
    sample_spec = pl.BlockSpec((None, ns_rows * nh, HEAD_DIM), lambda i: (layer, 0, 0))
    prompt_shape = jax.ShapeDtypeStruct((depth, np_rows * nh, HEAD_DIM), F32)
    tail_shape = jax.ShapeDtypeStruct((depth, n_tail * nh, HEAD_DIM), F32)
    sample_shape = jax.ShapeDtypeStruct((depth, ns_rows * nh, HEAD_DIM), F32)
    return pl.pallas_call(
        kernel_fn,
        grid=(r // tm,),
        in_specs=[
            pl.BlockSpec((tm, d), lambda i: (i, 0)),
            layered(g), layered(w4), layered(wf), layered(bf), *[const(c) for c in consts],
        ] + [pl.BlockSpec(memory_space=pl.ANY)] * n_alias,
        out_specs=[
            pl.BlockSpec((ns_rows, aw), lambda i: (0, 0)),
            prompt_spec, prompt_spec, tail_spec, tail_spec, sample_spec, sample_spec,
            pl.BlockSpec((tm, aw), lambda i: (i, 0)),
            pl.BlockSpec((tm, 2 * aw), lambda i: (i, 0)),
            pl.BlockSpec((tm, 2 * aw), lambda i: (i, 0)),
            pl.BlockSpec((nh, None, VT_ROWS, tm), lambda i: (0, i // per_block, 0, i % per_block)),
            pl.BlockSpec((tm, LANES), lambda i: (i, 0)),
            pl.BlockSpec((nh, tm), lambda i: (0, i)),
        ],
        out_shape=[
            jax.ShapeDtypeStruct((ns_rows, aw), F32),
            prompt_shape, prompt_shape, tail_shape, tail_shape, sample_shape, sample_shape,
            jax.ShapeDtypeStruct((r, aw), F32),
            jax.ShapeDtypeStruct((r, 2 * aw), BF16),
            jax.ShapeDtypeStruct((r, 2 * aw), BF16),
            jax.ShapeDtypeStruct((nh, r // tk, VT_ROWS, tk), BF16),
            jax.ShapeDtypeStruct((r, LANES), F32),
            jax.ShapeDtypeStruct((nh, r), F32),
        ],
        scratch_shapes=[pltpu.VMEM((1, LANES), F32)],
        input_output_aliases={5 + len(consts) + n: 1 + n for n in range(n_alias)},
        compiler_params=_cparams("arbitrary"),
        name="in_proj",
    )(x, g, w4, wf, bf, *consts, *(kv_prev or ()))


def _attn_p_kernel(qa_ref, ka_ref, vt_ref, o_ref, sa_ref, sb_ref, acc_ref):
    qi = pl.program_id(1)
    tq = qa_ref.shape[0]
    tk = sa_ref.shape[0]
    n_diag = tq // tk
    n_off = qi * n_diag

    def produce(s_ref, kb):
        s = _dot_nt(ka_ref[pl.ds(pl.multiple_of(kb * tk, tk), tk), :], qa_ref[...])
        s_ref[...] = s
        return jnp.max(s, axis=0, keepdims=True)

    def consume(s_ref, m_blk, kb, m):
        m_new = jnp.maximum(m, m_blk)
        p = jnp.exp2(s_ref[...] - m_new)
        acc_ref[...] = jnp.exp2(m - m_new) * acc_ref[...] + _dot(vt_ref[kb], p.astype(BF16))
        return m_new

    acc_ref[...] = jnp.zeros_like(acc_ref)
    m_first = produce(sa_ref, 0)

    def pair(kb, carry):
        m_a, m = carry
        m_b = produce(sb_ref, kb + 1)
        m = consume(sa_ref, m_a, kb, m)
        m_a = produce(sa_ref, kb + 2)
        m = consume(sb_ref, m_b, kb + 1, m)
        return m_a, m

    init = (m_first, jnp.full((1, tq), NEG_INF, F32))
    def pairs(first, count, carry):
        for n in range(count):
            carry = pair(first + 2 * n, carry)
        return carry

    per_trip = 2 * ATTN_PAIRS_PER_TRIP
    carry = lax.fori_loop(0, n_off // per_trip, lambda i, c: pairs(per_trip * i, ATTN_PAIRS_PER_TRIP, c), init)
    done = (n_off // per_trip) * per_trip
    m_a, m = lax.fori_loop(0, (n_off % per_trip) // 2, lambda i, c: pair(done + 2 * i, c), carry)
    def produce_diag(s_ref, j):
        c0 = j * tk
        kb = n_off + j
        s_ref[:, :tq - c0] = _dot_nt(ka_ref[pl.ds(pl.multiple_of(kb * tk, tk), tk), :], qa_ref[c0:, :])

    def consume_diag(s_ref, j, m):
        c0 = j * tk
        w = tq - c0
        visible = _iota2((tk, w), 1) >= _iota2((tk, w), 0)
        s = jnp.where(visible, s_ref[:, :w], NEG_INF)
        m_old = m[:, c0:]
        m_new = jnp.maximum(m_old, jnp.max(s, axis=0, keepdims=True))
        p = jnp.exp2(s - m_new)
        acc_ref[:, c0:] = jnp.exp2(m_old - m_new) * acc_ref[:, c0:] + _dot(vt_ref[n_off + j], p.astype(BF16))
        return m_new if c0 == 0 else jnp.concatenate([m[:, :c0], m_new], axis=1)

    def tail(lead_block, m_lead, m):
        bufs = (sa_ref, sb_ref)
        has_lead = lead_block is not None
        for idx in range(n_diag + has_lead):
            cur = bufs[idx % 2]
            j = idx - has_lead
            if j + 1 < n_diag:
                produce_diag(bufs[(idx + 1) % 2], j + 1)
            if j < 0:
                m = consume(cur, m_lead, lead_block, m)
            else:
                m = consume_diag(cur, j, m)
        o_ref[...] = jnp.transpose(acc_ref[:HEAD_DIM, :] / acc_ref[HEAD_DIM:HEAD_DIM + 1, :])

    @pl.when(n_off % 2 == 0)
    def _():
        tail(None, None, m)

    @pl.when(n_off % 2 == 1)
    def _():
        tail(n_off - 1, m_a, m)


def _attn_prompt(qa, ka, vt, *, tq):
    nh, nkb, _, tk = vt.shape
    r = nkb * tk
    return pl.pallas_call(
        _attn_p_kernel,
        grid=(nh, r // tq),
        in_specs=[
            pl.BlockSpec((tq, 2 * HEAD_DIM), lambda h, i: (i, h)),
            pl.BlockSpec((r, 2 * HEAD_DIM), lambda h, i: (0, h)),
            pl.BlockSpec((None, nkb, VT_ROWS, tk), lambda h, i: (h, 0, 0, 0)),
        ],
        out_specs=pl.BlockSpec((tq, HEAD_DIM), lambda h, i: (i, h)),
        out_shape=jax.ShapeDtypeStruct((r, nh * HEAD_DIM), F32),
        scratch_shapes=[pltpu.VMEM((tk, tq), F32), pltpu.VMEM((tk, tq), F32), pltpu.VMEM((VT_ROWS, tq), F32)],
        compiler_params=_cparams("parallel", "arbitrary"),
        name="attn_p",
    )(qa, ka, vt)


def _attn_s_kernel(qbd_ref, ck_ref, cv_ref, cf_ref, kn_ref, vn_ref, fn_ref, attn_in_ref, o_ref,
                   m_ref, l_ref, acc_ref, suf_ref, cq_ref, *, rep, seg, nh):
    del attn_in_ref
    step = pl.program_id(1)
    s_new = fn_ref.shape[0]
    tk = cf_ref.shape[1]
    qbd = qbd_ref[...]

    def heads_side_by_side(ref, n):
        return jnp.concatenate([ref[pl.ds(h, n, stride=nh), :].astype(BF16) for h in range(nh)], axis=1)

    def update(t, v_bf):
        m_old = m_ref[...]
        m_new = jnp.maximum(m_old, jnp.max(t, axis=0, keepdims=True))
        p = jnp.exp(t - m_new)
        alpha = jnp.exp(m_old - m_new)
        l_ref[...] = alpha * l_ref[...] + jnp.sum(p, axis=0, keepdims=True)
        m_ref[...] = m_new
        alpha_col = jnp.transpose(jnp.broadcast_to(alpha, (LANES, LANES)))
        pv = _dot(jnp.transpose(p).astype(BF16), v_bf)
        for c in range(nh):
            sl = slice(c * HEAD_DIM, (c + 1) * HEAD_DIM)
            acc_ref[:, sl] = acc_ref[:, sl] * alpha_col + pv[:, sl]

    @pl.when(step == 0)
    def _():
        m_ref[...] = jnp.full_like(m_ref, NEG_INF)
        l_ref[...] = jnp.zeros_like(l_ref)
        acc_ref[...] = jnp.zeros_like(acc_ref)
        tril = (_iota2((s_new, s_new), 1) <= _iota2((s_new, s_new), 0)).astype(BF16)
        cn = _dot_exact_lhs01(tril, fn_ref[...])
        krow = _iota2((s_new, LANES), 0)
        qlane = _iota2((s_new, LANES), 1) % rep
        cq = jnp.sum(jnp.where(krow == qlane, cn, 0.0), axis=0, keepdims=True)
        cq_ref[...] = cq
        suf_ref[...] = jnp.zeros_like(suf_ref)
        st = _dot(heads_side_by_side(kn_ref, s_new), qbd)
        t = jnp.where(krow <= qlane, st + (cq - cn), NEG_INF)
        update(t, heads_side_by_side(vn_ref, s_new))

    spread = (_iota2((LANES, nh), 0) // rep == _iota2((LANES, nh), 1)).astype(BF16)
    cf_blk = jnp.transpose(_dot_exact_lhs01(spread, cf_ref[...]))
    triu = (_iota2((seg, seg), 1) > _iota2((seg, seg), 0)).astype(BF16)
    carry = suf_ref[...]
    sufs = [None] * (tk // seg)
    for sidx in reversed(range(tk // seg)):
        x = cf_blk[sidx * seg:(sidx + 1) * seg, :]
        sfx = _dot_exact_lhs01(triu, x) + carry
        sufs[sidx] = sfx
        carry = sfx[0:1, :] + x[0:1, :]
    suf_ref[...] = carry
    bias = jnp.concatenate(sufs, axis=0) + cq_ref[...]
    update(_dot(heads_side_by_side(ck_ref, tk), qbd) + bias, heads_side_by_side(cv_ref, tk))

    @pl.when(step == pl.num_programs(1) - 1)
    def _():
        l_col = jnp.transpose(jnp.broadcast_to(l_ref[...], (LANES, LANES)))
        for c in range(nh):
            rws = slice(c * rep, c * rep + s_new)
            sl = slice(c * HEAD_DIM, (c + 1) * HEAD_DIM)
            o_ref[:, sl] = acc_ref[rws, sl] / l_col[rws, :]


def _attn_sample(layer, qbd, cache_k, cache_v, cf, kf, vf, logf, attn, *, np_rows, tk, rep):
    depth, nb, past, nh, _ = cache_k.shape
    aw = nh * HEAD_DIM
    s_new = S5_CHUNK
    nkb = past // tk
    base = np_rows // s_new
    seg = min(tk, LANES)

    ck = cache_k.reshape(depth, nb, past * nh, HEAD_DIM)
    cv = cache_v.reshape(depth, nb, past * nh, HEAD_DIM)
    cache_spec = pl.BlockSpec((None, None, tk * nh, HEAD_DIM), lambda b, s: (layer, b, nkb - 1 - s, 0))
    return pl.pallas_call(
        functools.partial(_attn_s_kernel, rep=rep, seg=seg, nh=nh),
        grid=(nb, nkb),
        in_specs=[
            pl.BlockSpec((None, aw, LANES), lambda b, s: (b, 0, 0)),
            cache_spec,
            cache_spec,
            pl.BlockSpec((None, None, nh, tk), lambda b, s: (layer, b, 0, nkb - 1 - s)),
            pl.BlockSpec((None, s_new * nh, HEAD_DIM), lambda b, s: (layer, b, 0)),
            pl.BlockSpec((None, s_new * nh, HEAD_DIM), lambda b, s: (layer, b, 0)),
            pl.BlockSpec((s_new, LANES), lambda b, s: (base + b, 0)),
            pl.BlockSpec(memory_space=pl.ANY),
        ],
        out_specs=pl.BlockSpec((s_new, aw), lambda b, s: (base + b, 0)),
        out_shape=jax.ShapeDtypeStruct(attn.shape, F32),
        scratch_shapes=[
            pltpu.VMEM((1, LANES), F32), pltpu.VMEM((1, LANES), F32), pltpu.VMEM((LANES, aw), F32),
            pltpu.VMEM((1, LANES), F32), pltpu.VMEM((1, LANES), F32),
        ],
        input_output_aliases={7: 0},
        compiler_params=_cparams("parallel", "arbitrary"),
        name="attn_s",
    )(qbd, ck, cv, cf, kf, vf, logf, attn)


def _s5_kernel(u_ref, kc_ref, wc_ref, vc_ref, a_ref, hinit_ref, d_ref, wglu2_ref, o_ref, hend_ref,
               kds2_ref, wfull_ref, vfull_ref, s_ref, hprev_ref, hc_ref, *, n_prompt_chunks):
    rt = pl.program_id(1)
    ct, sd = s_ref.shape
    half = sd // 2
    L = S5_CHUNK
    P = SSM_STATE

    @pl.when(rt == 0)
    def _():
        hc_ref[...] = jnp.zeros_like(hc_ref)
        r_, c_ = _iota2((2 * P, sd), 0), _iota2((2 * P, sd), 1)
        rep_w = jnp.logical_and(r_ // P == c_ // half, r_ % P == c_ % P).astype(BF16)
        r_, c_ = _iota2((L * LANES, sd), 0), _iota2((L * LANES, sd), 1)
        same = (r_ // SSM_CH) % GROUPS_PER_TILE == (c_ % half) // P
        wfull_ref[...] = jnp.where(same, _dot(wc_ref[...], rep_w), 0.0).astype(BF16)
        r_, c_ = _iota2((L * SSM_CH, L * LANES), 0), _iota2((L * SSM_CH, L * LANES), 1)
        rep_v = jnp.logical_and(r_ // SSM_CH == c_ // LANES, r_ % SSM_CH == c_ % SSM_CH).astype(BF16)
        r_, c_ = _iota2((sd, L * LANES), 0), _iota2((sd, L * LANES), 1)
        same = (r_ % half) // P == (c_ % LANES) // SSM_CH
        vfull_ref[...] = jnp.where(same, _dot(vc_ref[...], rep_v), 0.0).astype(BF16)
        r_, c_ = _iota2((SSM_CH, LANES), 0), _iota2((SSM_CH, LANES), 1)
        rep_k = (r_ == c_ % SSM_CH).astype(BF16)
        r_, c_ = _iota2((L * LANES, LANES), 0), _iota2((L * LANES, LANES), 1)
        same = (r_ // SSM_CH) % GROUPS_PER_TILE == c_ // SSM_CH
        kds = jnp.where(same, _dot(kc_ref[...], rep_k), 0.0).astype(BF16)
        kds2_ref[:, :LANES] = kds
        kds2_ref[:LANES, LANES:] = jnp.zeros((LANES, LANES), BF16)
        kds2_ref[LANES:, LANES:] = kds[:(L - 1) * LANES, :]

    def u_at(tau):
        return u_ref[pl.ds(tau, ct, stride=L), :]

    xr = jnp.concatenate([u_at(L - 1 - j).astype(BF16) for j in range(L)], axis=1)
    s_ref[...] = _dot(xr, wfull_ref[...])

    ar = a_ref[:, :half]
    ai = a_ref[:, half:]

    def scan(c, hcar):
        cg = rt * ct + c
        reset = jnp.logical_or(cg == 0, cg >= n_prompt_chunks)
        hp = jnp.where(reset, hinit_ref[pl.ds(c, 1), :], hcar)
        hprev_ref[pl.ds(c, 1), :] = hp
        s = s_ref[pl.ds(c, 1), :]
        hr = hp[:, :half]
        hi = hp[:, half:]
        hn = jnp.concatenate([ar * hr - ai * hi + s[:, :half], ar * hi + ai * hr + s[:, half:]], axis=1)
        hend_ref[pl.ds(c, 1), :] = hn
        return hn

    hc_ref[...] = lax.fori_loop(0, ct, scan, hc_ref[...], unroll=SCAN_UNROLL)

    ystate = _dot(hprev_ref[...].astype(BF16), vfull_ref[...])
    d2 = jnp.concatenate([d_ref[...], d_ref[...]], axis=1)
    wglu2 = wglu2_ref[...]
    for tau in range(0, L, 2):
        lag_rows = (tau + 2) * LANES
        yy = _dot(xr[:, (L - 2 - tau) * LANES:], kds2_ref[:lag_rows, :])
        yy = yy + jnp.concatenate([ystate[:, (tau + 1) * LANES:(tau + 2) * LANES],
                                   ystate[:, tau * LANES:(tau + 1) * LANES]], axis=1)
        yy = yy + d2 * jnp.concatenate([u_at(tau + 1), u_at(tau)], axis=1)
        yy = 0.5 * yy * (1.0 + jnp.tanh(math.sqrt(2.0 / math.pi) * (yy + 0.044715 * (yy * yy * yy))))
        out = yy * jax.nn.sigmoid(_dot(yy.astype(BF16), wglu2))
        o_ref[pl.ds(tau + 1, ct, stride=L), :] = out[:, :LANES]
        o_ref[pl.ds(tau, ct, stride=L), :] = out[:, LANES:]


def _s5(u, kc, wc, vc, a16, hinit, d, wglu2, layer, *, rows_tile, n_prompt_chunks):
    r, sw = u.shape
    no = sw // LANES
    L = S5_CHUNK
    ct = rows_tile // L
    nrt = r // rows_tile
    sd = a16.shape[-1]
    return pl.pallas_call(
        functools.partial(_s5_kernel, n_prompt_chunks=n_prompt_chunks),
        grid=(no, nrt),
        in_specs=[
            pl.BlockSpec((rows_tile, LANES), lambda o, t: (t, o)),
            pl.BlockSpec((None, None, L * LANES, SSM_CH), lambda o, t: (layer, o, 0, 0)),
            pl.BlockSpec((None, None, L * LANES, 2 * SSM_STATE), lambda o, t: (layer, o, 0, 0)),
            pl.BlockSpec((None, None, sd, L * SSM_CH), lambda o, t: (layer, o, 0, 0)),
            pl.BlockSpec((None, None, 1, sd), lambda o, t: (layer, o, 0, 0)),
            pl.BlockSpec((None, ct, sd), lambda o, t: (layer, t, o)),
            pl.BlockSpec((None, 1, LANES), lambda o, t: (layer, 0, o)),
            pl.BlockSpec((None, None, 2 * LANES, 2 * LANES), lambda o, t: (layer, o, 0, 0)),
        ],
        out_specs=[
            pl.BlockSpec((rows_tile, LANES), lambda o, t: (t, o)),
            pl.BlockSpec((ct, sd), lambda o, t: (t, o)),
        ],
        out_shape=[
            jax.ShapeDtypeStruct((r, sw), F32),
            jax.ShapeDtypeStruct((r // L, no * sd), F32),
        ],
        scratch_shapes=[pltpu.VMEM((L * LANES, 2 * LANES), BF16),
                        pltpu.VMEM((L * LANES, sd), BF16), pltpu.VMEM((sd, L * LANES), BF16),
                        pltpu.VMEM((ct, sd), F32), pltpu.VMEM((ct, sd), F32), pltpu.VMEM((1, sd), F32)],
        compiler_params=_cparams("parallel", "arbitrary"),
        name="s5",
    )(u, kc, wc, vc, a16, hinit, d, wglu2)


def _s5_operators(a_re, a_im, log_dt, b_re, b_im, c_re, c_im, w_glu):
    g, p = a_re.shape
    c = SSM_CH
    L = S5_CHUNK
    gt = GROUPS_PER_TILE
    no = g // gt
    hp = lax.Precision.HIGHEST
    cmul = lambda xr, xi, yr, yi: (xr * yr - xi * yi, xr * yi + xi * yr)
    dt = jnp.exp(log_dt)[:, None]
    mag = jnp.exp(a_re * dt)
    ar, ai = mag * jnp.cos(a_im * dt), mag * jnp.sin(a_im * dt)
    den = a_re * a_re + a_im * a_im
    fr = ((ar - 1.0) * a_re + ai * a_im) / den
    fi = (ai * a_re - (ar - 1.0) * a_im) / den
    bbr, bbi = cmul(fr[..., None], fi[..., None], b_re, b_im)
    pr, pi = [jnp.ones_like(ar)], [jnp.zeros_like(ar)]
    for _ in range(L):
        nr, ni = cmul(pr[-1], pi[-1], ar, ai)
        pr.append(nr)
        pi.append(ni)
    pr, pi = jnp.stack(pr), jnp.stack(pi)

    abr, abi = cmul(pr[:L, :, :, None], pi[:L, :, :, None], bbr[None], bbi[None])
    kd = (jnp.einsum('gop,dgpi->gdio', c_re, abr, precision=hp)
          - jnp.einsum('gop,dgpi->gdio', c_im, abi, precision=hp))
    eye = jnp.eye(gt, dtype=F32)
    kc = kd.reshape(no, gt, L, c, c).transpose(0, 2, 1, 3, 4).reshape(no, L * LANES, c)

    wc = jnp.stack([abr, abi], axis=0).reshape(2, L, no, gt, p, c)
    wc = wc.transpose(2, 1, 3, 5, 0, 4).reshape(no, L * LANES, 2 * p)

    zr, zi = cmul(c_re.transpose(0, 2, 1)[None], c_im.transpose(0, 2, 1)[None],
                  pr[1:, :, :, None], pi[1:, :, :, None])
    vc = jnp.stack([zr, -zi], axis=0).reshape(2, L, no, gt, p, c)
    vc = vc.transpose(2, 0, 3, 4, 1, 5).reshape(no, 2 * gt * p, L * c)

    a16 = jnp.concatenate([pr[L].reshape(no, 1, gt * p), pi[L].reshape(no, 1, gt * p)], axis=2)
    wg = w_glu.reshape(no, gt, c, c)
    wglu = (wg[:, :, :, None, :] * eye[None, :, None, :, None]).reshape(no, LANES, LANES)
    zero = jnp.zeros_like(wglu)
    wglu2 = jnp.concatenate([jnp.concatenate([wglu, zero], axis=2), jnp.concatenate([zero, wglu], axis=2)], axis=1)
    return kc.astype(BF16), wc.astype(BF16), vc.astype(BF16), a16, wglu2.astype(BF16)


def _out_kernel(x_ref, attn_ref, ssm_ref, ga_ref, gs_ref, gpost_ref, wa_ref, ws_ref, o_ref):
    an = _rms(attn_ref[...], ga_ref[...]).astype(BF16)
    sn = _rms(ssm_ref[...], gs_ref[...]).astype(BF16)
    mixed = _dot(an, wa_ref[...]) + _dot(sn, ws_ref[...])
    o_ref[...] = x_ref[...] + _rms(mixed, gpost_ref[...])


def _out_proj(x, attn, ssm, ga, gs, gpost, w_out, layer, *, tm):
    r, d = x.shape
    aw = attn.shape[1]
    sw = ssm.shape[1]
    assert aw == sw
    once = dict(pipeline_mode=pl.Buffered(1))
    return pl.pallas_call(
        _out_kernel,
        grid=(r // tm,),
        in_specs=[
            pl.BlockSpec((tm, d), lambda i: (i, 0)),
            pl.BlockSpec((tm, aw), lambda i: (i, 0)),
            pl.BlockSpec((tm, sw), lambda i: (i, 0)),
            pl.BlockSpec((1, aw), lambda i: (0, 0)),
            pl.BlockSpec((1, sw), lambda i: (0, 0)),
            pl.BlockSpec((1, d), lambda i: (0, 0)),
            pl.BlockSpec((None, aw, d), lambda i: (layer, 0, 0), **once),
            pl.BlockSpec((None, sw, d), lambda i: (layer, 1, 0), **once),
        ],
        out_specs=pl.BlockSpec((tm, d), lambda i: (i, 0)),
        out_shape=jax.ShapeDtypeStruct((r, d), F32),
        compiler_params=_cparams("parallel"),
        name="out_proj",
    )(x, attn, ssm, ga, gs, gpost, w_out, w_out)


def _tiles(r_min):
    tm = 512
    r = -(-r_min // tm) * tm
    s5_tiles = 4 if (r // 4) % (8 * S5_CHUNK) == 0 else 1
    tq = 3 * tm if r % (3 * tm) == 0 else tm
    tm_dense = 768 if r % 768 == 0 else tm
    return dict(r=r, tm=tm, tm_in=tm // 2, tm_dense=tm_dense, tq=tq, tf=512, s5_rows=r // s5_tiles, tk_cache=1024)


def _forward(x_prompt, x_sample, cache_k, cache_v, cache_logf, state_ssm_re, state_ssm_im, meta_tokens,
             ffn1_norm_pre, ffn1_norm_post, ffn1_w_gate, ffn1_w_up, ffn1_w_down,
             mix_norm_pre, mix_norm_post, w_in, b_forget,
             ssm_a_re, ssm_a_im, ssm_log_dt, ssm_b_re, ssm_b_im, ssm_c_re, ssm_c_im, ssm_d, ssm_w_glu,
             attn_out_norm, ssm_out_norm, w_out,
             ffn2_norm_pre, ffn2_norm_post, ffn2_w_gate, ffn2_w_up, ffn2_w_down, tiles=None):
    bsz, seq, d = x_prompt.shape
    nb, s_new, _ = x_sample.shape
    depth, _, past, nh, _ = cache_k.shape
    n_meta = meta_tokens.shape[0]
    aw = nh * HEAD_DIM
    g, p = ssm_a_re.shape[1:]
    L = S5_CHUNK
    assert bsz == 1 and s_new == L and n_meta % L == 0 and seq % L == 0 and p == SSM_STATE
    assert LANES % nh == 0 and s_new <= LANES // nh and g % GROUPS_PER_TILE == 0
    rep = LANES // nh
    no = g // GROUPS_PER_TILE
    sd = 2 * GROUPS_PER_TILE * p
    np_rows = n_meta + seq
    ns_rows = nb * s_new
    t = tiles or _tiles(np_rows + ns_rows)
    r = t["r"]
    npc = np_rows // L
    n_chunks = r // L

    x = jnp.concatenate([meta_tokens.astype(F32), x_prompt[0], x_sample.reshape(ns_rows, d),
                         jnp.zeros((r - np_rows - ns_rows, d), F32)], axis=0)
    cf = cache_logf.transpose(0, 1, 3, 2)

    row2 = lambda a: a.reshape(1, -1)
    ffn1_w = [w.astype(BF16) for w in (ffn1_w_gate, ffn1_w_up, ffn1_w_down)]
    ffn2_w = [w.astype(BF16) for w in (ffn2_w_gate, ffn2_w_up, ffn2_w_down)]
    w_out_b = w_out.astype(BF16)
    w4 = jnp.stack([w_in[:, :, :aw], w_in[:, :, aw:2 * aw], w_in[:, :, 2 * aw:3 * aw], w_in[:, :, 3 * aw + nh:]],
                   axis=1).astype(BF16)
    wf = jnp.repeat(w_in[:, :, 3 * aw:3 * aw + nh], rep, axis=2).astype(BF16)
    bf = jnp.repeat(b_forget, rep, axis=1).reshape(depth, 1, LANES)
    in_consts = _in_proj_constants(t["tm_in"], aw, rep)
    s5_ops = jax.vmap(_s5_operators)(ssm_a_re, ssm_a_im, ssm_log_dt, ssm_b_re, ssm_b_im, ssm_c_re, ssm_c_im, ssm_w_glu)
    h0 = jnp.concatenate([state_ssm_re.reshape(depth, nb, no, 1, sd // 2),
                          state_ssm_im.reshape(depth, nb, no, 1, sd // 2)], axis=3).reshape(depth, nb, no * sd)
    hinit = jnp.zeros((depth, n_chunks, no * sd), F32).at[:, npc:npc + nb].set(h0)
    outs = dict(logf=[], hend=[])
    kv = None
    for l in range(depth):
        x = _ffn(x, row2(ffn1_norm_pre[l]), row2(MACARON_W * ffn1_norm_post[l]), *ffn1_w, l,
                 tm=t["tm_dense"], tf=t["tf"])

        qs, *kv, uf, qa, ka, vt, logf, logf_t = _in_proj(x, mix_norm_pre.reshape(depth, 1, d), w4, wf, bf, in_consts, kv,
                                                 layer=l, depth=depth, np_rows=np_rows, ns_rows=ns_rows,
                                                 tm=t["tm_in"], tk=t["tm"])

        attn = _attn_prompt(qa, ka, vt, tq=t["tq"])

        qs = (qs * ATTN_SCALE).astype(BF16).reshape(nb, s_new, nh, HEAD_DIM)
        qs = jnp.pad(qs.transpose(0, 2, 3, 1), ((0, 0), (0, 0), (0, 0), (0, rep - s_new)))
        qbd = (qs[:, :, :, None, :] * jnp.eye(nh, dtype=BF16)[None, :, None, :, None]).reshape(nb, aw, LANES)
        attn = _attn_sample(l, qbd, cache_k, cache_v, cf, kv[4], kv[5], logf, attn,
                            np_rows=np_rows, tk=t["tk_cache"], rep=rep)

        kc, wc, vc, a16, wglu2 = s5_ops
        ssm, hend = _s5(uf, kc, wc, vc, a16, hinit, ssm_d.reshape(depth, 1, -1), wglu2, l,
                        rows_tile=t["s5_rows"], n_prompt_chunks=npc)

        x = _out_proj(x, attn, ssm, row2(attn_out_norm[l]), row2(ssm_out_norm[l]), row2(mix_norm_post[l]),
                      w_out_b, l, tm=t["tm_dense"])

        x = _ffn(x, row2(ffn2_norm_pre[l]), row2(MACARON_W * ffn2_norm_post[l]), *ffn2_w, l,
                 tm=t["tm_dense"], tf=t["tf"])

        outs["logf"].append(logf_t)
        outs["hend"].append(hend[npc - 1:npc + nb].reshape(1 + nb, no, 2, GROUPS_PER_TILE, p))

    sl_p = slice(0, np_rows)
    sl_s = slice(np_rows, np_rows + ns_rows)
    rows = lambda arrs, sl: jnp.stack([a[:, sl] for a in arrs]).transpose(0, 2, 1)
    heads = lambda a, n: a.reshape(depth, -1, n, nh, HEAD_DIM)
    states = lambda a: a.reshape(depth, -1, g, p)
    h_all = jnp.stack(outs["hend"])
    tail_at = (0, (np_rows // t["tm_in"]) * t["tm_in"] * nh, 0)
    k_prompt = lax.dynamic_update_slice(kv[0], kv[2], tail_at)
    v_prompt = lax.dynamic_update_slice(kv[1], kv[3], tail_at)
    return (x[n_meta:np_rows][None], x[sl_s].reshape(nb, s_new, d),
            heads(k_prompt, np_rows), heads(v_prompt, np_rows),
            rows(outs["logf"], sl_p).reshape(depth, 1, np_rows, nh),
            states(h_all[:, :1, :, 0]), states(h_all[:, :1, :, 1]),
            heads(kv[4], s_new), heads(kv[5], s_new),
            rows(outs["logf"], sl_s).reshape(depth, nb, s_new, nh),
            states(h_all[:, 1:, :, 0]), states(h_all[:, 1:, :, 1]))


def kernel(x_prompt, x_sample, cache_k, cache_v, cache_logf, state_ssm_re, state_ssm_im, meta_tokens, ffn1_norm_pre, ffn1_norm_post, ffn1_w_gate, ffn1_w_up, ffn1_w_down, mix_norm_pre, mix_norm_post, w_in, b_forget, ssm_a_re, ssm_a_im, ssm_log_dt, ssm_b_re, ssm_b_im, ssm_c_re, ssm_c_im, ssm_d, ssm_w_glu, attn_out_norm, ssm_out_norm, w_out, ffn2_norm_pre, ffn2_norm_post, ffn2_w_gate, ffn2_w_up, ffn2_w_down):
    return _forward(x_prompt, x_sample, cache_k, cache_v, cache_logf, state_ssm_re, state_ssm_im, meta_tokens,
                    ffn1_norm_pre, ffn1_norm_post, ffn1_w_gate, ffn1_w_up, ffn1_w_down,
                    mix_norm_pre, mix_norm_post, w_in, b_forget,
                    ssm_a_re, ssm_a_im, ssm_log_dt, ssm_b_re, ssm_b_im, ssm_c_re, ssm_c_im, ssm_d, ssm_w_glu,
                    attn_out_norm, ssm_out_norm, w_out,
                    ffn2_norm_pre, ffn2_norm_post, ffn2_w_gate, ffn2_w_up, ffn2_w_down)
```

```python
import functools
import math

import jax
import jax.numpy as jnp
from jax import lax
from jax.experimental import pallas as pl
from jax.experimental.pallas import tpu as pltpu

F32 = jnp.float32
BF16 = jnp.bfloat16

HEAD_DIM = 128
SSM_CH = 16
SSM_STATE = 64
S5_CHUNK = 16
LANES = 128
GROUPS_PER_TILE = LANES // SSM_CH
MACARON_W = 0.5
NORM_EPS = 1e-6
ATTN_SCALE = HEAD_DIM ** -0.5
LOG2E = math.log2(math.e)
NEG_INF = -1e30
VMEM_LIMIT = 56 * 1024 * 1024
N_BIAS_TERMS = 3
SCAN_UNROLL = 4
VT_ROWS = HEAD_DIM + 16
S5_TILES_PER_STEP = 1
ATTN_PAIRS_PER_TRIP = 2


def _cparams(*sem):
    return pltpu.CompilerParams(dimension_semantics=sem, vmem_limit_bytes=VMEM_LIMIT)


def _rms(x, g):
    return x * lax.rsqrt(jnp.mean(x * x, axis=-1, keepdims=True) + NORM_EPS) * g


def _dot(a, b):
    return jnp.dot(a, b, preferred_element_type=F32)


def _dot_nt(a, b):
    return lax.dot_general(a, b, (((1,), (1,)), ((), ())), preferred_element_type=F32)


def _split3(x):
    hi = x.astype(BF16)
    r1 = x - hi.astype(F32)
    mid = r1.astype(BF16)
    lo = (r1 - mid.astype(F32)).astype(BF16)
    return hi, mid, lo


def _dot_exact_lhs01(sel, x):
    hi, mid, lo = _split3(x)
    return _dot(sel, hi) + _dot(sel, mid) + _dot(sel, lo)


def _iota2(shape, axis):
    return lax.broadcasted_iota(jnp.int32, shape, axis)


def _ffn_kernel(x_ref, gpre_ref, gpost_ref, wg_ref, wu_ref, wd_ref, o_ref, xn_ref, acc_ref):
    j = pl.program_id(1)

    @pl.when(j == 0)
    def _():
        xn_ref[...] = _rms(x_ref[...], gpre_ref[...]).astype(BF16)
        acc_ref[...] = jnp.zeros_like(acc_ref)

    xn = xn_ref[...]
    g = _dot(xn, wg_ref[...])
    u = _dot(xn, wu_ref[...])
    h = (g * jax.nn.sigmoid(g) * u).astype(BF16)
    acc_ref[...] += _dot(h, wd_ref[...])

    @pl.when(j == pl.num_programs(1) - 1)
    def _():
        o_ref[...] = x_ref[...] + _rms(acc_ref[...], gpost_ref[...])


def _ffn(x, g_pre, g_post, wg, wu, wd, layer, *, tm, tf):
    r, d = x.shape
    f = wg.shape[2]
    return pl.pallas_call(
        _ffn_kernel,
        grid=(r // tm, f // tf),
        in_specs=[
            pl.BlockSpec((tm, d), lambda i, j: (i, 0)),
            pl.BlockSpec((1, d), lambda i, j: (0, 0)),
            pl.BlockSpec((1, d), lambda i, j: (0, 0)),
            pl.BlockSpec((None, d, tf), lambda i, j: (layer, 0, j)),
            pl.BlockSpec((None, d, tf), lambda i, j: (layer, 0, j)),
            pl.BlockSpec((None, tf, d), lambda i, j: (layer, j, 0)),
        ],
        out_specs=pl.BlockSpec((tm, d), lambda i, j: (i, 0)),
        out_shape=jax.ShapeDtypeStruct((r, d), F32),
        scratch_shapes=[pltpu.VMEM((tm, d), BF16), pltpu.VMEM((tm, d), F32)],
        compiler_params=_cparams("parallel", "arbitrary"),
        name="ffn",
    )(x, g_pre, g_post, wg, wu, wd)


def _in_kernel(x_ref, g_ref, w_ref, wf_ref, bf_ref, tril_ref, selk_ref, selq_ref, *refs,
               n_alias, np_rows, ns_rows, rep):
    (qs_ref, kp_ref, vp_ref, kt_ref, vt_tail_ref, ks_ref, vs_ref, uf_ref, qa_ref, ka_ref, vt_ref, logf_ref,
     logf_t_ref, carry_ref) = refs[n_alias:]
    i = pl.program_id(0)
    tm = x_ref.shape[0]
    aw = w_ref.shape[2]
    nh = aw // HEAD_DIM
    nb = N_BIAS_TERMS
    n_full, n_tail = divmod(np_rows, tm)

    xn = _rms(x_ref[...], g_ref[...]).astype(BF16)
    logf = jax.nn.log_sigmoid(_dot(xn, wf_ref[...]) + bf_ref[...])
    logf_ref[...] = logf
    logf_t = jnp.transpose(logf)
    logf_t_ref[...] = jnp.concatenate([logf_t[h * rep:h * rep + 1, :] for h in range(nh)], axis=0)

    @pl.when(i == 0)
    def _():
        carry_ref[...] = jnp.zeros_like(carry_ref)

    fcum = _dot_exact_lhs01(tril_ref[...], logf) + carry_ref[...]
    carry_ref[...] = fcum[tm - 1:tm, :]
    terms = jnp.concatenate(_split3(fcum * LOG2E), axis=1)
    lane_t = _iota2((tm, aw), 1) % HEAD_DIM
    kaug = (jnp.logical_and(lane_t >= nb, lane_t < 2 * nb).astype(F32) - _dot(terms, selk_ref[...])).astype(BF16)
    qaug = ((lane_t < nb).astype(F32) + _dot(terms, selq_ref[...])).astype(BF16)

    def head(a, h):
        return a[:, h * HEAD_DIM:(h + 1) * HEAD_DIM]

    def interleave(dst_ref, zb, aug):
        for h in range(nh):
            dst_ref[:, 2 * h * HEAD_DIM:(2 * h + 1) * HEAD_DIM] = head(zb, h)
            dst_ref[:, (2 * h + 1) * HEAD_DIM:(2 * h + 2) * HEAD_DIM] = head(aug, h)

    def sample_rows(z, dst_ref, by_head):
        for tile in range(np_rows // tm, (np_rows + ns_rows - 1) // tm + 1):
            lo, hi = max(np_rows, tile * tm), min(np_rows + ns_rows, (tile + 1) * tm)

            @pl.when(i == tile)
            def _(tile=tile, lo=lo, hi=hi):
                rows = z[lo - tile * tm:hi - tile * tm, :]
                if by_head:
                    for h in range(nh):
                        dst_ref[pl.ds((lo - np_rows) * nh + h, hi - lo, stride=nh), :] = head(rows, h)
                else:
                    dst_ref[lo - np_rows:hi - np_rows, :] = rows

    def returned_rows(z, p_ref, t_ref, s_ref):
        @pl.when(i < n_full)
        def _():
            for h in range(nh):
                p_ref[pl.ds(h, tm, stride=nh), :] = head(z, h)

        @pl.when(i == n_full)
        def _():
            for h in range(nh):
                t_ref[pl.ds(h, n_tail, stride=nh), :] = head(z[:n_tail, :], h)

        sample_rows(z, s_ref, True)

    z = _dot(xn, w_ref[0])
    interleave(qa_ref, (z * (ATTN_SCALE * LOG2E)).astype(BF16), qaug)
    sample_rows(z, qs_ref, False)

    z = _dot(xn, w_ref[1])
    interleave(ka_ref, z.astype(BF16), kaug)
    returned_rows(z, kp_ref, kt_ref, ks_ref)

    z = _dot(xn, w_ref[2])
    returned_rows(z, vp_ref, vt_tail_ref, vs_ref)
    ones_row = (_iota2((VT_ROWS - HEAD_DIM, tm), 0) == 0).astype(BF16)
    for h in range(nh):
        vt_ref[h, :HEAD_DIM, :] = jnp.transpose(head(z, h)).astype(BF16)
        vt_ref[h, HEAD_DIM:, :] = ones_row

    uf_ref[...] = _dot(xn, w_ref[3])


def _in_proj_constants(tm, aw, rep):
    nb = N_BIAS_TERMS
    tril = (jnp.arange(tm)[None, :] <= jnp.arange(tm)[:, None]).astype(BF16)
    src = jnp.arange(nb * LANES)[:, None]
    dst = jnp.arange(aw)[None, :]
    head_src = src % LANES == (dst // HEAD_DIM) * rep
    selk = jnp.logical_and(head_src, dst % HEAD_DIM == src // LANES).astype(BF16)
    selq = jnp.logical_and(head_src, dst % HEAD_DIM == nb + src // LANES).astype(BF16)
    return tril, selk, selq


def _in_proj(x, g, w4, wf, bf, consts, kv_prev, *, layer, depth, np_rows, ns_rows, tm, tk):
    r, d = x.shape
    aw = w4.shape[-1]
    nh = aw // HEAD_DIM
    n_full, n_tail = divmod(np_rows, tm)
    assert n_tail > 0 and n_full > 0 and tk % tm == 0
    per_block = tk // tm
    n_alias = 0 if kv_prev is None else len(kv_prev)
    kernel_fn = functools.partial(_in_kernel, n_alias=n_alias, np_rows=np_rows, ns_rows=ns_rows, rep=LANES // nh)
    once = dict(pipeline_mode=pl.Buffered(1))
    const = lambda a: pl.BlockSpec(a.shape, lambda i: (0,) * a.ndim, **once)
    layered = lambda a: pl.BlockSpec((None,) + a.shape[1:], lambda i: (layer,) + (0,) * (a.ndim - 1), **once)
    prompt_spec = pl.BlockSpec((None, tm * nh, HEAD_DIM), lambda i: (layer, jnp.minimum(i, n_full - 1), 0))
    tail_spec = pl.BlockSpec((None, n_tail * nh, HEAD_DIM), lambda i: (layer, 0, 0))
    sample_spec = pl.BlockSpec((None, ns_rows * nh, HEAD_DIM), lambda i: (layer, 0, 0))
    prompt_shape = jax.ShapeDtypeStruct((depth, np_rows * nh, HEAD_DIM), F32)
    tail_shape = jax.ShapeDtypeStruct((depth, n_tail * nh, HEAD_DIM), F32)
    sample_shape = jax.ShapeDtypeStruct((depth, ns_rows * nh, HEAD_DIM), F32)
    return pl.pallas_call(
        kernel_fn,
        grid=(r // tm,),
        in_specs=[
            pl.BlockSpec((tm, d), lambda i: (i, 0)),
            layered(g), layered(w4), layered(wf), layered(bf), *[const(c) for c in consts],
        ] + [pl.BlockSpec(memory_space=pl.ANY)] * n_alias,
        out_specs=[
            pl.BlockSpec((ns_rows, aw), lambda i: (0, 0)),
            prompt_spec, prompt_spec, tail_spec, tail_spec, sample_spec, sample_spec,
            pl.BlockSpec((tm, aw), lambda i: (i, 0)),
            pl.BlockSpec((tm, 2 * aw), lambda i: (i, 0)),
            pl.BlockSpec((tm, 2 * aw), lambda i: (i, 0)),
            pl.BlockSpec((nh, None, VT_ROWS, tm), lambda i: (0, i // per_block, 0, i % per_block)),
            pl.BlockSpec((tm, LANES), lambda i: (i, 0)),
            pl.BlockSpec((nh, tm), lambda i: (0, i)),
        ],
        out_shape=[
            jax.ShapeDtypeStruct((ns_rows, aw), F32),
            prompt_shape, prompt_shape, tail_shape, tail_shape, sample_shape, sample_shape,
            jax.ShapeDtypeStruct((r, aw), F32),
            jax.ShapeDtypeStruct((r, 2 * aw), BF16),
            jax.ShapeDtypeStruct((r, 2 * aw), BF16),
            jax.ShapeDtypeStruct((nh, r // tk, VT_ROWS, tk), BF16),
            jax.ShapeDtypeStruct((r, LANES), F32),
            jax.ShapeDtypeStruct((nh, r), F32),
        ],
        scratch_shapes=[pltpu.VMEM((1, LANES), F32)],
        input_output_aliases={5 + len(consts) + n: 1 + n for n in range(n_alias)},
        compiler_params=_cparams("arbitrary"),
        name="in_proj",
    )(x, g, w4, wf, bf, *consts, *(kv_prev or ()))


def _attn_p_kernel(qa_ref, ka_ref, vt_ref, o_ref, sa_ref, sb_ref, acc_ref):
    qi = pl.program_id(1)
    tq = qa_ref.shape[0]
    tk = sa_ref.shape[0]
    n_diag = tq // tk
    n_off = qi * n_diag

    def produce(s_ref, kb):
        s = _dot_nt(ka_ref[pl.ds(pl.multiple_of(kb * tk, tk), tk), :], qa_ref[...])
        s_ref[...] = s
        return jnp.max(s, axis=0, keepdims=True)

    def consume(s_ref, m_blk, kb, m):
        m_new = jnp.maximum(m, m_blk)
        p = jnp.exp2(s_ref[...] - m_new)
        acc_ref[...] = jnp.exp2(m - m_new) * acc_ref[...] + _dot(vt_ref[kb], p.astype(BF16))
        return m_new

    acc_ref[...] = jnp.zeros_like(acc_ref)
    m_first = produce(sa_ref, 0)

    def pair(kb, carry):
        m_a, m = carry
        m_b = produce(sb_ref, kb + 1)
        m = consume(sa_ref, m_a, kb, m)
        m_a = produce(sa_ref, kb + 2)
        m = consume(sb_ref, m_b, kb + 1, m)
        return m_a, m

    init = (m_first, jnp.full((1, tq), NEG_INF, F32))
    def pairs(first, count, carry):
        for n in range(count):
            carry = pair(first + 2 * n, carry)
        return carry

    per_trip = 2 * ATTN_PAIRS_PER_TRIP
    carry = lax.fori_loop(0, n_off // per_trip, lambda i, c: pairs(per_trip * i, ATTN_PAIRS_PER_TRIP, c), init)
    done = (n_off // per_trip) * per_trip
    m_a, m = lax.fori_loop(0, (n_off % per_trip) // 2, lambda i, c: pair(done + 2 * i, c), carry)
    def produce_diag(s_ref, j):
        c0 = j * tk
        kb = n_off + j
        s_ref[:, :tq - c0] = _dot_nt(ka_ref[pl.ds(pl.multiple_of(kb * tk, tk), tk), :], qa_ref[c0:, :])

    def consume_diag(s_ref, j, m):
        c0 = j * tk
        w = tq - c0
        visible = _iota2((tk, w), 1) >= _iota2((tk, w), 0)
        s = jnp.where(visible, s_ref[:, :w], NEG_INF)
        m_old = m[:, c0:]
        m_new = jnp.maximum(m_old, jnp.max(s, axis=0, keepdims=True))
        p = jnp.exp2(s - m_new)
        acc_ref[:, c0:] = jnp.exp2(m_old - m_new) * acc_ref[:, c0:] + _dot(vt_ref[n_off + j], p.astype(BF16))
        return m_new if c0 == 0 else jnp.concatenate([m[:, :c0], m_new], axis=1)

    def tail(lead_block, m_lead, m):
        bufs = (sa_ref, sb_ref)
        has_lead = lead_block is not None
        for idx in range(n_diag + has_lead):
            cur = bufs[idx % 2]
            j = idx - has_lead
            if j + 1 < n_diag:
                produce_diag(bufs[(idx + 1) % 2], j + 1)
            if j < 0:
                m = consume(cur, m_lead, lead_block, m)
            else:
                m = consume_diag(cur, j, m)
        o_ref[...] = jnp.transpose(acc_ref[:HEAD_DIM, :] / acc_ref[HEAD_DIM:HEAD_DIM + 1, :])

    @pl.when(n_off % 2 == 0)
    def _():
        tail(None, None, m)

    @pl.when(n_off % 2 == 1)
    def _():
        tail(n_off - 1, m_a, m)


def _attn_prompt(qa, ka, vt, *, tq):
    nh, nkb, _, tk = vt.shape
    r = nkb * tk
    return pl.pallas_call(
        _attn_p_kernel,
        grid=(nh, r // tq),
        in_specs=[
            pl.BlockSpec((tq, 2 * HEAD_DIM), lambda h, i: (i, h)),
            pl.BlockSpec((r, 2 * HEAD_DIM), lambda h, i: (0, h)),
            pl.BlockSpec((None, nkb, VT_ROWS, tk), lambda h, i: (h, 0, 0, 0)),
        ],
        out_specs=pl.BlockSpec((tq, HEAD_DIM), lambda h, i: (i, h)),
        out_shape=jax.ShapeDtypeStruct((r, nh * HEAD_DIM), F32),
        scratch_shapes=[pltpu.VMEM((tk, tq), F32), pltpu.VMEM((tk, tq), F32), pltpu.VMEM((VT_ROWS, tq), F32)],
        compiler_params=_cparams("parallel", "arbitrary"),
        name="attn_p",
    )(qa, ka, vt)


def _attn_s_kernel(qbd_ref, ck_ref, cv_ref, cf_ref, kn_ref, vn_ref, fn_ref, attn_in_ref, o_ref,
                   m_ref, l_ref, acc_ref, suf_ref, cq_ref, *, rep, seg, nh):
    del attn_in_ref
    step = pl.program_id(1)
    s_new = fn_ref.shape[0]
    tk = cf_ref.shape[1]
    qbd = qbd_ref[...]

    def heads_side_by_side(ref, n):
        return jnp.concatenate([ref[pl.ds(h, n, stride=nh), :].astype(BF16) for h in range(nh)], axis=1)

    def update(t, v_bf):
        m_old = m_ref[...]
        m_new = jnp.maximum(m_old, jnp.max(t, axis=0, keepdims=True))
        p = jnp.exp(t - m_new)
        alpha = jnp.exp(m_old - m_new)
        l_ref[...] = alpha * l_ref[...] + jnp.sum(p, axis=0, keepdims=True)
        m_ref[...] = m_new
        alpha_col = jnp.transpose(jnp.broadcast_to(alpha, (LANES, LANES)))
        pv = _dot(jnp.transpose(p).astype(BF16), v_bf)
        for c in range(nh):
            sl = slice(c * HEAD_DIM, (c + 1) * HEAD_DIM)
            acc_ref[:, sl] = acc_ref[:, sl] * alpha_col + pv[:, sl]

    @pl.when(step == 0)
    def _():
        m_ref[...] = jnp.full_like(m_ref, NEG_INF)
        l_ref[...] = jnp.zeros_like(l_ref)
        acc_ref[...] = jnp.zeros_like(acc_ref)
        tril = (_iota2((s_new, s_new), 1) <= _iota2((s_new, s_new), 0)).astype(BF16)
        cn = _dot_exact_lhs01(tril, fn_ref[...])
        krow = _iota2((s_new, LANES), 0)
        qlane = _iota2((s_new, LANES), 1) % rep
        cq = jnp.sum(jnp.where(krow == qlane, cn, 0.0), axis=0, keepdims=True)
        cq_ref[...] = cq
        suf_ref[...] = jnp.zeros_like(suf_ref)
        st = _dot(heads_side_by_side(kn_ref, s_new), qbd)
        t = jnp.where(krow <= qlane, st + (cq - cn), NEG_INF)
        update(t, heads_side_by_side(vn_ref, s_new))

    spread = (_iota2((LANES, nh), 0) // rep == _iota2((LANES, nh), 1)).astype(BF16)
    cf_blk = jnp.transpose(_dot_exact_lhs01(spread, cf_ref[...]))
    triu = (_iota2((seg, seg), 1) > _iota2((seg, seg), 0)).astype(BF16)
    carry = suf_ref[...]
    sufs = [None] * (tk // seg)
    for sidx in reversed(range(tk // seg)):
        x = cf_blk[sidx * seg:(sidx + 1) * seg, :]
        sfx = _dot_exact_lhs01(triu, x) + carry
        sufs[sidx] = sfx
        carry = sfx[0:1, :] + x[0:1, :]
    suf_ref[...] = carry
    bias = jnp.concatenate(sufs, axis=0) + cq_ref[...]
    update(_dot(heads_side_by_side(ck_ref, tk), qbd) + bias, heads_side_by_side(cv_ref, tk))

    @pl.when(step == pl.num_programs(1) - 1)
    def _():
        l_col = jnp.transpose(jnp.broadcast_to(l_ref[...], (LANES, LANES)))
        for c in range(nh):
            rws = slice(c * rep, c * rep + s_new)
            sl = slice(c * HEAD_DIM, (c + 1) * HEAD_DIM)
            o_ref[:, sl] = acc_ref[rws, sl] / l_col[rws, :]


def _attn_sample(layer, qbd, cache_k, cache_v, cf, kf, vf, logf, attn, *, np_rows, tk, rep):
    depth, nb, past, nh, _ = cache_k.shape
    aw = nh * HEAD_DIM
    s_new = S5_CHUNK
    nkb = past // tk
    base = np_rows // s_new
    seg = min(tk, LANES)

    ck = cache_k.reshape(depth, nb, past * nh, HEAD_DIM)
    cv = cache_v.reshape(depth, nb, past * nh, HEAD_DIM)
    cache_spec = pl.BlockSpec((None, None, tk * nh, HEAD_DIM), lambda b, s: (layer, b, nkb - 1 - s, 0))
    return pl.pallas_call(
        functools.partial(_attn_s_kernel, rep=rep, seg=seg, nh=nh),
        grid=(nb, nkb),
        in_specs=[
            pl.BlockSpec((None, aw, LANES), lambda b, s: (b, 0, 0)),
            cache_spec,
            cache_spec,
            pl.BlockSpec((None, None, nh, tk), lambda b, s: (layer, b, 0, nkb - 1 - s)),
            pl.BlockSpec((None, s_new * nh, HEAD_DIM), lambda b, s: (layer, b, 0)),
            pl.BlockSpec((None, s_new * nh, HEAD_DIM), lambda b, s: (layer, b, 0)),
            pl.BlockSpec((s_new, LANES), lambda b, s: (base + b, 0)),
            pl.BlockSpec(memory_space=pl.ANY),
        ],
        out_specs=pl.BlockSpec((s_new, aw), lambda b, s: (base + b, 0)),
        out_shape=jax.ShapeDtypeStruct(attn.shape, F32),
        scratch_shapes=[
            pltpu.VMEM((1, LANES), F32), pltpu.VMEM((1, LANES), F32), pltpu.VMEM((LANES, aw), F32),
            pltpu.VMEM((1, LANES), F32), pltpu.VMEM((1, LANES), F32),
        ],
        input_output_aliases={7: 0},
        compiler_params=_cparams("parallel", "arbitrary"),
        name="attn_s",
    )(qbd, ck, cv, cf, kf, vf, logf, attn)


def _s5_kernel(u_ref, kc_ref, wc_ref, vc_ref, a_ref, hinit_ref, d_ref, wglu2_ref, o_ref, hend_ref,
               kds2_ref, wfull_ref, vfull_ref, s_ref, hprev_ref, hc_ref, *staging, n_prompt_chunks):
    rt = pl.program_id(1)
    nt = kc_ref.shape[0]
    ct = s_ref.shape[0]
    sd = s_ref.shape[1] // nt
    half = sd // 2
    L = S5_CHUNK
    P = SSM_STATE

    def lanes(k):
        return slice(k * LANES, (k + 1) * LANES)

    def state(k):
        return slice(k * sd, (k + 1) * sd)

    @pl.when(rt == 0)
    def _():
        hc_ref[...] = jnp.zeros_like(hc_ref)
        r_, c_ = _iota2((2 * P, sd), 0), _iota2((2 * P, sd), 1)
        rep_w = jnp.logical_and(r_ // P == c_ // half, r_ % P == c_ % P).astype(BF16)
        r_, c_ = _iota2((L * LANES, sd), 0), _iota2((L * LANES, sd), 1)
        same_w = (r_ // SSM_CH) % GROUPS_PER_TILE == (c_ % half) // P
        r_, c_ = _iota2((L * SSM_CH, L * LANES), 0), _iota2((L * SSM_CH, L * LANES), 1)
        rep_v = jnp.logical_and(r_ // SSM_CH == c_ // LANES, r_ % SSM_CH == c_ % SSM_CH).astype(BF16)
        r_, c_ = _iota2((sd, L * LANES), 0), _iota2((sd, L * LANES), 1)
        same_v = (r_ % half) // P == (c_ % LANES) // SSM_CH
        r_, c_ = _iota2((SSM_CH, LANES), 0), _iota2((SSM_CH, LANES), 1)
        rep_k = (r_ == c_ % SSM_CH).astype(BF16)
        r_, c_ = _iota2((L * LANES, LANES), 0), _iota2((L * LANES, LANES), 1)
        same_k = (r_ // SSM_CH) % GROUPS_PER_TILE == c_ // SSM_CH
        for k in range(nt):
            wfull_ref[k] = jnp.where(same_w, _dot(wc_ref[k], rep_w), 0.0).astype(BF16)
            vfull_ref[k] = jnp.where(same_v, _dot(vc_ref[k], rep_v), 0.0).astype(BF16)
            kds = jnp.where(same_k, _dot(kc_ref[k], rep_k), 0.0).astype(BF16)
            kds2_ref[k, :, :LANES] = kds
            kds2_ref[k, :LANES, LANES:] = jnp.zeros((LANES, LANES), BF16)
            kds2_ref[k, LANES:, LANES:] = kds[:(L - 1) * LANES, :]

    staged = nt > 1
    if staged:
        ubuf_ref, obuf_ref = staging
        for k in range(nt):
            ubuf_ref[k] = u_ref[:, lanes(k)]

    def u_at(tau, k):
        rows = pl.ds(tau, ct, stride=L)
        return ubuf_ref[k, rows, :] if staged else u_ref[rows, :]

    def put(tau, k, val):
        rows = pl.ds(tau, ct, stride=L)
        if staged:
            obuf_ref[k, rows, :] = val
        else:
            o_ref[rows, :] = val

    xr = [jnp.concatenate([u_at(L - 1 - j, k).astype(BF16) for j in range(L)], axis=1) for k in range(nt)]
    for k in range(nt):
        s_ref[:, state(k)] = _dot(xr[k], wfull_ref[k])

    def scan(c, hcar):
        cg = rt * ct + c
        reset = jnp.logical_or(cg == 0, cg >= n_prompt_chunks)
        hp = jnp.where(reset, hinit_ref[pl.ds(c, 1), :], hcar)
        hprev_ref[pl.ds(c, 1), :] = hp
        s = s_ref[pl.ds(c, 1), :]
        parts = []
        for k in range(nt):
            ar, ai = a_ref[k][:, :half], a_ref[k][:, half:]
            re, im = slice(k * sd, k * sd + half), slice(k * sd + half, (k + 1) * sd)
            parts += [ar * hp[:, re] - ai * hp[:, im] + s[:, re], ar * hp[:, im] + ai * hp[:, re] + s[:, im]]
        hn = jnp.concatenate(parts, axis=1)
        hend_ref[pl.ds(c, 1), :] = hn
        return hn

    hc_ref[...] = lax.fori_loop(0, ct, scan, hc_ref[...], unroll=SCAN_UNROLL)

    for k in range(nt):
        ystate = _dot(hprev_ref[:, state(k)].astype(BF16), vfull_ref[k])
        d2 = jnp.concatenate([d_ref[:, lanes(k)], d_ref[:, lanes(k)]], axis=1)
        wglu2 = wglu2_ref[k]
        for tau in range(0, L, 2):
            lag_rows = (tau + 2) * LANES
            yy = _dot(xr[k][:, (L - 2 - tau) * LANES:], kds2_ref[k, :lag_rows, :])
            yy = yy + jnp.concatenate([ystate[:, (tau + 1) * LANES:(tau + 2) * LANES],
                                       ystate[:, tau * LANES:(tau + 1) * LANES]], axis=1)
            yy = yy + d2 * jnp.concatenate([u_at(tau + 1, k), u_at(tau, k)], axis=1)
            yy = 0.5 * yy * (1.0 + jnp.tanh(math.sqrt(2.0 / math.pi) * (yy + 0.044715 * (yy * yy * yy))))
            out = yy * jax.nn.sigmoid(_dot(yy.astype(BF16), wglu2))
            put(tau + 1, k, out[:, :LANES])
            put(tau, k, out[:, LANES:])
        if staged:
            o_ref[:, lanes(k)] = obuf_ref[k]


def _s5(u, kc, wc, vc, a16, hinit, d, wglu2, layer, *, rows_tile, n_prompt_chunks):
    r, sw = u.shape
    no = sw // LANES
    nt = S5_TILES_PER_STEP if no % S5_TILES_PER_STEP == 0 else 1
    L = S5_CHUNK
    ct = rows_tile // L
    nrt = r // rows_tile
    sd = a16.shape[-1]
    return pl.pallas_call(
        functools.partial(_s5_kernel, n_prompt_chunks=n_prompt_chunks),
        grid=(no // nt, nrt),
        in_specs=[
            pl.BlockSpec((rows_tile, nt * LANES), lambda o, t: (t, o)),
            pl.BlockSpec((None, nt, L * LANES, SSM_CH), lambda o, t: (layer, o, 0, 0)),
            pl.BlockSpec((None, nt, L * LANES, 2 * SSM_STATE), lambda o, t: (layer, o, 0, 0)),
            pl.BlockSpec((None, nt, sd, L * SSM_CH), lambda o, t: (layer, o, 0, 0)),
            pl.BlockSpec((None, nt, 1, sd), lambda o, t: (layer, o, 0, 0)),
            pl.BlockSpec((None, ct, nt * sd), lambda o, t: (layer, t, o)),
            pl.BlockSpec((None, 1, nt * LANES), lambda o, t: (layer, 0, o)),
            pl.BlockSpec((None, nt, 2 * LANES, 2 * LANES), lambda o, t: (layer, o, 0, 0)),
        ],
        out_specs=[
            pl.BlockSpec((rows_tile, nt * LANES), lambda o, t: (t, o)),
            pl.BlockSpec((ct, nt * sd), lambda o, t: (t, o)),
        ],
        out_shape=[
            jax.ShapeDtypeStruct((r, sw), F32),
            jax.ShapeDtypeStruct((r // L, no * sd), F32),
        ],
        scratch_shapes=[pltpu.VMEM((nt, L * LANES, 2 * LANES), BF16),
                        pltpu.VMEM((nt, L * LANES, sd), BF16), pltpu.VMEM((nt, sd, L * LANES), BF16),
                        pltpu.VMEM((ct, nt * sd), F32), pltpu.VMEM((ct, nt * sd), F32),
                        pltpu.VMEM((1, nt * sd), F32)]
        + [pltpu.VMEM((nt, rows_tile, LANES), F32)] * (2 if nt > 1 else 0),
        compiler_params=_cparams("parallel", "arbitrary"),
        name="s5",
    )(u, kc, wc, vc, a16, hinit, d, wglu2)


def _s5_operators(a_re, a_im, log_dt, b_re, b_im, c_re, c_im, w_glu):
    g, p = a_re.shape
    c = SSM_CH
    L = S5_CHUNK
    gt = GROUPS_PER_TILE
    no = g // gt
    hp = lax.Precision.HIGHEST
    cmul = lambda xr, xi, yr, yi: (xr * yr - xi * yi, xr * yi + xi * yr)
    dt = jnp.exp(log_dt)[:, None]
    mag = jnp.exp(a_re * dt)
    ar, ai = mag * jnp.cos(a_im * dt), mag * jnp.sin(a_im * dt)
    den = a_re * a_re + a_im * a_im
    fr = ((ar - 1.0) * a_re + ai * a_im) / den
    fi = (ai * a_re - (ar - 1.0) * a_im) / den
    bbr, bbi = cmul(fr[..., None], fi[..., None], b_re, b_im)
    pr, pi = [jnp.ones_like(ar)], [jnp.zeros_like(ar)]
    for _ in range(L):
        nr, ni = cmul(pr[-1], pi[-1], ar, ai)
        pr.append(nr)
        pi.append(ni)
    pr, pi = jnp.stack(pr), jnp.stack(pi)

    abr, abi = cmul(pr[:L, :, :, None], pi[:L, :, :, None], bbr[None], bbi[None])
    kd = (jnp.einsum('gop,dgpi->gdio', c_re, abr, precision=hp)
          - jnp.einsum('gop,dgpi->gdio', c_im, abi, precision=hp))
    eye = jnp.eye(gt, dtype=F32)
    kc = kd.reshape(no, gt, L, c, c).transpose(0, 2, 1, 3, 4).reshape(no, L * LANES, c)

    wr, wi = cmul(pr[:L, :, None, :], pi[:L, :, None, :],
                  bbr.transpose(0, 2, 1)[None], bbi.transpose(0, 2, 1)[None])
    wc = jnp.stack([wr, wi], axis=3).reshape(L, no, gt, c, 2 * p)
    wc = wc.transpose(1, 0, 2, 3, 4).reshape(no, L * LANES, 2 * p)

    zr, zi = cmul(c_re.transpose(0, 2, 1)[:, :, None, :], c_im.transpose(0, 2, 1)[:, :, None, :],
                  pr[1:].transpose(1, 2, 0)[..., None], pi[1:].transpose(1, 2, 0)[..., None])
    vc = jnp.stack([zr, -zi], axis=0).reshape(2, no, gt, p, L * c)
    vc = vc.transpose(1, 0, 2, 3, 4).reshape(no, 2 * gt * p, L * c)

    a16 = jnp.concatenate([pr[L].reshape(no, 1, gt * p), pi[L].reshape(no, 1, gt * p)], axis=2)
    wg = w_glu.reshape(no, gt, c, c)
    wglu = (wg[:, :, :, None, :] * eye[None, :, None, :, None]).reshape(no, LANES, LANES)
    zero = jnp.zeros_like(wglu)
    wglu2 = jnp.concatenate([jnp.concatenate([wglu, zero], axis=2), jnp.concatenate([zero, wglu], axis=2)], axis=1)
    return kc.astype(BF16), wc.astype(BF16), vc.astype(BF16), a16, wglu2.astype(BF16)


def _out_kernel(x_ref, attn_ref, ssm_ref, ga_ref, gs_ref, gpost_ref, wa_ref, ws_ref, o_ref):
    an = _rms(attn_ref[...], ga_ref[...]).astype(BF16)
    sn = _rms(ssm_ref[...], gs_ref[...]).astype(BF16)
    mixed = _dot(an, wa_ref[...]) + _dot(sn, ws_ref[...])
    o_ref[...] = x_ref[...] + _rms(mixed, gpost_ref[...])


def _out_proj(x, attn, ssm, ga, gs, gpost, w_out, layer, *, tm):
    r, d = x.shape
    aw = attn.shape[1]
    sw = ssm.shape[1]
    assert aw == sw
    once = dict(pipeline_mode=pl.Buffered(1))
    return pl.pallas_call(
        _out_kernel,
        grid=(r // tm,),
        in_specs=[
            pl.BlockSpec((tm, d), lambda i: (i, 0)),
            pl.BlockSpec((tm, aw), lambda i: (i, 0)),
            pl.BlockSpec((tm, sw), lambda i: (i, 0)),
            pl.BlockSpec((1, aw), lambda i: (0, 0)),
            pl.BlockSpec((1, sw), lambda i: (0, 0)),
            pl.BlockSpec((1, d), lambda i: (0, 0)),
            pl.BlockSpec((None, aw, d), lambda i: (layer, 0, 0), **once),
            pl.BlockSpec((None, sw, d), lambda i: (layer, 1, 0), **once),
        ],
        out_specs=pl.BlockSpec((tm, d), lambda i: (i, 0)),
        out_shape=jax.ShapeDtypeStruct((r, d), F32),
        compiler_params=_cparams("parallel"),
        name="out_proj",
    )(x, attn, ssm, ga, gs, gpost, w_out, w_out)


def _tiles(r_min):
    tm = 512
    r = -(-r_min // tm) * tm
    s5_tiles = 4 if (r // 4) % (8 * S5_CHUNK) == 0 else 1
    tq = 3 * tm if r % (3 * tm) == 0 else tm
    tm_dense = 768 if r % 768 == 0 else tm
    return dict(r=r, tm=tm, tm_in=tm // 2, tm_dense=tm_dense, tq=tq, tf=512, s5_rows=r // s5_tiles, tk_cache=1024)


def _forward(x_prompt, x_sample, cache_k, cache_v, cache_logf, state_ssm_re, state_ssm_im, meta_tokens,
             ffn1_norm_pre, ffn1_norm_post, ffn1_w_gate, ffn1_w_up, ffn1_w_down,
             mix_norm_pre, mix_norm_post, w_in, b_forget,
             ssm_a_re, ssm_a_im, ssm_log_dt, ssm_b_re, ssm_b_im, ssm_c_re, ssm_c_im, ssm_d, ssm_w_glu,
             attn_out_norm, ssm_out_norm, w_out,
             ffn2_norm_pre, ffn2_norm_post, ffn2_w_gate, ffn2_w_up, ffn2_w_down, tiles=None):
    bsz, seq, d = x_prompt.shape
    nb, s_new, _ = x_sample.shape
    depth, _, past, nh, _ = cache_k.shape
    n_meta = meta_tokens.shape[0]
    aw = nh * HEAD_DIM
    g, p = ssm_a_re.shape[1:]
    L = S5_CHUNK
    assert bsz == 1 and s_new == L and n_meta % L == 0 and seq % L == 0 and p == SSM_STATE
    assert LANES % nh == 0 and s_new <= LANES // nh and g % GROUPS_PER_TILE == 0
    rep = LANES // nh
    no = g // GROUPS_PER_TILE
    sd = 2 * GROUPS_PER_TILE * p
    np_rows = n_meta + seq
    ns_rows = nb * s_new
    t = tiles or _tiles(np_rows + ns_rows)
    r = t["r"]
    npc = np_rows // L
    n_chunks = r // L

    x = jnp.concatenate([meta_tokens.astype(F32), x_prompt[0], x_sample.reshape(ns_rows, d),
                         jnp.zeros((r - np_rows - ns_rows, d), F32)], axis=0)
    cf = cache_logf.transpose(0, 1, 3, 2)

    row2 = lambda a: a.reshape(1, -1)
    ffn1_w = [w.astype(BF16) for w in (ffn1_w_gate, ffn1_w_up, ffn1_w_down)]
    ffn2_w = [w.astype(BF16) for w in (ffn2_w_gate, ffn2_w_up, ffn2_w_down)]
    w_out_b = w_out.astype(BF16)
    w4 = jnp.stack([w_in[:, :, :aw], w_in[:, :, aw:2 * aw], w_in[:, :, 2 * aw:3 * aw], w_in[:, :, 3 * aw + nh:]],
                   axis=1).astype(BF16)
    wf = jnp.repeat(w_in[:, :, 3 * aw:3 * aw + nh], rep, axis=2).astype(BF16)
    bf = jnp.repeat(b_forget, rep, axis=1).reshape(depth, 1, LANES)
    in_consts = _in_proj_constants(t["tm_in"], aw, rep)
    s5_ops = jax.vmap(_s5_operators)(ssm_a_re, ssm_a_im, ssm_log_dt, ssm_b_re, ssm_b_im, ssm_c_re, ssm_c_im, ssm_w_glu)
    h0 = jnp.concatenate([state_ssm_re.reshape(depth, nb, no, 1, sd // 2),
                          state_ssm_im.reshape(depth, nb, no, 1, sd // 2)], axis=3).reshape(depth, nb, no * sd)
    hinit = jnp.zeros((depth, n_chunks, no * sd), F32).at[:, npc:npc + nb].set(h0)
    outs = dict(logf=[], hend=[])
    kv = None
    for l in range(depth):
        x = _ffn(x, row2(ffn1_norm_pre[l]), row2(MACARON_W * ffn1_norm_post[l]), *ffn1_w, l,
                 tm=t["tm_dense"], tf=t["tf"])

        qs, *kv, uf, qa, ka, vt, logf, logf_t = _in_proj(x, mix_norm_pre.reshape(depth, 1, d), w4, wf, bf, in_consts, kv,
                                                 layer=l, depth=depth, np_rows=np_rows, ns_rows=ns_rows,
                                                 tm=t["tm_in"], tk=t["tm"])

        attn = _attn_prompt(qa, ka, vt, tq=t["tq"])

        qs = (qs * ATTN_SCALE).astype(BF16).reshape(nb, s_new, nh, HEAD_DIM)
        qs = jnp.pad(qs.transpose(0, 2, 3, 1), ((0, 0), (0, 0), (0, 0), (0, rep - s_new)))
        qbd = (qs[:, :, :, None, :] * jnp.eye(nh, dtype=BF16)[None, :, None, :, None]).reshape(nb, aw, LANES)
        attn = _attn_sample(l, qbd, cache_k, cache_v, cf, kv[4], kv[5], logf, attn,
                            np_rows=np_rows, tk=t["tk_cache"], rep=rep)

        kc, wc, vc, a16, wglu2 = s5_ops
        ssm, hend = _s5(uf, kc, wc, vc, a16, hinit, ssm_d.reshape(depth, 1, -1), wglu2, l,
                        rows_tile=t["s5_rows"], n_prompt_chunks=npc)

        x = _out_proj(x, attn, ssm, row2(attn_out_norm[l]), row2(ssm_out_norm[l]), row2(mix_norm_post[l]),
                      w_out_b, l, tm=t["tm_dense"])

        x = _ffn(x, row2(ffn2_norm_pre[l]), row2(MACARON_W * ffn2_norm_post[l]), *ffn2_w, l,
                 tm=t["tm_dense"], tf=t["tf"])

        outs["logf"].append(logf_t)
        outs["hend"].append(hend[npc - 1:npc + nb].reshape(1 + nb, no, 2, GROUPS_PER_TILE, p))

    sl_p = slice(0, np_rows)
    sl_s = slice(np_rows, np_rows + ns_rows)
    rows = lambda arrs, sl: jnp.stack([a[:, sl] for a in arrs]).transpose(0, 2, 1)
    heads = lambda a, n: a.reshape(depth, -1, n, nh, HEAD_DIM)
    states = lambda a: a.reshape(depth, -1, g, p)
    h_all = jnp.stack(outs["hend"])
    tail_at = (0, (np_rows // t["tm_in"]) * t["tm_in"] * nh, 0)
    k_prompt = lax.dynamic_update_slice(kv[0], kv[2], tail_at)
    v_prompt = lax.dynamic_update_slice(kv[1], kv[3], tail_at)
    return (x[n_meta:np_rows][None], x[sl_s].reshape(nb, s_new, d),
            heads(k_prompt, np_rows), heads(v_prompt, np_rows),
            rows(outs["logf"], sl_p).reshape(depth, 1, np_rows, nh),
            states(h_all[:, :1, :, 0]), states(h_all[:, :1, :, 1]),
            heads(kv[4], s_new), heads(kv[5], s_new),
            rows(outs["logf"], sl_s).reshape(depth, nb, s_new, nh),
            states(h_all[:, 1:, :, 0]), states(h_all[:, 1:, :, 1]))


def kernel(x_prompt, x_sample, cache_k, cache_v, cache_logf, state_ssm_re, state_ssm_im, meta_tokens, ffn1_norm_pre, ffn1_norm_post, ffn1_w_gate, ffn1_w_up, ffn1_w_down, mix_norm_pre, mix_norm_post, w_in, b_forget, ssm_a_re, ssm_a_im, ssm_log_dt, ssm_b_re, ssm_b_im, ssm_c_re, ssm_c_im, ssm_d, ssm_w_glu, attn_out_norm, ssm_out_norm, w_out, ffn2_norm_pre, ffn2_norm_post, ffn2_w_gate, ffn2_w_up, ffn2_w_down):
    return _forward(x_prompt, x_sample, cache_k, cache_v, cache_logf, state_ssm_re, state_ssm_im, meta_tokens,
                    ffn1_norm_pre, ffn1_norm_post, ffn1_w_gate, ffn1_w_up, ffn1_w_down,
                    mix_norm_pre, mix_norm_post, w_in, b_forget,
                    ssm_a_re, ssm_a_im, ssm_log_dt, ssm_b_re, ssm_b_im, ssm_c_re, ssm_c_im, ssm_d, ssm_w_glu,
                    attn_out_norm, ssm_out_norm, w_out,
                    ffn2_norm_pre, ffn2_norm_post, ffn2_w_gate, ffn2_w_up, ffn2_w_down)
```

```python
import functools
import math

import jax
import jax.numpy as jnp
from jax import lax
from jax.experimental import pallas as pl
from jax.experimental.pallas import tpu as pltpu

F32 = jnp.float32
BF16 = jnp.bfloat16

HEAD_DIM = 128
SSM_CH = 16
SSM_STATE = 64
S5_CHUNK = 16
LANES = 128
GROUPS_PER_TILE = LANES // SSM_CH
MACARON_W = 0.5
NORM_EPS = 1e-6
ATTN_SCALE = HEAD_DIM ** -0.5
LOG2E = math.log2(math.e)
NEG_INF = -1e30
VMEM_LIMIT = 56 * 1024 * 1024
N_BIAS_TERMS = 3
SCAN_UNROLL = 4
VT_ROWS = HEAD_DIM + 16
ATTN_PAIRS_PER_TRIP = 2


def _cparams(*sem):
    return pltpu.CompilerParams(dimension_semantics=sem, vmem_limit_bytes=VMEM_LIMIT)


def _rms(x, g):
    return x * lax.rsqrt(jnp.mean(x * x, axis=-1, keepdims=True) + NORM_EPS) * g


def _dot(a, b):
    return jnp.dot(a, b, preferred_element_type=F32)


def _dot_nt(a, b):
    return lax.dot_general(a, b, (((1,), (1,)), ((), ())), preferred_element_type=F32)


def _split3(x):
    hi = x.astype(BF16)
    r1 = x - hi.astype(F32)
    mid = r1.astype(BF16)
    lo = (r1 - mid.astype(F32)).astype(BF16)
    return hi, mid, lo


def _dot_exact_lhs01(sel, x):
    hi, mid, lo = _split3(x)
    return _dot(sel, hi) + _dot(sel, mid) + _dot(sel, lo)


def _iota2(shape, axis):
    return lax.broadcasted_iota(jnp.int32, shape, axis)


def _ffn_kernel(x_ref, gpre_ref, gpost_ref, wg_ref, wu_ref, wd_ref, o_ref, xn_ref, acc_ref):
    j = pl.program_id(1)

    @pl.when(j == 0)
    def _():
        xn_ref[...] = _rms(x_ref[...], gpre_ref[...]).astype(BF16)
        acc_ref[...] = jnp.zeros_like(acc_ref)

    xn = xn_ref[...]
    g = _dot(xn, wg_ref[...])
    u = _dot(xn, wu_ref[...])
    h = (g * jax.nn.sigmoid(g) * u).astype(BF16)
    acc_ref[...] += _dot(h, wd_ref[...])

    @pl.when(j == pl.num_programs(1) - 1)
    def _():
        o_ref[...] = x_ref[...] + _rms(acc_ref[...], gpost_ref[...])


def _ffn(x, g_pre, g_post, wg, wu, wd, layer, *, tm, tf):
    r, d = x.shape
    f = wg.shape[2]
    return pl.pallas_call(
        _ffn_kernel,
        grid=(r // tm, f // tf),
        in_specs=[
            pl.BlockSpec((tm, d), lambda i, j: (i, 0)),
            pl.BlockSpec((1, d), lambda i, j: (0, 0)),
            pl.BlockSpec((1, d), lambda i, j: (0, 0)),
            pl.BlockSpec((None, d, tf), lambda i, j: (layer, 0, j)),
            pl.BlockSpec((None, d, tf), lambda i, j: (layer, 0, j)),
            pl.BlockSpec((None, tf, d), lambda i, j: (layer, j, 0)),
        ],
        out_specs=pl.BlockSpec((tm, d), lambda i, j: (i, 0)),
        out_shape=jax.ShapeDtypeStruct((r, d), F32),
        scratch_shapes=[pltpu.VMEM((tm, d), BF16), pltpu.VMEM((tm, d), F32)],
        compiler_params=_cparams("parallel", "arbitrary"),
        name="ffn",
    )(x, g_pre, g_post, wg, wu, wd)


def _in_kernel(x_ref, g_ref, w_ref, wf_ref, bf_ref, tril_ref, selk_ref, selq_ref, *refs,
               n_alias, np_rows, ns_rows, rep):
    (qs_ref, kp_ref, vp_ref, kt_ref, vt_tail_ref, ks_ref, vs_ref, uf_ref, qa_ref, ka_ref, vt_ref, logf_ref,
     logf_t_ref, carry_ref) = refs[n_alias:]
    i = pl.program_id(0)
    tm = x_ref.shape[0]
    aw = w_ref.shape[2]
    nh = aw // HEAD_DIM
    nb = N_BIAS_TERMS
    n_full, n_tail = divmod(np_rows, tm)

    xn = _rms(x_ref[...], g_ref[...]).astype(BF16)
    logf = jax.nn.log_sigmoid(_dot(xn, wf_ref[...]) + bf_ref[...])
    logf_ref[...] = logf
    logf_t = jnp.transpose(logf)
    logf_t_ref[...] = jnp.concatenate([logf_t[h * rep:h * rep + 1, :] for h in range(nh)], axis=0)

    @pl.when(i == 0)
    def _():
        carry_ref[...] = jnp.zeros_like(carry_ref)

    fcum = _dot_exact_lhs01(tril_ref[...], logf) + carry_ref[...]
    carry_ref[...] = fcum[tm - 1:tm, :]
    terms = jnp.concatenate(_split3(fcum * LOG2E), axis=1)
    lane_t = _iota2((tm, aw), 1) % HEAD_DIM
    kaug = (jnp.logical_and(lane_t >= nb, lane_t < 2 * nb).astype(F32) - _dot(terms, selk_ref[...])).astype(BF16)
    qaug = ((lane_t < nb).astype(F32) + _dot(terms, selq_ref[...])).astype(BF16)

    def head(a, h):
        return a[:, h * HEAD_DIM:(h + 1) * HEAD_DIM]

    def interleave(dst_ref, zb, aug):
        for h in range(nh):
            dst_ref[:, 2 * h * HEAD_DIM:(2 * h + 1) * HEAD_DIM] = head(zb, h)
            dst_ref[:, (2 * h + 1) * HEAD_DIM:(2 * h + 2) * HEAD_DIM] = head(aug, h)

    def sample_rows(z, dst_ref, by_head):
        for tile in range(np_rows // tm, (np_rows + ns_rows - 1) // tm + 1):
            lo, hi = max(np_rows, tile * tm), min(np_rows + ns_rows, (tile + 1) * tm)

            @pl.when(i == tile)
            def _(tile=tile, lo=lo, hi=hi):
                rows = z[lo - tile * tm:hi - tile * tm, :]
                if by_head:
                    for h in range(nh):
                        dst_ref[pl.ds((lo - np_rows) * nh + h, hi - lo, stride=nh), :] = head(rows, h)
                else:
                    dst_ref[lo - np_rows:hi - np_rows, :] = rows

    def returned_rows(z, p_ref, t_ref, s_ref):
        @pl.when(i < n_full)
        def _():
            for h in range(nh):
                p_ref[pl.ds(h, tm, stride=nh), :] = head(z, h)

        @pl.when(i == n_full)
        def _():
            for h in range(nh):
                t_ref[pl.ds(h, n_tail, stride=nh), :] = head(z[:n_tail, :], h)

        sample_rows(z, s_ref, True)

    z = _dot(xn, w_ref[0])
    interleave(qa_ref, (z * (ATTN_SCALE * LOG2E)).astype(BF16), qaug)
    sample_rows(z, qs_ref, False)

    z = _dot(xn, w_ref[1])
    interleave(ka_ref, z.astype(BF16), kaug)
    returned_rows(z, kp_ref, kt_ref, ks_ref)

    z = _dot(xn, w_ref[2])
    returned_rows(z, vp_ref, vt_tail_ref, vs_ref)
    ones_row = (_iota2((VT_ROWS - HEAD_DIM, tm), 0) == 0).astype(BF16)
    for h in range(nh):
        vt_ref[h, :HEAD_DIM, :] = jnp.transpose(head(z, h)).astype(BF16)
        vt_ref[h, HEAD_DIM:, :] = ones_row

    uf_ref[...] = _dot(xn, w_ref[3])


def _in_proj_constants(tm, aw, rep):
    nb = N_BIAS_TERMS
    tril = (jnp.arange(tm)[None, :] <= jnp.arange(tm)[:, None]).astype(BF16)
    src = jnp.arange(nb * LANES)[:, None]
    dst = jnp.arange(aw)[None, :]
    head_src = src % LANES == (dst // HEAD_DIM) * rep
    selk = jnp.logical_and(head_src, dst % HEAD_DIM == src // LANES).astype(BF16)
    selq = jnp.logical_and(head_src, dst % HEAD_DIM == nb + src // LANES).astype(BF16)
    return tril, selk, selq


def _in_proj(x, g, w4, wf, bf, consts, kv_prev, *, layer, depth, np_rows, ns_rows, tm, tk):
    r, d = x.shape
    aw = w4.shape[-1]
    nh = aw // HEAD_DIM
    n_full, n_tail = divmod(np_rows, tm)
    assert n_tail > 0 and n_full > 0 and tk % tm == 0
    per_block = tk // tm
    n_alias = 0 if kv_prev is None else len(kv_prev)
    kernel_fn = functools.partial(_in_kernel, n_alias=n_alias, np_rows=np_rows, ns_rows=ns_rows, rep=LANES // nh)
    once = dict(pipeline_mode=pl.Buffered(1))
    const = lambda a: pl.BlockSpec(a.shape, lambda i: (0,) * a.ndim, **once)
    layered = lambda a: pl.BlockSpec((None,) + a.shape[1:], lambda i: (layer,) + (0,) * (a.ndim - 1), **once)
    prompt_spec = pl.BlockSpec((None, tm * nh, HEAD_DIM), lambda i: (layer, jnp.minimum(i, n_full - 1), 0))
    tail_spec = pl.BlockSpec((None, n_tail * nh, HEAD_DIM), lambda i: (layer, 0, 0))
    sample_spec = pl.BlockSpec((None, ns_rows * nh, HEAD_DIM), lambda i: (layer, 0, 0))
    prompt_shape = jax.ShapeDtypeStruct((depth, np_rows * nh, HEAD_DIM), F32)
    tail_shape = jax.ShapeDtypeStruct((depth, n_tail * nh, HEAD_DIM), F32)
    sample_shape = jax.ShapeDtypeStruct((depth, ns_rows * nh, HEAD_DIM), F32)
    return pl.pallas_call(
        kernel_fn,
        grid=(r // tm,),
        in_specs=[
            pl.BlockSpec((tm, d), lambda i: (i, 0)),
            layered(g), layered(w4), layered(wf), layered(bf), *[const(c) for c in consts],
        ] + [pl.BlockSpec(memory_space=pl.ANY)] * n_alias,
        out_specs=[
            pl.BlockSpec((ns_rows, aw), lambda i: (0, 0)),
            prompt_spec, prompt_spec, tail_spec, tail_spec, sample_spec, sample_spec,
            pl.BlockSpec((tm, aw), lambda i: (i, 0)),
            pl.BlockSpec((tm, 2 * aw), lambda i: (i, 0)),
            pl.BlockSpec((tm, 2 * aw), lambda i: (i, 0)),
            pl.BlockSpec((nh, None, VT_ROWS, tm), lambda i: (0, i // per_block, 0, i % per_block)),
            pl.BlockSpec((tm, LANES), lambda i: (i, 0)),
            pl.BlockSpec((nh, tm), lambda i: (0, i)),
        ],
        out_shape=[
            jax.ShapeDtypeStruct((ns_rows, aw), F32),
            prompt_shape, prompt_shape, tail_shape, tail_shape, sample_shape, sample_shape,
            jax.ShapeDtypeStruct((r, aw), F32),
            jax.ShapeDtypeStruct((r, 2 * aw), BF16),
            jax.ShapeDtypeStruct((r, 2 * aw), BF16),
            jax.ShapeDtypeStruct((nh, r // tk, VT_ROWS, tk), BF16),
            jax.ShapeDtypeStruct((r, LANES), F32),
            jax.ShapeDtypeStruct((nh, r), F32),
        ],
        scratch_shapes=[pltpu.VMEM((1, LANES), F32)],
        input_output_aliases={5 + len(consts) + n: 1 + n for n in range(n_alias)},
        compiler_params=_cparams("arbitrary"),
        name="in_proj",
    )(x, g, w4, wf, bf, *consts, *(kv_prev or ()))


def _attn_p_kernel(qa_ref, ka_ref, vt_ref, o_ref, sa_ref, sb_ref, acc_ref):
    qi = pl.program_id(1)
    tq = qa_ref.shape[0]
    tk = sa_ref.shape[0]
    n_diag = tq // tk
    n_off = qi * n_diag

    def produce(s_ref, kb):
        s = _dot_nt(ka_ref[pl.ds(pl.multiple_of(kb * tk, tk), tk), :], qa_ref[...])
        s_ref[...] = s
        return jnp.max(s, axis=0, keepdims=True)

    def consume(s_ref, m_blk, kb, m):
        m_new = jnp.maximum(m, m_blk)
        p = jnp.exp2(s_ref[...] - m_new)
        acc_ref[...] = jnp.exp2(m - m_new) * acc_ref[...] + _dot(vt_ref[kb], p.astype(BF16))
        return m_new

    acc_ref[...] = jnp.zeros_like(acc_ref)
    m_first = produce(sa_ref, 0)

    def pair(kb, carry):
        m_a, m = carry
        m_b = produce(sb_ref, kb + 1)
        m = consume(sa_ref, m_a, kb, m)
        m_a = produce(sa_ref, kb + 2)
        m = consume(sb_ref, m_b, kb + 1, m)
        return m_a, m

    init = (m_first, jnp.full((1, tq), NEG_INF, F32))
    def pairs(first, count, carry):
        for n in range(count):
            carry = pair(first + 2 * n, carry)
        return carry

    per_trip = 2 * ATTN_PAIRS_PER_TRIP
    carry = lax.fori_loop(0, n_off // per_trip, lambda i, c: pairs(per_trip * i, ATTN_PAIRS_PER_TRIP, c), init)
    done = (n_off // per_trip) * per_trip
    m_a, m = lax.fori_loop(0, (n_off % per_trip) // 2, lambda i, c: pair(done + 2 * i, c), carry)
    def produce_diag(s_ref, j):
        c0 = j * tk
        kb = n_off + j
        s_ref[:, :tq - c0] = _dot_nt(ka_ref[pl.ds(pl.multiple_of(kb * tk, tk), tk), :], qa_ref[c0:, :])

    def consume_diag(s_ref, j, m):
        c0 = j * tk
        w = tq - c0
        visible = _iota2((tk, w), 1) >= _iota2((tk, w), 0)
        s = jnp.where(visible, s_ref[:, :w], NEG_INF)
        m_old = m[:, c0:]
        m_new = jnp.maximum(m_old, jnp.max(s, axis=0, keepdims=True))
        p = jnp.exp2(s - m_new)
        acc_ref[:, c0:] = jnp.exp2(m_old - m_new) * acc_ref[:, c0:] + _dot(vt_ref[n_off + j], p.astype(BF16))
        return m_new if c0 == 0 else jnp.concatenate([m[:, :c0], m_new], axis=1)

    def tail(lead_block, m_lead, m):
        bufs = (sa_ref, sb_ref)
        has_lead = lead_block is not None
        for idx in range(n_diag + has_lead):
            cur = bufs[idx % 2]
            j = idx - has_lead
            if j + 1 < n_diag:
                produce_diag(bufs[(idx + 1) % 2], j + 1)
            if j < 0:
                m = consume(cur, m_lead, lead_block, m)
            else:
                m = consume_diag(cur, j, m)
        o_ref[...] = jnp.transpose(acc_ref[:HEAD_DIM, :] / acc_ref[HEAD_DIM:HEAD_DIM + 1, :])

    @pl.when(n_off % 2 == 0)
    def _():
        tail(None, None, m)

    @pl.when(n_off % 2 == 1)
    def _():
        tail(n_off - 1, m_a, m)


def _attn_prompt(qa, ka, vt, *, tq):
    nh, nkb, _, tk = vt.shape
    r = nkb * tk
    return pl.pallas_call(
        _attn_p_kernel,
        grid=(nh, r // tq),
        in_specs=[
            pl.BlockSpec((tq, 2 * HEAD_DIM), lambda h, i: (i, h)),
            pl.BlockSpec((r, 2 * HEAD_DIM), lambda h, i: (0, h)),
            pl.BlockSpec((None, nkb, VT_ROWS, tk), lambda h, i: (h, 0, 0, 0)),
        ],
        out_specs=pl.BlockSpec((tq, HEAD_DIM), lambda h, i: (i, h)),
        out_shape=jax.ShapeDtypeStruct((r, nh * HEAD_DIM), F32),
        scratch_shapes=[pltpu.VMEM((tk, tq), F32), pltpu.VMEM((tk, tq), F32), pltpu.VMEM((VT_ROWS, tq), F32)],
        compiler_params=_cparams("parallel", "arbitrary"),
        name="attn_p",
    )(qa, ka, vt)


def _attn_s_kernel(qbd_ref, ck_ref, cv_ref, cf_ref, kn_ref, vn_ref, fn_ref, attn_in_ref, o_ref,
                   m_ref, l_ref, acc_ref, suf_ref, cq_ref, *, rep, seg, nh):
    del attn_in_ref
    step = pl.program_id(1)
    s_new = fn_ref.shape[0]
    tk = cf_ref.shape[1]
    qbd = qbd_ref[...]

    def heads_side_by_side(ref, n):
        return jnp.concatenate([ref[pl.ds(h, n, stride=nh), :].astype(BF16) for h in range(nh)], axis=1)

    def update(t, v_bf):
        m_old = m_ref[...]
        m_new = jnp.maximum(m_old, jnp.max(t, axis=0, keepdims=True))
        p = jnp.exp(t - m_new)
        alpha = jnp.exp(m_old - m_new)
        l_ref[...] = alpha * l_ref[...] + jnp.sum(p, axis=0, keepdims=True)
        m_ref[...] = m_new
        alpha_col = jnp.transpose(jnp.broadcast_to(alpha, (LANES, LANES)))
        pv = _dot(jnp.transpose(p).astype(BF16), v_bf)
        for c in range(nh):
            sl = slice(c * HEAD_DIM, (c + 1) * HEAD_DIM)
            acc_ref[:, sl] = acc_ref[:, sl] * alpha_col + pv[:, sl]

    @pl.when(step == 0)
    def _():
        m_ref[...] = jnp.full_like(m_ref, NEG_INF)
        l_ref[...] = jnp.zeros_like(l_ref)
        acc_ref[...] = jnp.zeros_like(acc_ref)
        tril = (_iota2((s_new, s_new), 1) <= _iota2((s_new, s_new), 0)).astype(BF16)
        cn = _dot_exact_lhs01(tril, fn_ref[...])
        krow = _iota2((s_new, LANES), 0)
        qlane = _iota2((s_new, LANES), 1) % rep
        cq = jnp.sum(jnp.where(krow == qlane, cn, 0.0), axis=0, keepdims=True)
        cq_ref[...] = cq
        suf_ref[...] = jnp.zeros_like(suf_ref)
        st = _dot(heads_side_by_side(kn_ref, s_new), qbd)
        t = jnp.where(krow <= qlane, st + (cq - cn), NEG_INF)
        update(t, heads_side_by_side(vn_ref, s_new))

    spread = (_iota2((LANES, nh), 0) // rep == _iota2((LANES, nh), 1)).astype(BF16)
    cf_blk = jnp.transpose(_dot_exact_lhs01(spread, cf_ref[...]))
    triu = (_iota2((seg, seg), 1) > _iota2((seg, seg), 0)).astype(BF16)
    carry = suf_ref[...]
    sufs = [None] * (tk // seg)
    for sidx in reversed(range(tk // seg)):
        x = cf_blk[sidx * seg:(sidx + 1) * seg, :]
        sfx = _dot_exact_lhs01(triu, x) + carry
        sufs[sidx] = sfx
        carry = sfx[0:1, :] + x[0:1, :]
    suf_ref[...] = carry
    bias = jnp.concatenate(sufs, axis=0) + cq_ref[...]
    update(_dot(heads_side_by_side(ck_ref, tk), qbd) + bias, heads_side_by_side(cv_ref, tk))

    @pl.when(step == pl.num_programs(1) - 1)
    def _():
        l_col = jnp.transpose(jnp.broadcast_to(l_ref[...], (LANES, LANES)))
        for c in range(nh):
            rws = slice(c * rep, c * rep + s_new)
            sl = slice(c * HEAD_DIM, (c + 1) * HEAD_DIM)
            o_ref[:, sl] = acc_ref[rws, sl] / l_col[rws, :]


def _attn_sample(layer, qbd, cache_k, cache_v, cf, kf, vf, logf, attn, *, np_rows, tk, rep):
    depth, nb, past, nh, _ = cache_k.shape
    aw = nh * HEAD_DIM
    s_new = S5_CHUNK
    nkb = past // tk
    base = np_rows // s_new
    seg = min(tk, LANES)

    ck = cache_k.reshape(depth, nb, past * nh, HEAD_DIM)
    cv = cache_v.reshape(depth, nb, past * nh, HEAD_DIM)
    cache_spec = pl.BlockSpec((None, None, tk * nh, HEAD_DIM), lambda b, s: (layer, b, nkb - 1 - s, 0))
    return pl.pallas_call(
        functools.partial(_attn_s_kernel, rep=rep, seg=seg, nh=nh),
        grid=(nb, nkb),
        in_specs=[
            pl.BlockSpec((None, aw, LANES), lambda b, s: (b, 0, 0)),
            cache_spec,
            cache_spec,
            pl.BlockSpec((None, None, nh, tk), lambda b, s: (layer, b, 0, nkb - 1 - s)),
            pl.BlockSpec((None, s_new * nh, HEAD_DIM), lambda b, s: (layer, b, 0)),
            pl.BlockSpec((None, s_new * nh, HEAD_DIM), lambda b, s: (layer, b, 0)),
            pl.BlockSpec((s_new, LANES), lambda b, s: (base + b, 0)),
            pl.BlockSpec(memory_space=pl.ANY),
        ],
        out_specs=pl.BlockSpec((s_new, aw), lambda b, s: (base + b, 0)),
        out_shape=jax.ShapeDtypeStruct(attn.shape, F32),
        scratch_shapes=[
            pltpu.VMEM((1, LANES), F32), pltpu.VMEM((1, LANES), F32), pltpu.VMEM((LANES, aw), F32),
            pltpu.VMEM((1, LANES), F32), pltpu.VMEM((1, LANES), F32),
        ],
        input_output_aliases={7: 0},
        compiler_params=_cparams("parallel", "arbitrary"),
        name="attn_s",
    )(qbd, ck, cv, cf, kf, vf, logf, attn)


def _s5_kernel(u_ref, kc_ref, wc_ref, vc_ref, a_ref, hinit_ref, d_ref, wglu2_ref, o_ref, hend_ref,
               kds2_ref, wfull_ref, vfull_ref, s_ref, hprev_ref, hc_ref, *, n_prompt_chunks):
    rt = pl.program_id(1)
    ct, sd = s_ref.shape
    half = sd // 2
    L = S5_CHUNK
    P = SSM_STATE

    @pl.when(rt == 0)
    def _():
        hc_ref[...] = jnp.zeros_like(hc_ref)
        r_, c_ = _iota2((2 * P, sd), 0), _iota2((2 * P, sd), 1)
        rep_w = jnp.logical_and(r_ // P == c_ // half, r_ % P == c_ % P).astype(BF16)
        r_, c_ = _iota2((L * LANES, sd), 0), _iota2((L * LANES, sd), 1)
        same_w = (r_ // SSM_CH) % GROUPS_PER_TILE == (c_ % half) // P
        r_, c_ = _iota2((L * SSM_CH, L * LANES), 0), _iota2((L * SSM_CH, L * LANES), 1)
        rep_v = jnp.logical_and(r_ // SSM_CH == c_ // LANES, r_ % SSM_CH == c_ % SSM_CH).astype(BF16)
        r_, c_ = _iota2((sd, L * LANES), 0), _iota2((sd, L * LANES), 1)
        same_v = (r_ % half) // P == (c_ % LANES) // SSM_CH
        r_, c_ = _iota2((SSM_CH, LANES), 0), _iota2((SSM_CH, LANES), 1)
        rep_k = (r_ == c_ % SSM_CH).astype(BF16)
        r_, c_ = _iota2((L * LANES, LANES), 0), _iota2((L * LANES, LANES), 1)
        same_k = (r_ // SSM_CH) % GROUPS_PER_TILE == c_ // SSM_CH
        wfull_ref[...] = jnp.where(same_w, _dot(wc_ref[...], rep_w), 0.0).astype(BF16)
        vfull_ref[...] = jnp.where(same_v, _dot(vc_ref[...], rep_v), 0.0).astype(BF16)
        kds = jnp.where(same_k, _dot(kc_ref[...], rep_k), 0.0).astype(BF16)
        kds2_ref[:, :LANES] = kds
        kds2_ref[:LANES, LANES:] = jnp.zeros((LANES, LANES), BF16)
        kds2_ref[LANES:, LANES:] = kds[:(L - 1) * LANES, :]

    def u_at(tau):
        return u_ref[pl.ds(tau, ct, stride=L), :]

    xr = jnp.concatenate([u_at(L - 1 - j).astype(BF16) for j in range(L)], axis=1)
    s_ref[...] = _dot(xr, wfull_ref[...])

    ar = a_ref[:, :half]
    ai = a_ref[:, half:]

    def scan(c, hcar):
        cg = rt * ct + c
        reset = jnp.logical_or(cg == 0, cg >= n_prompt_chunks)
        hp = jnp.where(reset, hinit_ref[pl.ds(c, 1), :], hcar)
        hprev_ref[pl.ds(c, 1), :] = hp
        s = s_ref[pl.ds(c, 1), :]
        hr = hp[:, :half]
        hi = hp[:, half:]
        hn = jnp.concatenate([ar * hr - ai * hi + s[:, :half], ar * hi + ai * hr + s[:, half:]], axis=1)
        hend_ref[pl.ds(c, 1), :] = hn
        return hn

    hc_ref[...] = lax.fori_loop(0, ct, scan, hc_ref[...], unroll=SCAN_UNROLL)

    ystate = _dot(hprev_ref[...].astype(BF16), vfull_ref[...])
    d2 = jnp.concatenate([d_ref[...], d_ref[...]], axis=1)
    wglu2 = wglu2_ref[...]
    for tau in range(0, L, 2):
        lag_rows = (tau + 2) * LANES
        yy = _dot(xr[:, (L - 2 - tau) * LANES:], kds2_ref[:lag_rows, :])
        yy = yy + jnp.concatenate([ystate[:, (tau + 1) * LANES:(tau + 2) * LANES],
                                   ystate[:, tau * LANES:(tau + 1) * LANES]], axis=1)
        yy = yy + d2 * jnp.concatenate([u_at(tau + 1), u_at(tau)], axis=1)
        yy = 0.5 * yy * (1.0 + jnp.tanh(math.sqrt(2.0 / math.pi) * (yy + 0.044715 * (yy * yy * yy))))
        out = yy * jax.nn.sigmoid(_dot(yy.astype(BF16), wglu2))
        o_ref[pl.ds(tau + 1, ct, stride=L), :] = out[:, :LANES]
        o_ref[pl.ds(tau, ct, stride=L), :] = out[:, LANES:]


def _s5(u, kc, wc, vc, a16, hinit, d, wglu2, layer, *, rows_tile, n_prompt_chunks):
    r, sw = u.shape
    no = sw // LANES
    L = S5_CHUNK
    ct = rows_tile // L
    nrt = r // rows_tile
    sd = a16.shape[-1]
    return pl.pallas_call(
        functools.partial(_s5_kernel, n_prompt_chunks=n_prompt_chunks),
        grid=(no, nrt),
        in_specs=[
            pl.BlockSpec((rows_tile, LANES), lambda o, t: (t, o)),
            pl.BlockSpec((None, None, L * LANES, SSM_CH), lambda o, t: (layer, o, 0, 0)),
            pl.BlockSpec((None, None, L * LANES, 2 * SSM_STATE), lambda o, t: (layer, o, 0, 0)),
            pl.BlockSpec((None, None, sd, L * SSM_CH), lambda o, t: (layer, o, 0, 0)),
            pl.BlockSpec((None, None, 1, sd), lambda o, t: (layer, o, 0, 0)),
            pl.BlockSpec((None, ct, sd), lambda o, t: (layer, t, o)),
            pl.BlockSpec((None, 1, LANES), lambda o, t: (layer, 0, o)),
            pl.BlockSpec((None, None, 2 * LANES, 2 * LANES), lambda o, t: (layer, o, 0, 0)),
        ],
        out_specs=[
            pl.BlockSpec((rows_tile, LANES), lambda o, t: (t, o)),
            pl.BlockSpec((ct, sd), lambda o, t: (t, o)),
        ],
        out_shape=[
            jax.ShapeDtypeStruct((r, sw), F32),
            jax.ShapeDtypeStruct((r // L, no * sd), F32),
        ],
        scratch_shapes=[pltpu.VMEM((L * LANES, 2 * LANES), BF16),
                        pltpu.VMEM((L * LANES, sd), BF16), pltpu.VMEM((sd, L * LANES), BF16),
                        pltpu.VMEM((ct, sd), F32), pltpu.VMEM((ct, sd), F32), pltpu.VMEM((1, sd), F32)],
        compiler_params=_cparams("parallel", "arbitrary"),
        name="s5",
    )(u, kc, wc, vc, a16, hinit, d, wglu2)


def _s5_operators(a_re, a_im, log_dt, b_re, b_im, c_re, c_im, w_glu):
    g, p = a_re.shape
    c = SSM_CH
    L = S5_CHUNK
    gt = GROUPS_PER_TILE
    no = g // gt
    hp = lax.Precision.HIGHEST
    cmul = lambda xr, xi, yr, yi: (xr * yr - xi * yi, xr * yi + xi * yr)
    dt = jnp.exp(log_dt)[:, None]
    mag = jnp.exp(a_re * dt)
    ar, ai = mag * jnp.cos(a_im * dt), mag * jnp.sin(a_im * dt)
    den = a_re * a_re + a_im * a_im
    fr = ((ar - 1.0) * a_re + ai * a_im) / den
    fi = (ai * a_re - (ar - 1.0) * a_im) / den
    bbr, bbi = cmul(fr[..., None], fi[..., None], b_re, b_im)
    pr, pi = [jnp.ones_like(ar)], [jnp.zeros_like(ar)]
    for _ in range(L):
        nr, ni = cmul(pr[-1], pi[-1], ar, ai)
        pr.append(nr)
        pi.append(ni)
    pr, pi = jnp.stack(pr), jnp.stack(pi)

    abr, abi = cmul(pr[:L, :, :, None], pi[:L, :, :, None], bbr[None], bbi[None])
    kd = (jnp.einsum('gop,dgpi->gdio', c_re, abr, precision=hp)
          - jnp.einsum('gop,dgpi->gdio', c_im, abi, precision=hp))
    eye = jnp.eye(gt, dtype=F32)
    kc = kd.reshape(no, gt, L, c, c).transpose(0, 2, 1, 3, 4).reshape(no, L * LANES, c)

    wr, wi = cmul(pr[:L, :, None, :], pi[:L, :, None, :],
                  bbr.transpose(0, 2, 1)[None], bbi.transpose(0, 2, 1)[None])
    wc = jnp.stack([wr, wi], axis=3).reshape(L, no, gt, c, 2 * p)
    wc = wc.transpose(1, 0, 2, 3, 4).reshape(no, L * LANES, 2 * p)

    zr, zi = cmul(c_re.transpose(0, 2, 1)[:, :, None, :], c_im.transpose(0, 2, 1)[:, :, None, :],
                  pr[1:].transpose(1, 2, 0)[..., None], pi[1:].transpose(1, 2, 0)[..., None])
    vc = jnp.stack([zr, -zi], axis=0).reshape(2, no, gt, p, L * c)
    vc = vc.transpose(1, 0, 2, 3, 4).reshape(no, 2 * gt * p, L * c)

    a16 = jnp.concatenate([pr[L].reshape(no, 1, gt * p), pi[L].reshape(no, 1, gt * p)], axis=2)
    wg = w_glu.reshape(no, gt, c, c)
    wglu = (wg[:, :, :, None, :] * eye[None, :, None, :, None]).reshape(no, LANES, LANES)
    zero = jnp.zeros_like(wglu)
    wglu2 = jnp.concatenate([jnp.concatenate([wglu, zero], axis=2), jnp.concatenate([zero, wglu], axis=2)], axis=1)
    return kc.astype(BF16), wc.astype(BF16), vc.astype(BF16), a16, wglu2.astype(BF16)


def _out_kernel(x_ref, attn_ref, ssm_ref, ga_ref, gs_ref, gpost_ref, wa_ref, ws_ref, o_ref):
    an = _rms(attn_ref[...], ga_ref[...]).astype(BF16)
    sn = _rms(ssm_ref[...], gs_ref[...]).astype(BF16)
    mixed = _dot(an, wa_ref[...]) + _dot(sn, ws_ref[...])
    o_ref[...] = x_ref[...] + _rms(mixed, gpost_ref[...])


def _out_proj(x, attn, ssm, ga, gs, gpost, w_out, layer, *, tm):
    r, d = x.shape
    aw = attn.shape[1]
    sw = ssm.shape[1]
    assert aw == sw
    once = dict(pipeline_mode=pl.Buffered(1))
    return pl.pallas_call(
        _out_kernel,
        grid=(r // tm,),
        in_specs=[
            pl.BlockSpec((tm, d), lambda i: (i, 0)),
            pl.BlockSpec((tm, aw), lambda i: (i, 0)),
            pl.BlockSpec((tm, sw), lambda i: (i, 0)),
            pl.BlockSpec((1, aw), lambda i: (0, 0)),
            pl.BlockSpec((1, sw), lambda i: (0, 0)),
            pl.BlockSpec((1, d), lambda i: (0, 0)),
            pl.BlockSpec((None, aw, d), lambda i: (layer, 0, 0), **once),
            pl.BlockSpec((None, sw, d), lambda i: (layer, 1, 0), **once),
        ],
        out_specs=pl.BlockSpec((tm, d), lambda i: (i, 0)),
        out_shape=jax.ShapeDtypeStruct((r, d), F32),
        compiler_params=_cparams("parallel"),
        name="out_proj",
    )(x, attn, ssm, ga, gs, gpost, w_out, w_out)


def _tiles(r_min):
    tm = 512
    r = -(-r_min // tm) * tm
    s5_tiles = 4 if (r // 4) % (8 * S5_CHUNK) == 0 else 1
    tq = 3 * tm if r % (3 * tm) == 0 else tm
    tm_dense = 768 if r % 768 == 0 else tm
    return dict(r=r, tm=tm, tm_in=tm // 2, tm_dense=tm_dense, tq=tq, tf=512, s5_rows=r // s5_tiles, tk_cache=1024)


def _forward(x_prompt, x_sample, cache_k, cache_v, cache_logf, state_ssm_re, state_ssm_im, meta_tokens,
             ffn1_norm_pre, ffn1_norm_post, ffn1_w_gate, ffn1_w_up, ffn1_w_down,
             mix_norm_pre, mix_norm_post, w_in, b_forget,
             ssm_a_re, ssm_a_im, ssm_log_dt, ssm_b_re, ssm_b_im, ssm_c_re, ssm_c_im, ssm_d, ssm_w_glu,
             attn_out_norm, ssm_out_norm, w_out,
             ffn2_norm_pre, ffn2_norm_post, ffn2_w_gate, ffn2_w_up, ffn2_w_down, tiles=None):
    bsz, seq, d = x_prompt.shape
    nb, s_new, _ = x_sample.shape
    depth, _, past, nh, _ = cache_k.shape
    n_meta = meta_tokens.shape[0]
    aw = nh * HEAD_DIM
    g, p = ssm_a_re.shape[1:]
    L = S5_CHUNK
    assert bsz == 1 and s_new == L and n_meta % L == 0 and seq % L == 0 and p == SSM_STATE
    assert LANES % nh == 0 and s_new <= LANES // nh and g % GROUPS_PER_TILE == 0
    rep = LANES // nh
    no = g // GROUPS_PER_TILE
    sd = 2 * GROUPS_PER_TILE * p
    np_rows = n_meta + seq
    ns_rows = nb * s_new
    t = tiles or _tiles(np_rows + ns_rows)
    r = t["r"]
    npc = np_rows // L
    n_chunks = r // L

    x = jnp.concatenate([meta_tokens.astype(F32), x_prompt[0], x_sample.reshape(ns_rows, d),
                         jnp.zeros((r - np_rows - ns_rows, d), F32)], axis=0)
    cf = cache_logf.transpose(0, 1, 3, 2)

    row2 = lambda a: a.reshape(1, -1)
    ffn1_w = [w.astype(BF16) for w in (ffn1_w_gate, ffn1_w_up, ffn1_w_down)]
    ffn2_w = [w.astype(BF16) for w in (ffn2_w_gate, ffn2_w_up, ffn2_w_down)]
    w_out_b = w_out.astype(BF16)
    w4 = jnp.stack([w_in[:, :, :aw], w_in[:, :, aw:2 * aw], w_in[:, :, 2 * aw:3 * aw], w_in[:, :, 3 * aw + nh:]],
                   axis=1).astype(BF16)
    wf = jnp.repeat(w_in[:, :, 3 * aw:3 * aw + nh], rep, axis=2).astype(BF16)
    bf = jnp.repeat(b_forget, rep, axis=1).reshape(depth, 1, LANES)
    in_consts = _in_proj_constants(t["tm_in"], aw, rep)
    s5_ops = jax.vmap(_s5_operators)(ssm_a_re, ssm_a_im, ssm_log_dt, ssm_b_re, ssm_b_im, ssm_c_re, ssm_c_im, ssm_w_glu)
    h0 = jnp.concatenate([state_ssm_re.reshape(depth, nb, no, 1, sd // 2),
                          state_ssm_im.reshape(depth, nb, no, 1, sd // 2)], axis=3).reshape(depth, nb, no * sd)
    hinit = jnp.zeros((depth, n_chunks, no * sd), F32).at[:, npc:npc + nb].set(h0)
    outs = dict(logf=[], hend=[])
    kv = None
    for l in range(depth):
        x = _ffn(x, row2(ffn1_norm_pre[l]), row2(MACARON_W * ffn1_norm_post[l]), *ffn1_w, l,
                 tm=t["tm_dense"], tf=t["tf"])

        qs, *kv, uf, qa, ka, vt, logf, logf_t = _in_proj(x, mix_norm_pre.reshape(depth, 1, d), w4, wf, bf, in_consts, kv,
                                                 layer=l, depth=depth, np_rows=np_rows, ns_rows=ns_rows,
                                                 tm=t["tm_in"], tk=t["tm"])

        attn = _attn_prompt(qa, ka, vt, tq=t["tq"])

        qs = (qs * ATTN_SCALE).astype(BF16).reshape(nb, s_new, nh, HEAD_DIM)
        qs = jnp.pad(qs.transpose(0, 2, 3, 1), ((0, 0), (0, 0), (0, 0), (0, rep - s_new)))
        qbd = (qs[:, :, :, None, :] * jnp.eye(nh, dtype=BF16)[None, :, None, :, None]).reshape(nb, aw, LANES)
        attn = _attn_sample(l, qbd, cache_k, cache_v, cf, kv[4], kv[5], logf, attn,
                            np_rows=np_rows, tk=t["tk_cache"], rep=rep)

        kc, wc, vc, a16, wglu2 = s5_ops
        ssm, hend = _s5(uf, kc, wc, vc, a16, hinit, ssm_d.reshape(depth, 1, -1), wglu2, l,
                        rows_tile=t["s5_rows"], n_prompt_chunks=npc)

        x = _out_proj(x, attn, ssm, row2(attn_out_norm[l]), row2(ssm_out_norm[l]), row2(mix_norm_post[l]),
                      w_out_b, l, tm=t["tm_dense"])

        x = _ffn(x, row2(ffn2_norm_pre[l]), row2(MACARON_W * ffn2_norm_post[l]), *ffn2_w, l,
                 tm=t["tm_dense"], tf=t["tf"])

        outs["logf"].append(logf_t)
        outs["hend"].append(hend[npc - 1:npc + nb].reshape(1 + nb, no, 2, GROUPS_PER_TILE, p))

    sl_p = slice(0, np_rows)
    sl_s = slice(np_rows, np_rows + ns_rows)
    rows = lambda arrs, sl: jnp.stack([a[:, sl] for a in arrs]).transpose(0, 2, 1)
    heads = lambda a, n: a.reshape(depth, -1, n, nh, HEAD_DIM)
    states = lambda a: a.reshape(depth, -1, g, p)
    h_all = jnp.stack(outs["hend"])
    tail_at = (0, (np_rows // t["tm_in"]) * t["tm_in"] * nh, 0)
    k_prompt = lax.dynamic_update_slice(kv[0], kv[2], tail_at)
    v_prompt = lax.dynamic_update_slice(kv[1], kv[3], tail_at)
    return (x[n_meta:np_rows][None], x[sl_s].reshape(nb, s_new, d),
            heads(k_prompt, np_rows), heads(v_prompt, np_rows),
            rows(outs["logf"], sl_p).reshape(depth, 1, np_rows, nh),
            states(h_all[:, :1, :, 0]), states(h_all[:, :1, :, 1]),
            heads(kv[4], s_new), heads(kv[5], s_new),
            rows(outs["logf"], sl_s).reshape(depth, nb, s_new, nh),
            states(h_all[:, 1:, :, 0]), states(h_all[:, 1:, :, 1]))


def kernel(x_prompt, x_sample, cache_k, cache_v, cache_logf, state_ssm_re, state_ssm_im, meta_tokens, ffn1_norm_pre, ffn1_norm_post, ffn1_w_gate, ffn1_w_up, ffn1_w_down, mix_norm_pre, mix_norm_post, w_in, b_forget, ssm_a_re, ssm_a_im, ssm_log_dt, ssm_b_re, ssm_b_im, ssm_c_re, ssm_c_im, ssm_d, ssm_w_glu, attn_out_norm, ssm_out_norm, w_out, ffn2_norm_pre, ffn2_norm_post, ffn2_w_gate, ffn2_w_up, ffn2_w_down):
    return _forward(x_prompt, x_sample, cache_k, cache_v, cache_logf, state_ssm_re, state_ssm_im, meta_tokens,
                    ffn1_norm_pre, ffn1_norm_post, ffn1_w_gate, ffn1_w_up, ffn1_w_down,
                    mix_norm_pre, mix_norm_post, w_in, b_forget,
                    ssm_a_re, ssm_a_im, ssm_log_dt, ssm_b_re, ssm_b_im, ssm_c_re, ssm_c_im, ssm_d, ssm_w_glu,
                    attn_out_norm, ssm_out_norm, w_out,
                    ffn2_norm_pre, ffn2_norm_post, ffn2_w_gate, ffn2_w_up, ffn2_w_down)
```

```python
import functools
import math

import jax
import jax.numpy as jnp
from jax import lax
from jax.experimental import pallas as pl
from jax.experimental.pallas import tpu as pltpu

F32 = jnp.float32
BF16 = jnp.bfloat16

HEAD_DIM = 128
SSM_CH = 16
SSM_STATE = 64
S5_CHUNK = 16
LANES = 128
GROUPS_PER_TILE = LANES // SSM_CH
MACARON_W = 0.5
NORM_EPS = 1e-6
ATTN_SCALE = HEAD_DIM ** -0.5
LOG2E = math.log2(math.e)
NEG_INF = -1e30
VMEM_LIMIT = 56 * 1024 * 1024
N_BIAS_TERMS = 3
SCAN_UNROLL = 4
VT_ROWS = HEAD_DIM + 16
ATTN_PAIRS_PER_TRIP = 2


def _cparams(*sem):
    return pltpu.CompilerParams(dimension_semantics=sem, vmem_limit_bytes=VMEM_LIMIT)


def _rms(x, g):
    return x * lax.rsqrt(jnp.mean(x * x, axis=-1, keepdims=True) + NORM_EPS) * g


def _dot(a, b):
    return jnp.dot(a, b, preferred_element_type=F32)


def _dot_nt(a, b):
    return lax.dot_general(a, b, (((1,), (1,)), ((), ())), preferred_element_type=F32)


def _split3(x):
    hi = x.astype(BF16)
    r1 = x - hi.astype(F32)
    mid = r1.astype(BF16)
    lo = (r1 - mid.astype(F32)).astype(BF16)
    return hi, mid, lo


def _dot_exact_lhs01(sel, x):
    hi, mid, lo = _split3(x)
    return _dot(sel, hi) + _dot(sel, mid) + _dot(sel, lo)


def _iota2(shape, axis):
    return lax.broadcasted_iota(jnp.int32, shape, axis)


def _ffn_kernel(x_ref, gpre_ref, gpost_ref, wg_ref, wu_ref, wd_ref, o_ref, xn_ref, acc_ref):
    j = pl.program_id(1)

    @pl.when(j == 0)
    def _():
        xn_ref[...] = _rms(x_ref[...], gpre_ref[...]).astype(BF16)
        acc_ref[...] = jnp.zeros_like(acc_ref)

    xn = xn_ref[...]
    g = _dot(xn, wg_ref[...])
    u = _dot(xn, wu_ref[...])
    h = (g * jax.nn.sigmoid(g) * u).astype(BF16)
    acc_ref[...] += _dot(h, wd_ref[...])

    @pl.when(j == pl.num_programs(1) - 1)
    def _():
        o_ref[...] = x_ref[...] + _rms(acc_ref[...], gpost_ref[...])


def _ffn(x, g_pre, g_post, wg, wu, wd, layer, *, tm, tf):
    r, d = x.shape
    f = wg.shape[2]
    return pl.pallas_call(
        _ffn_kernel,
        grid=(r // tm, f // tf),
        in_specs=[
            pl.BlockSpec((tm, d), lambda i, j: (i, 0)),
            pl.BlockSpec((1, d), lambda i, j: (0, 0)),
            pl.BlockSpec((1, d), lambda i, j: (0, 0)),
            pl.BlockSpec((None, d, tf), lambda i, j: (layer, 0, j)),
            pl.BlockSpec((None, d, tf), lambda i, j: (layer, 0, j)),
            pl.BlockSpec((None, tf, d), lambda i, j: (layer, j, 0)),
        ],
        out_specs=pl.BlockSpec((tm, d), lambda i, j: (i, 0)),
        out_shape=jax.ShapeDtypeStruct((r, d), F32),
        scratch_shapes=[pltpu.VMEM((tm, d), BF16), pltpu.VMEM((tm, d), F32)],
        compiler_params=_cparams("parallel", "arbitrary"),
        name="ffn",
    )(x, g_pre, g_post, wg, wu, wd)


def _in_kernel(x_ref, g_ref, w_ref, wf_ref, bf_ref, tril_ref, selk_ref, selq_ref, *refs,
               n_alias, np_rows, ns_rows, rep):
    (qs_ref, kp_ref, vp_ref, kt_ref, vt_tail_ref, ks_ref, vs_ref, uf_ref, qa_ref, ka_ref, vt_ref, logf_ref,
     logf_t_ref, carry_ref) = refs[n_alias:]
    i = pl.program_id(0)
    tm = x_ref.shape[0]
    aw = w_ref.shape[2]
    nh = aw // HEAD_DIM
    nb = N_BIAS_TERMS
    n_full, n_tail = divmod(np_rows, tm)

    xn = _rms(x_ref[...], g_ref[...]).astype(BF16)
    logf = jax.nn.log_sigmoid(_dot(xn, wf_ref[...]) + bf_ref[...])
    logf_ref[...] = logf
    logf_t = jnp.transpose(logf)
    logf_t_ref[...] = jnp.concatenate([logf_t[h * rep:h * rep + 1, :] for h in range(nh)], axis=0)

    @pl.when(i == 0)
    def _():
        carry_ref[...] = jnp.zeros_like(carry_ref)

    fcum = _dot_exact_lhs01(tril_ref[...], logf) + carry_ref[...]
    carry_ref[...] = fcum[tm - 1:tm, :]
    terms = jnp.concatenate(_split3(fcum * LOG2E), axis=1)
    lane_t = _iota2((tm, aw), 1) % HEAD_DIM
    kaug = (jnp.logical_and(lane_t >= nb, lane_t < 2 * nb).astype(F32) - _dot(terms, selk_ref[...])).astype(BF16)
    qaug = ((lane_t < nb).astype(F32) + _dot(terms, selq_ref[...])).astype(BF16)

    def head(a, h):
        return a[:, h * HEAD_DIM:(h + 1) * HEAD_DIM]

    def interleave(dst_ref, zb, aug):
        for h in range(nh):
            dst_ref[:, 2 * h * HEAD_DIM:(2 * h + 1) * HEAD_DIM] = head(zb, h)
            dst_ref[:, (2 * h + 1) * HEAD_DIM:(2 * h + 2) * HEAD_DIM] = head(aug, h)

    def sample_rows(z, dst_ref, by_head):
        for tile in range(np_rows // tm, (np_rows + ns_rows - 1) // tm + 1):
            lo, hi = max(np_rows, tile * tm), min(np_rows + ns_rows, (tile + 1) * tm)

            @pl.when(i == tile)
            def _(tile=tile, lo=lo, hi=hi):
                rows = z[lo - tile * tm:hi - tile * tm, :]
                if by_head:
                    for h in range(nh):
                        dst_ref[pl.ds((lo - np_rows) * nh + h, hi - lo, stride=nh), :] = head(rows, h)
                else:
                    dst_ref[lo - np_rows:hi - np_rows, :] = rows

    def returned_rows(z, p_ref, t_ref, s_ref):
        @pl.when(i < n_full)
        def _():
            for h in range(nh):
                p_ref[pl.ds(h, tm, stride=nh), :] = head(z, h)

        @pl.when(i == n_full)
        def _():
            for h in range(nh):
                t_ref[pl.ds(h, n_tail, stride=nh), :] = head(z[:n_tail, :], h)

        sample_rows(z, s_ref, True)

    z = _dot(xn, w_ref[0])
    interleave(qa_ref, (z * (ATTN_SCALE * LOG2E)).astype(BF16), qaug)
    sample_rows(z, qs_ref, False)

    z = _dot(xn, w_ref[1])
    interleave(ka_ref, z.astype(BF16), kaug)
    returned_rows(z, kp_ref, kt_ref, ks_ref)

    z = _dot(xn, w_ref[2])
    returned_rows(z, vp_ref, vt_tail_ref, vs_ref)
    ones_row = (_iota2((VT_ROWS - HEAD_DIM, tm), 0) == 0).astype(BF16)
    for h in range(nh):
        vt_ref[h, :HEAD_DIM, :] = jnp.transpose(head(z, h)).astype(BF16)
        vt_ref[h, HEAD_DIM:, :] = ones_row

    uf_ref[...] = _dot(xn, w_ref[3])


def _in_proj_constants(tm, aw, rep):
    nb = N_BIAS_TERMS
    tril = (jnp.arange(tm)[None, :] <= jnp.arange(tm)[:, None]).astype(BF16)
    src = jnp.arange(nb * LANES)[:, None]
    dst = jnp.arange(aw)[None, :]
    head_src = src % LANES == (dst // HEAD_DIM) * rep
    selk = jnp.logical_and(head_src, dst % HEAD_DIM == src // LANES).astype(BF16)
    selq = jnp.logical_and(head_src, dst % HEAD_DIM == nb + src // LANES).astype(BF16)
    return tril, selk, selq


def _in_proj(x, g, w4, wf, bf, consts, kv_prev, *, layer, depth, np_rows, ns_rows, tm, tk):
    r, d = x.shape
    aw = w4.shape[-1]
    nh = aw // HEAD_DIM
    n_full, n_tail = divmod(np_rows, tm)
    assert n_tail > 0 and n_full > 0 and tk % tm == 0
    per_block = tk // tm
    n_alias = 0 if kv_prev is None else len(kv_prev)
    kernel_fn = functools.partial(_in_kernel, n_alias=n_alias, np_rows=np_rows, ns_rows=ns_rows, rep=LANES // nh)
    once = dict(pipeline_mode=pl.Buffered(1))
    const = lambda a: pl.BlockSpec(a.shape, lambda i: (0,) * a.ndim, **once)
    layered = lambda a: pl.BlockSpec((None,) + a.shape[1:], lambda i: (layer,) + (0,) * (a.ndim - 1), **once)
    prompt_spec = pl.BlockSpec((None, tm * nh, HEAD_DIM), lambda i: (layer, jnp.minimum(i, n_full - 1), 0))
    tail_spec = pl.BlockSpec((None, n_tail * nh, HEAD_DIM), lambda i: (layer, 0, 0))
    sample_spec = pl.BlockSpec((None, ns_rows * nh, HEAD_DIM), lambda i: (layer, 0, 0))
    prompt_shape = jax.ShapeDtypeStruct((depth, np_rows * nh, HEAD_DIM), F32)
    tail_shape = jax.ShapeDtypeStruct((depth, n_tail * nh, HEAD_DIM), F32)
    sample_shape = jax.ShapeDtypeStruct((depth, ns_rows * nh, HEAD_DIM), F32)
    return pl.pallas_call(
        kernel_fn,
        grid=(r // tm,),
        in_specs=[
            pl.BlockSpec((tm, d), lambda i: (i, 0)),
            layered(g), layered(w4), layered(wf), layered(bf), *[const(c) for c in consts],
        ] + [pl.BlockSpec(memory_space=pl.ANY)] * n_alias,
        out_specs=[
            pl.BlockSpec((ns_rows, aw), lambda i: (0, 0)),
            prompt_spec, prompt_spec, tail_spec, tail_spec, sample_spec, sample_spec,
            pl.BlockSpec((tm, aw), lambda i: (i, 0)),
            pl.BlockSpec((tm, 2 * aw), lambda i: (i, 0)),
            pl.BlockSpec((tm, 2 * aw), lambda i: (i, 0)),
            pl.BlockSpec((nh, None, VT_ROWS, tm), lambda i: (0, i // per_block, 0, i % per_block)),
            pl.BlockSpec((tm, LANES), lambda i: (i, 0)),
            pl.BlockSpec((nh, tm), lambda i: (0, i)),
        ],
        out_shape=[
            jax.ShapeDtypeStruct((ns_rows, aw), F32),
            prompt_shape, prompt_shape, tail_shape, tail_shape, sample_shape, sample_shape,
            jax.ShapeDtypeStruct((r, aw), F32),
            jax.ShapeDtypeStruct((r, 2 * aw), BF16),
            jax.ShapeDtypeStruct((r, 2 * aw), BF16),
            jax.ShapeDtypeStruct((nh, r // tk, VT_ROWS, tk), BF16),
            jax.ShapeDtypeStruct((r, LANES), F32),
            jax.ShapeDtypeStruct((nh, r), F32),
        ],
        scratch_shapes=[pltpu.VMEM((1, LANES), F32)],
        input_output_aliases={5 + len(consts) + n: 1 + n for n in range(n_alias)},
        compiler_params=_cparams("arbitrary"),
        name="in_proj",
    )(x, g, w4, wf, bf, *consts, *(kv_prev or ()))


def _attn_p_kernel(qa_ref, ka_ref, vt_ref, o_ref, sa_ref, sb_ref, acc_ref):
    qi = pl.program_id(1)
    tq = qa_ref.shape[0]
    tk = sa_ref.shape[0]
    n_diag = tq // tk
    n_off = qi * n_diag

    def produce(s_ref, kb):
        s = _dot_nt(ka_ref[pl.ds(pl.multiple_of(kb * tk, tk), tk), :], qa_ref[...])
        s_ref[...] = s
        return jnp.max(s, axis=0, keepdims=True)

    def consume(s_ref, m_blk, kb, m):
        m_new = jnp.maximum(m, m_blk)
        p = jnp.exp2(s_ref[...] - m_new)
        acc_ref[...] = jnp.exp2(m - m_new) * acc_ref[...] + _dot(vt_ref[kb], p.astype(BF16))
        return m_new

    acc_ref[...] = jnp.zeros_like(acc_ref)
    m_first = produce(sa_ref, 0)

    def pair(kb, carry):
        m_a, m = carry
        m_b = produce(sb_ref, kb + 1)
        m = consume(sa_ref, m_a, kb, m)
        m_a = produce(sa_ref, kb + 2)
        m = consume(sb_ref, m_b, kb + 1, m)
        return m_a, m

    init = (m_first, jnp.full((1, tq), NEG_INF, F32))
    def pairs(first, count, carry):
        for n in range(count):
            carry = pair(first + 2 * n, carry)
        return carry

    per_trip = 2 * ATTN_PAIRS_PER_TRIP
    carry = lax.fori_loop(0, n_off // per_trip, lambda i, c: pairs(per_trip * i, ATTN_PAIRS_PER_TRIP, c), init)
    done = (n_off // per_trip) * per_trip
    m_a, m = lax.fori_loop(0, (n_off % per_trip) // 2, lambda i, c: pair(done + 2 * i, c), carry)
    def produce_diag(s_ref, j):
        c0 = j * tk
        kb = n_off + j
        s_ref[:, :tq - c0] = _dot_nt(ka_ref[pl.ds(pl.multiple_of(kb * tk, tk), tk), :], qa_ref[c0:, :])

    def consume_diag(s_ref, j, m):
        c0 = j * tk
        w = tq - c0
        visible = _iota2((tk, w), 1) >= _iota2((tk, w), 0)
        s = jnp.where(visible, s_ref[:, :w], NEG_INF)
        m_old = m[:, c0:]
        m_new = jnp.maximum(m_old, jnp.max(s, axis=0, keepdims=True))
        p = jnp.exp2(s - m_new)
        acc_ref[:, c0:] = jnp.exp2(m_old - m_new) * acc_ref[:, c0:] + _dot(vt_ref[n_off + j], p.astype(BF16))
        return m_new if c0 == 0 else jnp.concatenate([m[:, :c0], m_new], axis=1)

    def tail(lead_block, m_lead, m):
        bufs = (sa_ref, sb_ref)
        has_lead = lead_block is not None
        for idx in range(n_diag + has_lead):
            cur = bufs[idx % 2]
            j = idx - has_lead
            if j + 1 < n_diag:
                produce_diag(bufs[(idx + 1) % 2], j + 1)
            if j < 0:
                m = consume(cur, m_lead, lead_block, m)
            else:
                m = consume_diag(cur, j, m)
        o_ref[...] = jnp.transpose(acc_ref[:HEAD_DIM, :] / acc_ref[HEAD_DIM:HEAD_DIM + 1, :])

    @pl.when(n_off % 2 == 0)
    def _():
        tail(None, None, m)

    @pl.when(n_off % 2 == 1)
    def _():
        tail(n_off - 1, m_a, m)


def _attn_prompt(qa, ka, vt, *, tq):
    nh, nkb, _, tk = vt.shape
    r = nkb * tk
    return pl.pallas_call(
        _attn_p_kernel,
        grid=(nh, r // tq),
        in_specs=[
            pl.BlockSpec((tq, 2 * HEAD_DIM), lambda h, i: (i, h)),
            pl.BlockSpec((r, 2 * HEAD_DIM), lambda h, i: (0, h)),
            pl.BlockSpec((None, nkb, VT_ROWS, tk), lambda h, i: (h, 0, 0, 0)),
        ],
        out_specs=pl.BlockSpec((tq, HEAD_DIM), lambda h, i: (i, h)),
        out_shape=jax.ShapeDtypeStruct((r, nh * HEAD_DIM), F32),
        scratch_shapes=[pltpu.VMEM((tk, tq), F32), pltpu.VMEM((tk, tq), F32), pltpu.VMEM((VT_ROWS, tq), F32)],
        compiler_params=_cparams("parallel", "arbitrary"),
        name="attn_p",
    )(qa, ka, vt)


def _attn_s_kernel(qbd_ref, ck_ref, cv_ref, cf_ref, kn_ref, vn_ref, fn_ref, attn_in_ref, o_ref,
                   m_ref, l_ref, acc_ref, suf_ref, cq_ref, *, rep, seg, nh):
    del attn_in_ref
    step = pl.program_id(1)
    s_new = fn_ref.shape[0]
    tk = cf_ref.shape[1]
    qbd = qbd_ref[...]

    def heads_side_by_side(ref, n):
        return jnp.concatenate([ref[pl.ds(h, n, stride=nh), :].astype(BF16) for h in range(nh)], axis=1)

    def update(t, v_bf):
        m_old = m_ref[...]
        m_new = jnp.maximum(m_old, jnp.max(t, axis=0, keepdims=True))
        p = jnp.exp(t - m_new)
        alpha = jnp.exp(m_old - m_new)
        l_ref[...] = alpha * l_ref[...] + jnp.sum(p, axis=0, keepdims=True)
        m_ref[...] = m_new
        alpha_col = jnp.transpose(jnp.broadcast_to(alpha, (LANES, LANES)))
        pv = _dot(jnp.transpose(p).astype(BF16), v_bf)
        for c in range(nh):
            sl = slice(c * HEAD_DIM, (c + 1) * HEAD_DIM)
            acc_ref[:, sl] = acc_ref[:, sl] * alpha_col + pv[:, sl]

    @pl.when(step == 0)
    def _():
        m_ref[...] = jnp.full_like(m_ref, NEG_INF)
        l_ref[...] = jnp.zeros_like(l_ref)
        acc_ref[...] = jnp.zeros_like(acc_ref)
        tril = (_iota2((s_new, s_new), 1) <= _iota2((s_new, s_new), 0)).astype(BF16)
        cn = _dot_exact_lhs01(tril, fn_ref[...])
        krow = _iota2((s_new, LANES), 0)
        qlane = _iota2((s_new, LANES), 1) % rep
        cq = jnp.sum(jnp.where(krow == qlane, cn, 0.0), axis=0, keepdims=True)
        cq_ref[...] = cq
        suf_ref[...] = jnp.zeros_like(suf_ref)
        st = _dot(heads_side_by_side(kn_ref, s_new), qbd)
        t = jnp.where(krow <= qlane, st + (cq - cn), NEG_INF)
        update(t, heads_side_by_side(vn_ref, s_new))

    spread = (_iota2((LANES, nh), 0) // rep == _iota2((LANES, nh), 1)).astype(BF16)
    cf_blk = jnp.transpose(_dot_exact_lhs01(spread, cf_ref[...]))
    triu = (_iota2((seg, seg), 1) > _iota2((seg, seg), 0)).astype(BF16)
    carry = suf_ref[...]
    sufs = [None] * (tk // seg)
    for sidx in reversed(range(tk // seg)):
        x = cf_blk[sidx * seg:(sidx + 1) * seg, :]
        sfx = _dot_exact_lhs01(triu, x) + carry
        sufs[sidx] = sfx
        carry = sfx[0:1, :] + x[0:1, :]
    suf_ref[...] = carry
    bias = jnp.concatenate(sufs, axis=0) + cq_ref[...]
    update(_dot(heads_side_by_side(ck_ref, tk), qbd) + bias, heads_side_by_side(cv_ref, tk))

    @pl.when(step == pl.num_programs(1) - 1)
    def _():
        l_col = jnp.transpose(jnp.broadcast_to(l_ref[...], (LANES, LANES)))
        for c in range(nh):
            rws = slice(c * rep, c * rep + s_new)
            sl = slice(c * HEAD_DIM, (c + 1) * HEAD_DIM)
            o_ref[:, sl] = acc_ref[rws, sl] / l_col[rws, :]


def _attn_sample(layer, qbd, cache_k, cache_v, cf, kf, vf, logf, attn, *, np_rows, tk, rep):
    depth, nb, past, nh, _ = cache_k.shape
    aw = nh * HEAD_DIM
    s_new = S5_CHUNK
    nkb = past // tk
    base = np_rows // s_new
    seg = min(tk, LANES)

    ck = cache_k.reshape(depth, nb, past * nh, HEAD_DIM)
    cv = cache_v.reshape(depth, nb, past * nh, HEAD_DIM)
    cache_spec = pl.BlockSpec((None, None, tk * nh, HEAD_DIM), lambda b, s: (layer, b, nkb - 1 - s, 0))
    return pl.pallas_call(
        functools.partial(_attn_s_kernel, rep=rep, seg=seg, nh=nh),
        grid=(nb, nkb),
        in_specs=[
            pl.BlockSpec((None, aw, LANES), lambda b, s: (b, 0, 0)),
            cache_spec,
            cache_spec,
            pl.BlockSpec((None, None, nh, tk), lambda b, s: (layer, b, 0, nkb - 1 - s)),
            pl.BlockSpec((None, s_new * nh, HEAD_DIM), lambda b, s: (layer, b, 0)),
            pl.BlockSpec((None, s_new * nh, HEAD_DIM), lambda b, s: (layer, b, 0)),
            pl.BlockSpec((s_new, LANES), lambda b, s: (base + b, 0)),
            pl.BlockSpec(memory_space=pl.ANY),
        ],
        out_specs=pl.BlockSpec((s_new, aw), lambda b, s: (base + b, 0)),
        out_shape=jax.ShapeDtypeStruct(attn.shape, F32),
        scratch_shapes=[
            pltpu.VMEM((1, LANES), F32), pltpu.VMEM((1, LANES), F32), pltpu.VMEM((LANES, aw), F32),
            pltpu.VMEM((1, LANES), F32), pltpu.VMEM((1, LANES), F32),
        ],
        input_output_aliases={7: 0},
        compiler_params=_cparams("parallel", "arbitrary"),
        name="attn_s",
    )(qbd, ck, cv, cf, kf, vf, logf, attn)


def _s5_kernel(u_ref, kc_ref, wc_ref, vc_ref, a_ref, hinit_ref, d_ref, wglu2_ref, o_ref, hend_ref,
               kds2_ref, wfull_ref, vfull_ref, s_ref, hprev_ref, hc_ref, *, n_prompt_chunks):
    rt = pl.program_id(1)
    ct, sd = s_ref.shape
    half = sd // 2
    L = S5_CHUNK
    P = SSM_STATE

    @pl.when(rt == 0)
    def _():
        hc_ref[...] = jnp.zeros_like(hc_ref)
        r_, c_ = _iota2((2 * P, sd), 0), _iota2((2 * P, sd), 1)
        rep_w = jnp.logical_and(r_ // P == c_ // half, r_ % P == c_ % P).astype(BF16)
        r_, c_ = _iota2((L * LANES, sd), 0), _iota2((L * LANES, sd), 1)
        same_w = (r_ // SSM_CH) % GROUPS_PER_TILE == (c_ % half) // P
        r_, c_ = _iota2((L * SSM_CH, L * LANES), 0), _iota2((L * SSM_CH, L * LANES), 1)
        rep_v = jnp.logical_and(r_ // SSM_CH == c_ // LANES, r_ % SSM_CH == c_ % SSM_CH).astype(BF16)
        r_, c_ = _iota2((sd, L * LANES), 0), _iota2((sd, L * LANES), 1)
        same_v = (r_ % half) // P == (c_ % LANES) // SSM_CH
        r_, c_ = _iota2((SSM_CH, LANES), 0), _iota2((SSM_CH, LANES), 1)
        rep_k = (r_ == c_ % SSM_CH).astype(BF16)
        r_, c_ = _iota2((L * LANES, LANES), 0), _iota2((L * LANES, LANES), 1)
        same_k = (r_ // SSM_CH) % GROUPS_PER_TILE == c_ // SSM_CH
        wfull_ref[...] = jnp.where(same_w, _dot(wc_ref[...], rep_w), 0.0).astype(BF16)
        vfull_ref[...] = jnp.where(same_v, _dot(vc_ref[...], rep_v), 0.0).astype(BF16)
        kds = jnp.where(same_k, _dot(kc_ref[...], rep_k), 0.0).astype(BF16)
        kds2_ref[:, :LANES] = kds
        kds2_ref[:LANES, LANES:] = jnp.zeros((LANES, LANES), BF16)
        kds2_ref[LANES:, LANES:] = kds[:(L - 1) * LANES, :]

    def u_at(tau):
        return u_ref[pl.ds(tau, ct, stride=L), :]

    xr = jnp.concatenate([u_at(L - 1 - j).astype(BF16) for j in range(L)], axis=1)
    s_ref[...] = _dot(xr, wfull_ref[...])

    ar = a_ref[:, :half]
    ai = a_ref[:, half:]

    def scan(c, hcar):
        cg = rt * ct + c
        reset = jnp.logical_or(cg == 0, cg >= n_prompt_chunks)
        hp = jnp.where(reset, hinit_ref[pl.ds(c, 1), :], hcar)
        hprev_ref[pl.ds(c, 1), :] = hp
        s = s_ref[pl.ds(c, 1), :]
        hr = hp[:, :half]
        hi = hp[:, half:]
        hn = jnp.concatenate([ar * hr - ai * hi + s[:, :half], ar * hi + ai * hr + s[:, half:]], axis=1)
        hend_ref[pl.ds(c, 1), :] = hn
        return hn

    hc_ref[...] = lax.fori_loop(0, ct, scan, hc_ref[...], unroll=SCAN_UNROLL)

    ystate = _dot(hprev_ref[...].astype(BF16), vfull_ref[...])
    d2 = jnp.concatenate([d_ref[...], d_ref[...]], axis=1)
    wglu2 = wglu2_ref[...]
    for tau in range(0, L, 2):
        lag_rows = (tau + 2) * LANES
        yy = _dot(xr[:, (L - 2 - tau) * LANES:], kds2_ref[:lag_rows, :])
        yy = yy + jnp.concatenate([ystate[:, (tau + 1) * LANES:(tau + 2) * LANES],
                                   ystate[:, tau * LANES:(tau + 1) * LANES]], axis=1)
        yy = yy + d2 * jnp.concatenate([u_at(tau + 1), u_at(tau)], axis=1)
        yy = 0.5 * yy * (1.0 + jnp.tanh(math.sqrt(2.0 / math.pi) * (yy + 0.044715 * (yy * yy * yy))))
        out = yy * jax.nn.sigmoid(_dot(yy.astype(BF16), wglu2))
        o_ref[pl.ds(tau + 1, ct, stride=L), :] = out[:, :LANES]
        o_ref[pl.ds(tau, ct, stride=L), :] = out[:, LANES:]


def _s5(u, kc, wc, vc, a16, hinit, d, wglu2, layer, *, rows_tile, n_prompt_chunks):
    r, sw = u.shape
    no = sw // LANES
    L = S5_CHUNK
    ct = rows_tile // L
    nrt = r // rows_tile
    sd = a16.shape[-1]
    return pl.pallas_call(
        functools.partial(_s5_kernel, n_prompt_chunks=n_prompt_chunks),
        grid=(no, nrt),
        in_specs=[
            pl.BlockSpec((rows_tile, LANES), lambda o, t: (t, o)),
            pl.BlockSpec((None, None, L * LANES, SSM_CH), lambda o, t: (layer, o, 0, 0)),
            pl.BlockSpec((None, None, L * LANES, 2 * SSM_STATE), lambda o, t: (layer, o, 0, 0)),
            pl.BlockSpec((None, None, sd, L * SSM_CH), lambda o, t: (layer, o, 0, 0)),
            pl.BlockSpec((None, None, 1, sd), lambda o, t: (layer, o, 0, 0)),
            pl.BlockSpec((None, ct, sd), lambda o, t: (layer, t, o)),
            pl.BlockSpec((None, 1, LANES), lambda o, t: (layer, 0, o)),
            pl.BlockSpec((None, None, 2 * LANES, 2 * LANES), lambda o, t: (layer, o, 0, 0)),
        ],
        out_specs=[
            pl.BlockSpec((rows_tile, LANES), lambda o, t: (t, o)),
            pl.BlockSpec((ct, sd), lambda o, t: (t, o)),
        ],
        out_shape=[
            jax.ShapeDtypeStruct((r, sw), F32),
            jax.ShapeDtypeStruct((r // L, no * sd), F32),
        ],
        scratch_shapes=[pltpu.VMEM((L * LANES, 2 * LANES), BF16),
                        pltpu.VMEM((L * LANES, sd), BF16), pltpu.VMEM((sd, L * LANES), BF16),
                        pltpu.VMEM((ct, sd), F32), pltpu.VMEM((ct, sd), F32), pltpu.VMEM((1, sd), F32)],
        compiler_params=_cparams("parallel", "arbitrary"),
        name="s5",
    )(u, kc, wc, vc, a16, hinit, d, wglu2)


def _s5_operators(a_re, a_im, log_dt, b_re, b_im, c_re, c_im, w_glu):
    g, p = a_re.shape
    c = SSM_CH
    L = S5_CHUNK
    gt = GROUPS_PER_TILE
    no = g // gt
    hp = lax.Precision.HIGHEST
    cmul = lambda xr, xi, yr, yi: (xr * yr - xi * yi, xr * yi + xi * yr)
    dt = jnp.exp(log_dt)[:, None]
    mag = jnp.exp(a_re * dt)
    ar, ai = mag * jnp.cos(a_im * dt), mag * jnp.sin(a_im * dt)
    den = a_re * a_re + a_im * a_im
    fr = ((ar - 1.0) * a_re + ai * a_im) / den
    fi = (ai * a_re - (ar - 1.0) * a_im) / den
    bbr, bbi = cmul(fr[..., None], fi[..., None], b_re, b_im)
    pr, pi = [jnp.ones_like(ar)], [jnp.zeros_like(ar)]
    for _ in range(L):
        nr, ni = cmul(pr[-1], pi[-1], ar, ai)
        pr.append(nr)
        pi.append(ni)
    pr, pi = jnp.stack(pr), jnp.stack(pi)

    abr, abi = cmul(pr[:L, :, :, None], pi[:L, :, :, None], bbr[None], bbi[None])
    kd = (jnp.einsum('gop,dgpi->gdio', c_re, abr, precision=hp)
          - jnp.einsum('gop,dgpi->gdio', c_im, abi, precision=hp))
    eye = jnp.eye(gt, dtype=F32)
    kc = kd.reshape(no, gt, L, c, c).transpose(0, 2, 1, 3, 4).reshape(no, L * LANES, c)

    wr, wi = cmul(pr[:L, :, None, :], pi[:L, :, None, :],
                  bbr.transpose(0, 2, 1)[None], bbi.transpose(0, 2, 1)[None])
    wc = jnp.stack([wr, wi], axis=3).reshape(L, no, gt, c, 2 * p)
    wc = wc.transpose(1, 0, 2, 3, 4).reshape(no, L * LANES, 2 * p)

    zr, zi = cmul(c_re.transpose(0, 2, 1)[:, :, None, :], c_im.transpose(0, 2, 1)[:, :, None, :],
                  pr[1:].transpose(1, 2, 0)[..., None], pi[1:].transpose(1, 2, 0)[..., None])
    vc = jnp.stack([zr, -zi], axis=0).reshape(2, no, gt, p, L * c)
    vc = vc.transpose(1, 0, 2, 3, 4).reshape(no, 2 * gt * p, L * c)

    a16 = jnp.concatenate([pr[L].reshape(no, 1, gt * p), pi[L].reshape(no, 1, gt * p)], axis=2)
    wg = w_glu.reshape(no, gt, c, c)
    wglu = (wg[:, :, :, None, :] * eye[None, :, None, :, None]).reshape(no, LANES, LANES)
    zero = jnp.zeros_like(wglu)
    wglu2 = jnp.concatenate([jnp.concatenate([wglu, zero], axis=2), jnp.concatenate([zero, wglu], axis=2)], axis=1)
    return kc.astype(BF16), wc.astype(BF16), vc.astype(BF16), a16, wglu2.astype(BF16)


def _out_kernel(x_ref, attn_ref, ssm_ref, ga_ref, gs_ref, gpost_ref, wa_ref, ws_ref, o_ref):
    an = _rms(attn_ref[...], ga_ref[...]).astype(BF16)
    sn = _rms(ssm_ref[...], gs_ref[...]).astype(BF16)
    mixed = _dot(an, wa_ref[...]) + _dot(sn, ws_ref[...])
    o_ref[...] = x_ref[...] + _rms(mixed, gpost_ref[...])


def _out_proj(x, attn, ssm, ga, gs, gpost, w_out, layer, *, tm):
    r, d = x.shape
    aw = attn.shape[1]
    sw = ssm.shape[1]
    assert aw == sw
    once = dict(pipeline_mode=pl.Buffered(1))
    return pl.pallas_call(
        _out_kernel,
        grid=(r // tm,),
        in_specs=[
            pl.BlockSpec((tm, d), lambda i: (i, 0)),
            pl.BlockSpec((tm, aw), lambda i: (i, 0)),
            pl.BlockSpec((tm, sw), lambda i: (i, 0)),
            pl.BlockSpec((1, aw), lambda i: (0, 0)),
            pl.BlockSpec((1, sw), lambda i: (0, 0)),
            pl.BlockSpec((1, d), lambda i: (0, 0)),
            pl.BlockSpec((None, aw, d), lambda i: (layer, 0, 0), **once),
            pl.BlockSpec((None, sw, d), lambda i: (layer, 1, 0), **once),
        ],
        out_specs=pl.BlockSpec((tm, d), lambda i: (i, 0)),
        out_shape=jax.ShapeDtypeStruct((r, d), F32),
        compiler_params=_cparams("parallel"),
        name="out_proj",
    )(x, attn, ssm, ga, gs, gpost, w_out, w_out)


def _tiles(r_min):
    tm = 512
    r = -(-r_min // tm) * tm
    s5_tiles = 4 if (r // 4) % (8 * S5_CHUNK) == 0 else 1
    tq = 3 * tm if r % (3 * tm) == 0 else tm
    tm_dense = 768 if r % 768 == 0 else tm
    return dict(r=r, tm=tm, tm_in=tm // 2, tm_dense=tm_dense, tq=tq, tf=512, s5_rows=r // s5_tiles, tk_cache=1024)


def _forward(x_prompt, x_sample, cache_k, cache_v, cache_logf, state_ssm_re, state_ssm_im, meta_tokens,
             ffn1_norm_pre, ffn1_norm_post, ffn1_w_gate, ffn1_w_up, ffn1_w_down,
             mix_norm_pre, mix_norm_post, w_in, b_forget,
             ssm_a_re, ssm_a_im, ssm_log_dt, ssm_b_re, ssm_b_im, ssm_c_re, ssm_c_im, ssm_d, ssm_w_glu,
             attn_out_norm, ssm_out_norm, w_out,
             ffn2_norm_pre, ffn2_norm_post, ffn2_w_gate, ffn2_w_up, ffn2_w_down, tiles=None):
    bsz, seq, d = x_prompt.shape
    nb, s_new, _ = x_sample.shape
    depth, _, past, nh, _ = cache_k.shape
    n_meta = meta_tokens.shape[0]
    aw = nh * HEAD_DIM
    g, p = ssm_a_re.shape[1:]
    L = S5_CHUNK
    assert bsz == 1 and s_new == L and n_meta % L == 0 and seq % L == 0 and p == SSM_STATE
    assert LANES % nh == 0 and s_new <= LANES // nh and g % GROUPS_PER_TILE == 0
    rep = LANES // nh
    no = g // GROUPS_PER_TILE
    sd = 2 * GROUPS_PER_TILE * p
    np_rows = n_meta + seq
    ns_rows = nb * s_new
    t = tiles or _tiles(np_rows + ns_rows)
    r = t["r"]
    npc = np_rows // L
    n_chunks = r // L

    x = jnp.concatenate([meta_tokens.astype(F32), x_prompt[0], x_sample.reshape(ns_rows, d),
                         jnp.zeros((r - np_rows - ns_rows, d), F32)], axis=0)
    cf = cache_logf.transpose(0, 1, 3, 2)

    row2 = lambda a: a.reshape(1, -1)
    ffn1_w = [w.astype(BF16) for w in (ffn1_w_gate, ffn1_w_up, ffn1_w_down)]
    ffn2_w = [w.astype(BF16) for w in (ffn2_w_gate, ffn2_w_up, ffn2_w_down)]
    w_out_b = w_out.astype(BF16)
    w4 = jnp.stack([w_in[:, :, :aw], w_in[:, :, aw:2 * aw], w_in[:, :, 2 * aw:3 * aw], w_in[:, :, 3 * aw + nh:]],
                   axis=1).astype(BF16)
    wf = jnp.repeat(w_in[:, :, 3 * aw:3 * aw + nh], rep, axis=2).astype(BF16)
    bf = jnp.repeat(b_forget, rep, axis=1).reshape(depth, 1, LANES)
    in_consts = _in_proj_constants(t["tm_in"], aw, rep)
    s5_ops = jax.vmap(_s5_operators)(ssm_a_re, ssm_a_im, ssm_log_dt, ssm_b_re, ssm_b_im, ssm_c_re, ssm_c_im, ssm_w_glu)
    h0 = jnp.concatenate([state_ssm_re.reshape(depth, nb, no, 1, sd // 2),
                          state_ssm_im.reshape(depth, nb, no, 1, sd // 2)], axis=3).reshape(depth, nb, no * sd)
    hinit = jnp.zeros((depth, n_chunks, no * sd), F32).at[:, npc:npc + nb].set(h0)
    outs = dict(logf=[], hend=[])
    kv = [jnp.zeros((depth, n * nh, HEAD_DIM), F32)
          for n in (np_rows, np_rows, np_rows % t["tm_in"], np_rows % t["tm_in"], ns_rows, ns_rows)]
    for l in range(depth):
        x = _ffn(x, row2(ffn1_norm_pre[l]), row2(MACARON_W * ffn1_norm_post[l]), *ffn1_w, l,
                 tm=t["tm_dense"], tf=t["tf"])

        qs, *kv, uf, qa, ka, vt, logf, logf_t = _in_proj(x, mix_norm_pre.reshape(depth, 1, d), w4, wf, bf, in_consts, kv,
                                                 layer=l, depth=depth, np_rows=np_rows, ns_rows=ns_rows,
                                                 tm=t["tm_in"], tk=t["tm"])

        attn = _attn_prompt(qa, ka, vt, tq=t["tq"])

        qs = (qs * ATTN_SCALE).astype(BF16).reshape(nb, s_new, nh, HEAD_DIM)
        qs = jnp.pad(qs.transpose(0, 2, 3, 1), ((0, 0), (0, 0), (0, 0), (0, rep - s_new)))
        qbd = (qs[:, :, :, None, :] * jnp.eye(nh, dtype=BF16)[None, :, None, :, None]).reshape(nb, aw, LANES)
        attn = _attn_sample(l, qbd, cache_k, cache_v, cf, kv[4], kv[5], logf, attn,
                            np_rows=np_rows, tk=t["tk_cache"], rep=rep)

        kc, wc, vc, a16, wglu2 = s5_ops
        ssm, hend = _s5(uf, kc, wc, vc, a16, hinit, ssm_d.reshape(depth, 1, -1), wglu2, l,
                        rows_tile=t["s5_rows"], n_prompt_chunks=npc)

        x = _out_proj(x, attn, ssm, row2(attn_out_norm[l]), row2(ssm_out_norm[l]), row2(mix_norm_post[l]),
                      w_out_b, l, tm=t["tm_dense"])

        x = _ffn(x, row2(ffn2_norm_pre[l]), row2(MACARON_W * ffn2_norm_post[l]), *ffn2_w, l,
                 tm=t["tm_dense"], tf=t["tf"])

        outs["logf"].append(logf_t)
        outs["hend"].append(hend[npc - 1:npc + nb].reshape(1 + nb, no, 2, GROUPS_PER_TILE, p))

    sl_p = slice(0, np_rows)
    sl_s = slice(np_rows, np_rows + ns_rows)
    rows = lambda arrs, sl: jnp.stack([a[:, sl] for a in arrs]).transpose(0, 2, 1)
    heads = lambda a, n: a.reshape(depth, -1, n, nh, HEAD_DIM)
    states = lambda a: a.reshape(depth, -1, g, p)
    h_all = jnp.stack(outs["hend"])
    tail_at = (0, (np_rows // t["tm_in"]) * t["tm_in"] * nh, 0)
    k_prompt = lax.dynamic_update_slice(kv[0], kv[2], tail_at)
    v_prompt = lax.dynamic_update_slice(kv[1], kv[3], tail_at)
    return (x[n_meta:np_rows][None], x[sl_s].reshape(nb, s_new, d),
            heads(k_prompt, np_rows), heads(v_prompt, np_rows),
            rows(outs["logf"], sl_p).reshape(depth, 1, np_rows, nh),
            states(h_all[:, :1, :, 0]), states(h_all[:, :1, :, 1]),
            heads(kv[4], s_new), heads(kv[5], s_new),
            rows(outs["logf"], sl_s).reshape(depth, nb, s_new, nh),
            states(h_all[:, 1:, :, 0]), states(h_all[:, 1:, :, 1]))


def kernel(x_prompt, x_sample, cache_k, cache_v, cache_logf, state_ssm_re, state_ssm_im, meta_tokens, ffn1_norm_pre, ffn1_norm_post, ffn1_w_gate, ffn1_w_up, ffn1_w_down, mix_norm_pre, mix_norm_post, w_in, b_forget, ssm_a_re, ssm_a_im, ssm_log_dt, ssm_b_re, ssm_b_im, ssm_c_re, ssm_c_im, ssm_d, ssm_w_glu, attn_out_norm, ssm_out_norm, w_out, ffn2_norm_pre, ffn2_norm_post, ffn2_w_gate, ffn2_w_up, ffn2_w_down):
    return _forward(x_prompt, x_sample, cache_k, cache_v, cache_logf, state_ssm_re, state_ssm_im, meta_tokens,
                    ffn1_norm_pre, ffn1_norm_post, ffn1_w_gate, ffn1_w_up, ffn1_w_down,
                    mix_norm_pre, mix_norm_post, w_in, b_forget,
                    ssm_a_re, ssm_a_im, ssm_log_dt, ssm_b_re, ssm_b_im, ssm_c_re, ssm_c_im, ssm_d, ssm_w_glu,
                    attn_out_norm, ssm_out_norm, w_out,
                    ffn2_norm_pre, ffn2_norm_post, ffn2_w_gate, ffn2_w_up, ffn2_w_down)
```
